```python
import jax, jax.numpy as jnp
from jax import lax
import numpy as np

D_MODEL = 1024
BATCH = 16
SEQ = 4096
DEPTH = 1

CTX_LEN = 256
GRID_W = 64
EPS = 1e-6
N_MOD = 6
GDN_HEADS = 8
GDN_HEAD_DIM = 128
GDN_WIDTH = GDN_HEADS * GDN_HEAD_DIM
GDN_CHUNK = 64
CONV_SIZE = 3
GLA_HEADS = 4
GLA_KEY_DIM = D_MODEL // 2
GLA_VALUE_DIM = D_MODEL
GLA_DK = GLA_KEY_DIM // GLA_HEADS
GLA_DV = GLA_VALUE_DIM // GLA_HEADS
GLA_GATE_RANK = 16
GLA_GATE_NORMALIZER = 16.0
GLA_CHUNK = 64
N_EXPERTS = 32
TOP_K = 4
D_EXPERT = D_MODEL
SWIGLU_LIMIT = 7.0
SWIGLU_ALPHA = 1.702
MOE_BLOCK = 128
IN_SPLITS = (GDN_WIDTH, GDN_WIDTH, GDN_WIDTH, GDN_WIDTH, 2 * GDN_HEADS, 2 * GDN_HEADS,
             GLA_KEY_DIM, GLA_KEY_DIM, GLA_VALUE_DIM, GLA_VALUE_DIM, 2 * GLA_GATE_RANK, 2 * D_MODEL)
D_IN = 4 * GDN_WIDTH + 4 * GDN_HEADS + 2 * GLA_KEY_DIM + 2 * GLA_VALUE_DIM + 2 * GLA_GATE_RANK + 2 * D_MODEL

kernel_name = 'hybrid_gdn_gla_moe_prefix_dit'

F32 = jnp.float32


def rms_norm(x, g):
    xf = x.astype(F32)
    y = xf * lax.rsqrt(jnp.mean(xf * xf, axis=-1, keepdims=True) + EPS)
    return (y * g.astype(F32)).astype(x.dtype)


def l2_normalize(t):
    tf = t.astype(F32)
    return tf * lax.rsqrt(jnp.sum(tf * tf, axis=-1, keepdims=True) + EPS)


def modulate(h, shift, scale):
    return h * (1.0 + scale) + shift


def split_in(p):
    idx = np.cumsum(IN_SPLITS)[:-1].tolist()
    return jnp.split(p, idx, axis=-1)


def to_heads(t, n_heads):
    b, l, _ = t.shape
    return t.reshape(b, l, n_heads, -1).transpose(0, 2, 1, 3)


def from_heads(t):
    return t.transpose(0, 2, 1, 3)


def depthwise_conv2d(img, w):
    ch = img.shape[-1]
    return lax.conv_general_dilated(img, w[:, :, None, :].astype(img.dtype), window_strides=(1, 1), padding='SAME',
                                    dimension_numbers=('NHWC', 'HWIO', 'NHWC'), feature_group_count=ch)


def conv_latent(t, conv_w):
    b, l, ch = t.shape
    rows = l // GRID_W
    return depthwise_conv2d(t.reshape(b, rows, GRID_W, ch), conv_w).reshape(b, l, ch)


def conv_context(t, conv_w):
    mid = CONV_SIZE // 2
    return depthwise_conv2d(t[:, None], conv_w[mid:mid + 1])[:, 0]


def gated_delta_chunked(q, k, v, g, beta, s0):
    b, h, l, dk = q.shape
    dv = v.shape[-1]
    c = GDN_CHUNK
    n = l // c
    q = q.astype(F32).reshape(b, h, n, c, dk) * (dk ** -0.5)
    k = k.astype(F32).reshape(b, h, n, c, dk)
    v = v.astype(F32).reshape(b, h, n, c, dv)
    beta = beta.astype(F32).reshape(b, h, n, c)
    g = jnp.cumsum(g.astype(F32).reshape(b, h, n, c), axis=-1)
    incl = jnp.tril(jnp.ones((c, c), bool))
    strict = jnp.tril(jnp.ones((c, c), bool), -1)
    decay = jnp.exp(jnp.where(incl, g[..., :, None] - g[..., None, :], -jnp.inf))
    kb = k * beta[..., None]
    m = jnp.where(strict, jnp.einsum('bhnid,bhnjd->bhnij', kb, k) * decay, 0.0)
    a = m + jnp.eye(c, dtype=F32)
    rhs = jnp.concatenate([v * beta[..., None], kb * jnp.exp(g)[..., None]], axis=-1)
    sol = lax.linalg.triangular_solve(a, rhs, left_side=True, lower=True, unit_diagonal=True)
    u, w = sol[..., :dv], sol[..., dv:]
    attn = jnp.einsum('bhnid,bhnjd->bhnij', q, k) * decay
    qg = q * jnp.exp(g)[..., None]
    kd = k * jnp.exp(g[..., -1:] - g)[..., None]
    gl = jnp.exp(g[..., -1])
    xs = tuple(jnp.moveaxis(t, 2, 0) for t in (u, w, attn, qg, kd, gl))

    def step(s, xs_n):
        u_n, w_n, attn_n, qg_n, kd_n, gl_n = xs_n
        v_new = u_n - jnp.einsum('bhck,bhkv->bhcv', w_n, s)
        o_n = jnp.einsum('bhck,bhkv->bhcv', qg_n, s) + jnp.einsum('bhij,bhjv->bhiv', attn_n, v_new)
        s = s * gl_n[..., None, None] + jnp.einsum('bhck,bhcv->bhkv', kd_n, v_new)
        return s, o_n

    s, o = lax.scan(step, s0.astype(F32), xs)
    return jnp.moveaxis(o, 0, 2).reshape(b, h, l, dv), s


def gla_chunked(q, k, v, log_a, s0):
    b, h, l, dk = q.shape
    dv = v.shape[-1]
    c = GLA_CHUNK
    n = l // c
    q = q.astype(F32).reshape(b, h, n, c, dk) * (dk ** -0.5)
    k = k.astype(F32).reshape(b, h, n, c, dk)
    v = v.astype(F32).reshape(b, h, n, c, dv)
    gcum = jnp.cumsum(log_a.astype(F32).reshape(b, h, n, c, dk), axis=-2)
    ref = gcum[..., c // 2:c // 2 + 1, :]
    incl = jnp.tril(jnp.ones((c, c), bool))
    attn = jnp.einsum('bhnik,bhnjk->bhnij', q * jnp.exp(gcum - ref), k * jnp.exp(ref - gcum))
    attn = jnp.where(incl, attn, 0.0)
    intra = jnp.einsum('bhnij,bhnjv->bhniv', attn, v)
    qg = q * jnp.exp(gcum)
    kd = k * jnp.exp(gcum[..., -1:, :] - gcum)
    gl = jnp.exp(gcum[..., -1, :])
    xs = tuple(jnp.moveaxis(t, 2, 0) for t in (intra, qg, kd, v, gl))

    def step(s, xs_n):
        intra_n, qg_n, kd_n, v_n, gl_n = xs_n
        o_n = intra_n + jnp.einsum('bhck,bhkv->bhcv', qg_n, s)
        s = s * gl_n[..., :, None] + jnp.einsum('bhck,bhcv->bhkv', kd_n, v_n)
        return s, o_n

    s, o = lax.scan(step, s0.astype(F32), xs)
    return jnp.moveaxis(o, 0, 2).reshape(b, h, l, dv), s


def flip_seq(ts, flip):
    return tuple(jnp.flip(t, axis=2) for t in ts) if flip else tuple(ts)


def bidirectional_prefix_scan(scan_fn, ctx_shared, lat_shared, ctx_gates, lat_gates, s0, need_ctx):
    lat_outs, ctx_outs = [], []
    for direction, flip in enumerate((False, True)):
        o_c, s_c = scan_fn(*flip_seq(ctx_shared, flip), *flip_seq(ctx_gates[direction], flip), s0)
        o_l, _ = scan_fn(*flip_seq(lat_shared, flip), *flip_seq(lat_gates[direction], flip), s_c)
        lat_outs.append(flip_seq((o_l,), flip)[0])
        if need_ctx:
            ctx_outs.append(flip_seq((o_c,), flip)[0])
    o_lat = lat_outs[0] + lat_outs[1]
    o_ctx = ctx_outs[0] + ctx_outs[1] if need_ctx else None
    return o_lat, o_ctx


def gdn_inputs(parts, qkv, a_log_f, a_log_b, dt_bias_f, dt_bias_b):
    q, k, v = jnp.split(qkv, 3, axis=-1)
    q = l2_normalize(to_heads(q, GDN_HEADS))
    k = l2_normalize(to_heads(k, GDN_HEADS))
    v = to_heads(v, GDN_HEADS)
    beta = jax.nn.sigmoid(parts[4].astype(F32)).transpose(0, 2, 1)
    a = parts[5].astype(F32).transpose(0, 2, 1)
    hh = GDN_HEADS
    g_f = -jnp.exp(a_log_f.astype(F32))[:, None] * jax.nn.softplus(a[:, :hh] + dt_bias_f.astype(F32)[:, None])
    g_b = -jnp.exp(a_log_b.astype(F32))[:, None] * jax.nn.softplus(a[:, hh:] + dt_bias_b.astype(F32)[:, None])
    return (q, k, v), ((g_f, beta[:, :hh]), (g_b, beta[:, hh:]))


def gla_log_gate(z, w, bias):
    return to_heads(jax.nn.log_sigmoid((z @ w + bias).astype(F32)) / GLA_GATE_NORMALIZER, GLA_HEADS)


def gla_inputs(parts, w_f, w_b, b_f, b_b):
    q = to_heads(parts[6], GLA_HEADS)
    k = to_heads(parts[7], GLA_HEADS)
    v = to_heads(parts[8], GLA_HEADS)
    lr = parts[10]
    g_f = gla_log_gate(lr[..., :GLA_GATE_RANK], w_f, b_f)
    g_b = gla_log_gate(lr[..., GLA_GATE_RANK:], w_b, b_b)
    return (q, k, v), ((g_f,), (g_b,))


def merge_branches(parts, o_a, o_b, gdn_norm_g, gla_norm_g, w_out_a, w_out_b, w_out):
    dtype = parts[0].dtype
    bsz, l = parts[0].shape[:2]
    za = parts[3].reshape(bsz, l, GDN_HEADS, GDN_HEAD_DIM).astype(F32)
    ya = (rms_norm(from_heads(o_a), gdn_norm_g) * jax.nn.silu(za)).astype(dtype).reshape(bsz, l, GDN_WIDTH) @ w_out_a
    rb = parts[9].reshape(bsz, l, GLA_HEADS, GLA_DV).astype(F32)
    yb = (rms_norm(from_heads(o_b), gla_norm_g) * jax.nn.silu(rb)).astype(dtype).reshape(bsz, l, GLA_VALUE_DIM) @ w_out_b
    gates = jax.nn.sigmoid(parts[11])
    y = gates[..., :D_MODEL] * ya + gates[..., D_MODEL:] * yb
    return y @ w_out


def token_mixer(h_lat, h_ctx, need_ctx, w_in, conv_w, a_log_f, a_log_b, dt_bias_f, dt_bias_b, gdn_norm_g,
                gla_gate_w_f, gla_gate_w_b, gla_gate_b_f, gla_gate_b_b, gla_norm_g, w_out_a, w_out_b, w_out):
    p_lat = h_lat @ w_in
    p_ctx = h_ctx @ w_in
    parts_lat, parts_ctx = split_in(p_lat), split_in(p_ctx)
    bsz = h_lat.shape[0]
    qkv_lat = jax.nn.silu(conv_latent(p_lat[..., :3 * GDN_WIDTH], conv_w))
    qkv_ctx = jax.nn.silu(conv_context(p_ctx[..., :3 * GDN_WIDTH], conv_w))
    a_lat, a_lat_g = gdn_inputs(parts_lat, qkv_lat, a_log_f, a_log_b, dt_bias_f, dt_bias_b)
    a_ctx, a_ctx_g = gdn_inputs(parts_ctx, qkv_ctx, a_log_f, a_log_b, dt_bias_f, dt_bias_b)
    s0_a = jnp.zeros((bsz, GDN_HEADS, GDN_HEAD_DIM, GDN_HEAD_DIM), F32)
    o_a_lat, o_a_ctx = bidirectional_prefix_scan(gated_delta_chunked, a_ctx, a_lat, a_ctx_g, a_lat_g, s0_a, need_ctx)
    b_lat, b_lat_g = gla_inputs(parts_lat, gla_gate_w_f, gla_gate_w_b, gla_gate_b_f, gla_gate_b_b)
    b_ctx, b_ctx_g = gla_inputs(parts_ctx, gla_gate_w_f, gla_gate_w_b, gla_gate_b_f, gla_gate_b_b)
    s0_b = jnp.zeros((bsz, GLA_HEADS, GLA_DK, GLA_DV), F32)
    o_b_lat, o_b_ctx = bidirectional_prefix_scan(gla_chunked, b_ctx, b_lat, b_ctx_g, b_lat_g, s0_b, need_ctx)
    merge_w = (gdn_norm_g, gla_norm_g, w_out_a, w_out_b, w_out)
    y_lat = merge_branches(parts_lat, o_a_lat, o_b_lat, *merge_w)
    y_ctx = merge_branches(parts_ctx, o_a_ctx, o_b_ctx, *merge_w) if need_ctx else None
    return y_lat, y_ctx


def moe_ffn(h, w_router, b_router, w_gu, b_gu, w_down, b_down):
    t, d = h.shape
    n_assign = t * TOP_K
    n_blocks = -(-(n_assign + N_EXPERTS * (MOE_BLOCK - 1)) // MOE_BLOCK)
    logits = (h @ w_router + b_router).astype(F32)
    top_logit, top_idx = lax.top_k(logits, TOP_K)
    top_p = jax.nn.softmax(top_logit, axis=-1)
    flat_e = top_idx.reshape(-1).astype(jnp.int32)
    order = jnp.argsort(flat_e)
    sorted_e = flat_e[order]
    sorted_tok = (order // TOP_K).astype(jnp.int32)
    sorted_p = top_p.reshape(-1)[order]
    counts = jnp.bincount(flat_e, length=N_EXPERTS).astype(jnp.int32)
    start = jnp.cumsum(counts) - counts
    padded = (counts + MOE_BLOCK - 1) // MOE_BLOCK * MOE_BLOCK
    pad_end = jnp.cumsum(padded)
    pad_start = pad_end - padded
    dest = pad_start[sorted_e] + jnp.arange(n_assign, dtype=jnp.int32) - start[sorted_e]
    slot_tok = jnp.full((n_blocks * MOE_BLOCK,), t, jnp.int32).at[dest].set(sorted_tok)
    slot_p = jnp.zeros((n_blocks * MOE_BLOCK,), F32).at[dest].set(sorted_p)
    block_start = jnp.arange(n_blocks, dtype=jnp.int32) * MOE_BLOCK
    block_expert = jnp.minimum(jnp.searchsorted(pad_end, block_start, side='right'), N_EXPERTS - 1)
    h_pad = jnp.concatenate([h, jnp.zeros((1, d), h.dtype)], axis=0)

    def expert_block(args):
        tok, p, e = args
        xb = h_pad[tok]
        gu = xb @ w_gu[e] + b_gu[e]
        glu = jnp.minimum(gu[:, :D_EXPERT], SWIGLU_LIMIT)
        lin = jnp.clip(gu[:, D_EXPERT:], -SWIGLU_LIMIT, SWIGLU_LIMIT)
        act = glu * jax.nn.sigmoid(SWIGLU_ALPHA * glu) * (lin + 1.0)
        return (act @ w_down[e] + b_down[e]) * p[:, None].astype(h.dtype)

    yb = lax.map(expert_block, (slot_tok.reshape(n_blocks, MOE_BLOCK), slot_p.reshape(n_blocks, MOE_BLOCK), block_expert))
    y = jax.ops.segment_sum(yb.reshape(-1, d), slot_tok, num_segments=t + 1)
    return y[:t].astype(h.dtype)


def setup_inputs(seed: int = 0) -> dict:
    key = jax.random.key(seed)
    ks = iter(jax.random.split(key, 40))
    nrm = lambda shape, s: jax.random.normal(next(ks), shape, F32) * s
    gain = lambda shape: 1.0 + nrm(shape, 0.02)
    dd = D_MODEL
    u_dt = jax.random.uniform(next(ks), (2, DEPTH, GDN_HEADS), F32)
    dt = jnp.exp(u_dt * (jnp.log(0.1) - jnp.log(1e-3)) + jnp.log(1e-3))
    dt_bias = dt + jnp.log(-jnp.expm1(-dt))
    a_log = jnp.log(jax.random.uniform(next(ks), (2, DEPTH, GDN_HEADS), F32, 1.0, 16.0))
    return {
        'x': nrm((BATCH, SEQ, dd), 1.0),
        'c': nrm((BATCH, dd), 1.0),
        'ctx': nrm((BATCH, CTX_LEN, dd), 1.0),
        'c_ctx': nrm((dd,), 1.0),
        'w_mod': nrm((DEPTH, dd, N_MOD * dd), 0.5 * dd ** -0.5),
        'b_mod': nrm((DEPTH, N_MOD * dd), 0.02),
        'norm_mix_g': gain((DEPTH, dd)),
        'norm_ffn_g': gain((DEPTH, dd)),
        'w_in': nrm((DEPTH, dd, D_IN), dd ** -0.5),
        'conv_w': nrm((DEPTH, CONV_SIZE, CONV_SIZE, 3 * GDN_WIDTH), 1.0 / CONV_SIZE),
        'a_log_f': a_log[0],
        'a_log_b': a_log[1],
        'dt_bias_f': dt_bias[0],
        'dt_bias_b': dt_bias[1],
        'gdn_norm_g': gain((DEPTH, GDN_HEAD_DIM)),
        'gla_gate_w_f': nrm((DEPTH, GLA_GATE_RANK, GLA_KEY_DIM), GLA_GATE_RANK ** -0.5),
        'gla_gate_w_b': nrm((DEPTH, GLA_GATE_RANK, GLA_KEY_DIM), GLA_GATE_RANK ** -0.5),
        'gla_gate_b_f': nrm((DEPTH, GLA_KEY_DIM), 0.1),
        'gla_gate_b_b': nrm((DEPTH, GLA_KEY_DIM), 0.1),
        'gla_norm_g': gain((DEPTH, GLA_DV)),
        'w_out_a': nrm((DEPTH, GDN_WIDTH, dd), GDN_WIDTH ** -0.5),
        'w_out_b': nrm((DEPTH, GLA_VALUE_DIM, dd), GLA_VALUE_DIM ** -0.5),
        'w_out': nrm((DEPTH, dd, dd), dd ** -0.5),
        'w_router': nrm((DEPTH, dd, N_EXPERTS), dd ** -0.5),
        'b_router': nrm((DEPTH, N_EXPERTS), 0.01),
        'w_gu': nrm((DEPTH, N_EXPERTS, dd, 2 * D_EXPERT), dd ** -0.5),
        'b_gu': nrm((DEPTH, N_EXPERTS, 2 * D_EXPERT), 0.01),
        'w_down': nrm((DEPTH, N_EXPERTS, D_EXPERT, dd), D_EXPERT ** -0.5),
        'b_down': nrm((DEPTH, N_EXPERTS, dd), 0.01),
        'final_norm_g': gain((dd,)),
    }


def reference(x, c, ctx, c_ctx, w_mod, b_mod, norm_mix_g, norm_ffn_g, w_in, conv_w, a_log_f, a_log_b, dt_bias_f,
              dt_bias_b, gdn_norm_g, gla_gate_w_f, gla_gate_w_b, gla_gate_b_f, gla_gate_b_b, gla_norm_g, w_out_a,
              w_out_b, w_out, w_router, b_router, w_gu, b_gu, w_down, b_down, final_norm_g):
    for layer in range(DEPTH):
        need_ctx = layer < DEPTH - 1
        mod_lat = [m[:, None, :] for m in jnp.split(jax.nn.silu(c) @ w_mod[layer] + b_mod[layer], N_MOD, axis=-1)]
        mod_ctx = jnp.split(jax.nn.silu(c_ctx) @ w_mod[layer] + b_mod[layer], N_MOD, axis=-1)
        sh1, sc1, g1, sh2, sc2, g2 = mod_lat
        csh1, csc1, cg1, csh2, csc2, cg2 = mod_ctx
        h_lat = modulate(rms_norm(x, norm_mix_g[layer]), sh1, sc1)
        h_ctx = modulate(rms_norm(ctx, norm_mix_g[layer]), csh1, csc1)
        y_lat, y_ctx = token_mixer(h_lat, h_ctx, need_ctx, w_in[layer], conv_w[layer], a_log_f[layer], a_log_b[layer],
                                   dt_bias_f[layer], dt_bias_b[layer], gdn_norm_g[layer], gla_gate_w_f[layer],
                                   gla_gate_w_b[layer], gla_gate_b_f[layer], gla_gate_b_b[layer], gla_norm_g[layer],
                                   w_out_a[layer], w_out_b[layer], w_out[layer])
        x = x + g1 * y_lat
        h = modulate(rms_norm(x, norm_ffn_g[layer]), sh2, sc2).reshape(-1, D_MODEL)
        moe_w = (w_router[layer], b_router[layer], w_gu[layer], b_gu[layer], w_down[layer], b_down[layer])
        if need_ctx:
            ctx = ctx + cg1 * y_ctx
            hc = modulate(rms_norm(ctx, norm_ffn_g[layer]), csh2, csc2).reshape(-1, D_MODEL)
            n_lat = h.shape[0]
            y = moe_ffn(jnp.concatenate([h, hc], axis=0), *moe_w)
            x = x + g2 * y[:n_lat].reshape(x.shape)
            ctx = ctx + cg2 * y[n_lat:].reshape(ctx.shape)
        else:
            x = x + g2 * moe_ffn(h, *moe_w).reshape(x.shape)
    return rms_norm(x, final_norm_g)
```

```python
import functools

import jax
import jax.numpy as jnp
from jax import lax
from jax.experimental import pallas as pl
from jax.experimental.pallas import tpu as pltpu

F32 = jnp.float32
BF16 = jnp.bfloat16
HIGHEST = lax.Precision.HIGHEST

EPS = 1e-6
GRID_W = 64
CHUNK = 64
TOK_BLOCK = 256
CONV_ROWS = 512
GDN_HEADS = 8
GDN_DIM = 128
GLA_HEADS = 4
GLA_DK = 128
GLA_DV = 256
GLA_RANK = 16
GLA_NORMALIZER = 16.0
N_EXPERTS = 32
TOP_K = 4
SWIGLU_LIMIT = 7.0
SWIGLU_ALPHA = 1.702
EXPERT_BLOCK = 512
GATHER_ROWS = 1024
LANE = 128
SUBLANE = 8
VMEM_LIMIT = 56 * 1024 * 1024


def _silu(t):
    return t * jax.nn.sigmoid(t)


def _softplus(t):
    return jnp.maximum(t, 0.0) + jnp.log1p(jnp.exp(-jnp.abs(t)))


def _dot(a, b, **kw):
    return jnp.dot(a, b, preferred_element_type=F32, **kw)


def _dot_nt(a, b, **kw):
    return lax.dot_general(a, b, (((1,), (1,)), ((), ())), preferred_element_type=F32, **kw)


def _dot_tn(a, b, **kw):
    return lax.dot_general(a, b, (((0,), (0,)), ((), ())), preferred_element_type=F32, **kw)


def _store_tile_rows(ref, val):
    rows, width = val.shape
    groups = width // LANE
    for s in range(groups):
        ref[pl.ds(s, rows, stride=groups), :] = val[:, s * LANE:(s + 1) * LANE]


def _load_tile_rows(ref, rows, groups, first=0, stride=None):
    stride = groups if stride is None else stride
    return jnp.concatenate([ref[pl.ds(first + s, rows, stride=stride), :] for s in range(groups)], axis=-1)


def _mod_kernel(c_ref, w_ref, b_ref, o_ref):
    o_ref[...] = _dot(_silu(c_ref[...]), w_ref[...], precision=HIGHEST) + b_ref[...]


def _modulation(cc, w_mod, b_mod):
    rows, d = cc.shape
    n = w_mod.shape[1]
    tn = n // 4
    return pl.pallas_call(
        _mod_kernel,
        out_shape=jax.ShapeDtypeStruct((rows, n), F32),
        grid=(n // tn,),
        in_specs=[pl.BlockSpec((rows, d), lambda j: (0, 0)),
                  pl.BlockSpec((d, tn), lambda j: (0, j)),
                  pl.BlockSpec((1, tn), lambda j: (0, j))],
        out_specs=pl.BlockSpec((rows, tn), lambda j: (0, j)),
        compiler_params=pltpu.CompilerParams(vmem_limit_bytes=VMEM_LIMIT),
        name="mod",
    )(cc, w_mod, b_mod.reshape(1, n))


def _inproj_kernel(x_ref, ctx_ref, shl_ref, scl_ref, shc_ref, scc_ref, g_ref, w_ref, ws_ref, ca_ref, cb_ref,
                   p_ref, gd_ref, lr_ref):
    is_ctx = pl.program_id(1) == 0
    xin = jnp.where(is_ctx, ctx_ref[0], x_ref[0])
    shift = jnp.where(is_ctx, shc_ref[...], shl_ref[0])
    scale = jnp.where(is_ctx, scc_ref[...], scl_ref[0])
    ms = jnp.mean(xin * xin, axis=-1, keepdims=True)
    h = (xin * lax.rsqrt(ms + EPS) * g_ref[...]) * (1.0 + scale) + shift
    hb = h.astype(BF16)
    n_main = w_ref.shape[1]
    for j in range(n_main // 1024):
        p_ref[0, :, j * 1024:(j + 1) * 1024] = _dot(hb, w_ref[:, j * 1024:(j + 1) * 1024])
    sm = _dot(h, ws_ref[...], precision=HIGHEST)
    raw = sm[:, :LANE]
    lane = lax.broadcasted_iota(jnp.int32, raw.shape, 1)
    beta = jax.nn.sigmoid(raw)
    g = ca_ref[...] * _softplus(raw + cb_ref[...])
    gd_ref[0] = jnp.where(lane < 2 * GDN_HEADS, beta, jnp.where(lane < 4 * GDN_HEADS, g, 0.0))
    lr_ref[0] = sm[:, LANE:]


def _input_projection(x, ctx, sh_lat, sc_lat, sh_ctx, sc_ctx, norm_g, w_main, w_small, coef_a, coef_b):
    b, l, d = x.shape
    n_blk = l // TOK_BLOCK + 1
    n_main = w_main.shape[1]
    tot = n_blk * TOK_BLOCK
    row3 = lambda i, s: (i, 0, 0)
    const2 = lambda i, s: (0, 0)
    blk = lambda i, s: (i, s, 0)
    return pl.pallas_call(
        _inproj_kernel,
        out_shape=(jax.ShapeDtypeStruct((b, tot, n_main), F32),
                   jax.ShapeDtypeStruct((b, tot, LANE), F32),
                   jax.ShapeDtypeStruct((b, tot, LANE), F32)),
        grid=(b, n_blk),
        in_specs=[pl.BlockSpec((1, TOK_BLOCK, d), lambda i, s: (i, jnp.maximum(s - 1, 0), 0)),
                  pl.BlockSpec((1, TOK_BLOCK, d), row3),
                  pl.BlockSpec((1, 1, d), row3), pl.BlockSpec((1, 1, d), row3),
                  pl.BlockSpec((1, d), const2), pl.BlockSpec((1, d), const2),
                  pl.BlockSpec((1, d), const2),
                  pl.BlockSpec((d, n_main), const2, pipeline_mode=pl.Buffered(1)),
                  pl.BlockSpec((d, 2 * LANE), const2),
                  pl.BlockSpec((1, LANE), const2), pl.BlockSpec((1, LANE), const2)],
        out_specs=(pl.BlockSpec((1, TOK_BLOCK, n_main), blk),
                   pl.BlockSpec((1, TOK_BLOCK, LANE), blk),
                   pl.BlockSpec((1, TOK_BLOCK, LANE), blk)),
        compiler_params=pltpu.CompilerParams(dimension_semantics=("arbitrary", "arbitrary"),
                                             vmem_limit_bytes=VMEM_LIMIT),
        name="inproj",
    )(x, ctx, sh_lat, sc_lat, sh_ctx, sc_ctx, norm_g, w_main, w_small, coef_a, coef_b)


def _conv_kernel(p_ref, w_ref, o_ref, pad_ref, *, n_ctx, n_lat, tc, qk_cols):
    is_qk = pl.program_id(1) * tc < qk_cols
    w = w_ref[...]

    def post(y):
        s = _silu(y)
        outs = []
        for hh in range(tc // GDN_DIM):
            sh = s[:, hh * GDN_DIM:(hh + 1) * GDN_DIM]
            fac = lax.rsqrt(jnp.sum(sh * sh, axis=-1, keepdims=True) + EPS)
            outs.append(sh * jnp.where(is_qk, fac, 1.0))
        return outs[0] if len(outs) == 1 else jnp.concatenate(outs, axis=-1)

    xc = p_ref[0, 0:n_ctx, :]
    ridx = lax.broadcasted_iota(jnp.int32, (n_ctx, 1), 0)
    yc = (jnp.where(ridx == 0, 0.0, pltpu.roll(xc * w[3:4], 1, 0)) + xc * w[4:5]
          + jnp.where(ridx == n_ctx - 1, 0.0, pltpu.roll(xc * w[5:6], n_ctx - 1, 0)))
    o_ref[0, 0:n_ctx, :] = post(yc)

    zeros = jnp.zeros((GRID_W, tc), F32)
    pad_ref[0:GRID_W, :] = zeros
    pad_ref[GRID_W + n_lat:GRID_W + n_lat + GRID_W, :] = zeros
    pad_ref[GRID_W:GRID_W + n_lat, :] = p_ref[0, n_ctx:n_ctx + n_lat, :]
    rows = min(CONV_ROWS, n_lat)
    col = lax.broadcasted_iota(jnp.int32, (rows, 1), 0) % GRID_W

    def body(i, carry):
        s = pl.multiple_of(i * rows, rows)
        up = pad_ref[pl.ds(s, rows), :]
        mid = pad_ref[pl.ds(s + GRID_W, rows), :]
        dn = pad_ref[pl.ds(s + 2 * GRID_W, rows), :]
        a0 = up * w[0:1] + mid * w[3:4] + dn * w[6:7]
        a1 = up * w[1:2] + mid * w[4:5] + dn * w[7:8]
        a2 = up * w[2:3] + mid * w[5:6] + dn * w[8:9]
        y = (jnp.where(col == 0, 0.0, pltpu.roll(a0, 1, 0)) + a1
             + jnp.where(col == GRID_W - 1, 0.0, pltpu.roll(a2, rows - 1, 0)))
        o_ref[0, pl.ds(n_ctx + s, rows), :] = post(y)
        return carry

    lax.fori_loop(0, n_lat // rows, body, 0)


def _conv_qkv(p, conv_w9, n_ctx, n_lat):
    b, tot, _ = p.shape
    c = conv_w9.shape[1]
    tc = 2 * GDN_DIM
    kern = functools.partial(_conv_kernel, n_ctx=n_ctx, n_lat=n_lat, tc=tc, qk_cols=2 * GDN_HEADS * GDN_DIM)
    return pl.pallas_call(
        kern,
        out_shape=jax.ShapeDtypeStruct((b, tot, c), F32),
        grid=(b, c // tc),
        in_specs=[pl.BlockSpec((1, tot, tc), lambda i, j: (i, 0, j)),
                  pl.BlockSpec((9, tc), lambda i, j: (0, j))],
        out_specs=pl.BlockSpec((1, tot, tc), lambda i, j: (i, 0, j)),
        scratch_shapes=[pltpu.VMEM((n_lat + 2 * GRID_W, tc), F32)],
        compiler_params=pltpu.CompilerParams(dimension_semantics=("arbitrary", "arbitrary"),
                                             vmem_limit_bytes=VMEM_LIMIT),
        name="conv",
    )(p, conv_w9)


def _bwd_chunk(n, n_ctx_chunks, n_chunks):
    return jnp.where(n < n_ctx_chunks, n_ctx_chunks - 1 - n, n_chunks + n_ctx_chunks - 1 - n)


def _tri_masks():
    r = lax.broadcasted_iota(jnp.int32, (CHUNK, CHUNK), 0)
    c = lax.broadcasted_iota(jnp.int32, (CHUNK, CHUNK), 1)
    return r >= c, r <= c, r == c


INV_BLOCK = 16


def _unit_triangular_inverse(n, same_blk, eye_f):
    hp = functools.partial(_dot, precision=HIGHEST)
    n_diag = jnp.where(same_blk, n, 0.0)
    d_inv = eye_f + n_diag
    pw = n_diag
    for _ in range(INV_BLOCK.bit_length() - 2):
        pw = hp(pw, pw)
        d_inv = hp(d_inv, eye_f + pw)
    p = hp(d_inv, n - n_diag)
    out = eye_f + p
    pw = p
    for _ in range((CHUNK // INV_BLOCK).bit_length() - 2):
        pw = hp(pw, pw)
        out = hp(out, eye_f + pw)
    return hp(out, d_inv)


def _gdn_dir(q_ref, k_ref, v_ref, gc_ref, gr_ref, o_ref, s_ref, d, incl, incl_t, eye, same_blk):
    scale = GDN_DIM ** -0.5
    gcol = gc_ref[0]
    grow = gr_ref[0, 0]
    cum_c = _dot(incl.astype(F32), gcol, precision=HIGHEST)
    cum_r = _dot(grow, incl_t.astype(F32), precision=HIGHEST)
    last = CHUNK - 1 if d == 0 else 0
    strict = jnp.logical_and(incl, jnp.logical_not(eye))
    eye_f = eye.astype(F32)
    for h in range(GDN_HEADS):
        bl = GDN_HEADS * d + h
        gl = 2 * GDN_HEADS + bl
        sl = slice(h * GDN_DIM, (h + 1) * GDN_DIM)
        gc = cum_c[:, gl:gl + 1]
        gr = cum_r[gl:gl + 1, :]
        beta_c = gcol[:, bl:bl + 1]
        beta_r = grow[bl:bl + 1, :]
        q = q_ref[0, :, sl]
        k = k_ref[0, :, sl]
        v = v_ref[0, :, sl]
        dec = jnp.exp(jnp.where(incl, gc - gr, -jnp.inf))
        qk = jnp.concatenate([q, k], axis=0).astype(BF16)
        gram = _dot_nt(qk, k.astype(BF16))
        nm = jnp.where(strict, -(gram[CHUNK:] * beta_c * dec), 0.0)
        t = _unit_triangular_inverse(nm, same_blk, eye_f)
        s_old = s_ref[bl]
        qks = _dot(qk, s_old.astype(BF16))
        eg = jnp.exp(gc)
        x = v - eg * qks[CHUNK:]
        v_new = _dot((t * beta_r).astype(BF16), x.astype(BF16))
        attn = gram[:CHUNK] * scale * dec
        v_new_b = v_new.astype(BF16)
        o_ref[0, :, sl] = (scale * eg) * qks[:CHUNK] + _dot(attn.astype(BF16), v_new_b)
        g_tot = gc[last:last + 1, :]
        kd = k * jnp.exp(g_tot - gc)
        s_ref[bl] = s_old * jnp.exp(g_tot) + _dot_tn(kd.astype(BF16), v_new_b)


def _gdn_kernel(qf, kf, vf, qb, kb, vb, gcf, gcb, grf, grb, of, ob, s_ref):
    @pl.when(pl.program_id(1) == 0)
    def _():
        s_ref[...] = jnp.zeros(s_ref.shape, F32)

    lower, upper, eye = _tri_masks()
    r = lax.broadcasted_iota(jnp.int32, (CHUNK, CHUNK), 0)
    c = lax.broadcasted_iota(jnp.int32, (CHUNK, CHUNK), 1)
    same_blk = (r // INV_BLOCK) == (c // INV_BLOCK)
    _gdn_dir(qf, kf, vf, gcf, grf, of, s_ref, 0, lower, upper, eye, same_blk)
    _gdn_dir(qb, kb, vb, gcb, grb, ob, s_ref, 1, upper, lower, eye, same_blk)


def _gdn_scan(qkv, gd, gd_rows, n_ctx_chunks):
    b, tot, _ = qkv.shape
    n_chunks = tot // CHUNK
    w = GDN_HEADS * GDN_DIM
    bwd = functools.partial(_bwd_chunk, n_ctx_chunks=n_ctx_chunks, n_chunks=n_chunks)

    def tok(col, rev):
        if rev:
            return pl.BlockSpec((1, CHUNK, w), lambda i, n: (i, bwd(n), col))
        return pl.BlockSpec((1, CHUNK, w), lambda i, n: (i, n, col))

    gcol_f = pl.BlockSpec((1, CHUNK, LANE), lambda i, n: (i, n, 0))
    gcol_b = pl.BlockSpec((1, CHUNK, LANE), lambda i, n: (i, bwd(n), 0))
    grow_f = pl.BlockSpec((1, 1, 4 * GDN_HEADS, CHUNK), lambda i, n: (i, n, 0, 0))
    grow_b = pl.BlockSpec((1, 1, 4 * GDN_HEADS, CHUNK), lambda i, n: (i, bwd(n), 0, 0))
    out = jax.ShapeDtypeStruct((b, tot, w), F32)
    return pl.pallas_call(
        _gdn_kernel,
        out_shape=(out, out),
        grid=(b, n_chunks),
        in_specs=[tok(0, False), tok(1, False), tok(2, False), tok(0, True), tok(1, True), tok(2, True),
                  gcol_f, gcol_b, grow_f, grow_b],
        out_specs=(tok(0, False), tok(0, True)),
        scratch_shapes=[pltpu.VMEM((2 * GDN_HEADS, GDN_DIM, GDN_DIM), F32)],
        compiler_params=pltpu.CompilerParams(dimension_semantics=("arbitrary", "arbitrary"),
                                             vmem_limit_bytes=VMEM_LIMIT),
        name="gdn",
    )(qkv, qkv, qkv, qkv, qkv, qkv, gd, gd, gd_rows, gd_rows)


def _gla_dir(qk_ref, v_ref, lr_ref, w_ref, b_ref, o_ref, s_ref, d, incl):
    scale = GLA_DK ** -0.5
    kw = GLA_HEADS * GLA_DK
    z = _dot(lr_ref[0], w_ref[...], precision=HIGHEST) + b_ref[...]
    log_a = -_softplus(-z) * (1.0 / GLA_NORMALIZER)
    cum = _dot(incl.astype(F32), log_a, precision=HIGHEST)
    centre = CHUNK // 2 if d == 0 else CHUNK - 1 - CHUNK // 2
    last = CHUNK - 1 if d == 0 else 0
    for h in range(GLA_HEADS):
        idx = GLA_HEADS * d + h
        q = qk_ref[0, :, h * GLA_DK:(h + 1) * GLA_DK] * scale
        k = qk_ref[0, :, kw + h * GLA_DK:kw + (h + 1) * GLA_DK]
        v = v_ref[0, :, h * GLA_DV:(h + 1) * GLA_DV].astype(BF16)
        gc = cum[:, h * GLA_DK:(h + 1) * GLA_DK]
        g_ref = gc[centre:centre + 1, :]
        g_tot = gc[last:last + 1, :]
        attn = _dot_nt((q * jnp.exp(gc - g_ref)).astype(BF16), (k * jnp.exp(g_ref - gc)).astype(BF16))
        attn = jnp.where(incl, attn, 0.0)
        s_old = s_ref[idx]
        o = _dot(attn.astype(BF16), v) + _dot_nt((q * jnp.exp(gc)).astype(BF16), s_old.astype(BF16))
        o_ref[0, :, h * GLA_DV:(h + 1) * GLA_DV] = o
        kd = k * jnp.exp(g_tot - gc)
        s_ref[idx] = s_old * jnp.exp(g_tot) + _dot_tn(v, kd.astype(BF16))


def _gla_kernel(qkf, vf, qkb, vb, lrf, lrb, wf, wb, bf, bb, of, ob, s_ref):
    @pl.when(pl.program_id(1) == 0)
    def _():
        s_ref[...] = jnp.zeros(s_ref.shape, F32)

    lower, upper, _ = _tri_masks()
    _gla_dir(qkf, vf, lrf, wf, bf, of, s_ref, 0, lower)
    _gla_dir(qkb, vb, lrb, wb, bb, ob, s_ref, 1, upper)


def _gla_scan(p, lr, wf, wb, bf, bb, n_ctx_chunks, qk_col, v_col):
    b, tot, _ = p.shape
    n_chunks = tot // CHUNK
    w = GLA_HEADS * GLA_DV
    bwd = functools.partial(_bwd_chunk, n_ctx_chunks=n_ctx_chunks, n_chunks=n_chunks)

    def tok(width, col, rev):
        if rev:
            return pl.BlockSpec((1, CHUNK, width), lambda i, n: (i, bwd(n), col))
        return pl.BlockSpec((1, CHUNK, width), lambda i, n: (i, n, col))

    const2 = lambda i, n: (0, 0)
    out = jax.ShapeDtypeStruct((b, tot, w), F32)
    return pl.pallas_call(
        _gla_kernel,
        out_shape=(out, out),
        grid=(b, n_chunks),
        in_specs=[tok(w, qk_col, False), tok(w, v_col, False), tok(w, qk_col, True), tok(w, v_col, True),
                  tok(LANE, 0, False), tok(LANE, 0, True),
                  pl.BlockSpec(wf.shape, const2), pl.BlockSpec(wb.shape, const2),
                  pl.BlockSpec(bf.shape, const2), pl.BlockSpec(bb.shape, const2)],
        out_specs=(tok(w, 0, False), tok(w, 0, True)),
        scratch_shapes=[pltpu.VMEM((2 * GLA_HEADS, GLA_DV, GLA_DK), F32)],
        compiler_params=pltpu.CompilerParams(dimension_semantics=("arbitrary", "arbitrary"),
                                             vmem_limit_bytes=VMEM_LIMIT),
        name="gla",
    )(p, p, p, p, lr, lr, wf, wb, bf, bb)


def _head_rms(o, g, width):
    outs = []
    for h in range(o.shape[-1] // width):
        oh = o[:, h * width:(h + 1) * width]
        ms = jnp.mean(oh * oh, axis=-1, keepdims=True)
        outs.append(oh * lax.rsqrt(ms + EPS))
    return jnp.concatenate(outs, axis=-1) * g


def _merge_kernel(oaf, oab, obf, obb, z_ref, r_ref, ga_ref, gb_ref, x_ref, g1_ref, sh2_ref, sc2_ref,
                  na_ref, nb_ref, nf_ref, woa_ref, wob_ref, wo_ref, wr_ref, br_ref,
                  x1_ref, h2_ref, lg_ref):
    ya_in = _head_rms(oaf[0] + oab[0], na_ref[...], GDN_DIM) * _silu(z_ref[0])
    ya = _dot(ya_in.astype(BF16), woa_ref[...])
    yb_in = _head_rms(obf[0] + obb[0], nb_ref[...], GLA_DV) * _silu(r_ref[0])
    yb = _dot(yb_in.astype(BF16), wob_ref[...])
    y = jax.nn.sigmoid(ga_ref[0]) * ya + jax.nn.sigmoid(gb_ref[0]) * yb
    x1 = x_ref[0] + g1_ref[0] * _dot(y.astype(BF16), wo_ref[...])
    x1_ref[0] = x1
    ms = jnp.mean(x1 * x1, axis=-1, keepdims=True)
    h2 = (x1 * lax.rsqrt(ms + EPS) * nf_ref[...]) * (1.0 + sc2_ref[0]) + sh2_ref[0]
    _store_tile_rows(h2_ref, h2)
    lg_ref[0] = _dot(h2, wr_ref[...], precision=HIGHEST) + br_ref[...]


def _merge(oaf, oab, obf, obb, p, x, g1, sh2, sc2, na, nb, nf, woa, wob, wo, wr, br, cols):
    b, l, d = x.shape
    z_col, r_col, ga_col, gb_col = cols
    lat = lambda i, s: (i, s + 1, 0)
    row3 = lambda i, s: (i, 0, 0)
    const2 = lambda i, s: (0, 0)
    blk = lambda i, s: (i, s, 0)
    tb = (1, TOK_BLOCK, d)
    pcol = lambda c: pl.BlockSpec(tb, lambda i, s: (i, s + 1, c))
    vec = pl.BlockSpec((1, 1, d), row3)
    cvec = pl.BlockSpec((1, d), const2)
    wspec = pl.BlockSpec((d, d), const2)
    groups = d // LANE
    n_blk = l // TOK_BLOCK
    return pl.pallas_call(
        _merge_kernel,
        out_shape=(jax.ShapeDtypeStruct((b, l, d), F32),
                   jax.ShapeDtypeStruct((b * l * groups, LANE), F32),
                   jax.ShapeDtypeStruct((b, l, LANE), F32)),
        grid=(b, n_blk),
        in_specs=[pl.BlockSpec(tb, lat)] * 4 + [pcol(z_col), pcol(r_col), pcol(ga_col), pcol(gb_col),
                  pl.BlockSpec(tb, blk), vec, vec, vec, cvec, cvec, cvec, wspec, wspec, wspec,
                  pl.BlockSpec((d, LANE), const2), pl.BlockSpec((1, LANE), const2)],
        out_specs=(pl.BlockSpec(tb, blk),
                   pl.BlockSpec((TOK_BLOCK * groups, LANE), lambda i, s: (i * n_blk + s, 0)),
                   pl.BlockSpec((1, TOK_BLOCK, LANE), blk)),
        compiler_params=pltpu.CompilerParams(dimension_semantics=("arbitrary", "arbitrary"),
                                             vmem_limit_bytes=VMEM_LIMIT),
        name="merge",
    )(oaf, oab, obf, obb, p, p, p, p, x, g1, sh2, sc2, na, nb, nf, woa, wob, wo, wr, br)


def _row_copy(src_ref, dst_ref, sem, src_row, dst_row):
    src = pl.multiple_of(src_row * SUBLANE, SUBLANE)
    dst = pl.multiple_of(dst_row * SUBLANE, SUBLANE)
    return pltpu.make_async_copy(src_ref.at[pl.ds(src, SUBLANE)], dst_ref.at[pl.ds(dst, SUBLANE)], sem)


def _gather_kernel(idx_ref, src_ref, dst_ref, idx_smem, idx_sem, row_sem):
    step = pl.program_id(0)
    idx_copy = pltpu.make_async_copy(idx_ref.at[step], idx_smem, idx_sem)
    idx_copy.start()
    idx_copy.wait()
    base = step * GATHER_ROWS

    def start(r, carry):
        _row_copy(src_ref, dst_ref, row_sem, idx_smem[r], base + r).start()
        return carry

    def wait(r, carry):
        _row_copy(src_ref, dst_ref, row_sem, 0, base + r).wait()
        return carry

    lax.fori_loop(0, GATHER_ROWS, start, 0)
    lax.fori_loop(0, GATHER_ROWS, wait, 0)


def _gather_rows(src, idx):
    m = idx.shape[0]
    steps = m // GATHER_ROWS
    assert src.shape[1] == LANE and src.dtype == F32
    return pl.pallas_call(
        _gather_kernel,
        out_shape=jax.ShapeDtypeStruct((m * SUBLANE, LANE), F32),
        grid=(steps,),
        in_specs=[pl.BlockSpec(memory_space=pl.ANY), pl.BlockSpec(memory_space=pl.ANY)],
        out_specs=pl.BlockSpec(memory_space=pl.ANY),
        scratch_shapes=[pltpu.SMEM((GATHER_ROWS,), jnp.int32), pltpu.SemaphoreType.DMA, pltpu.SemaphoreType.DMA],
        compiler_params=pltpu.CompilerParams(dimension_semantics=("arbitrary",), has_side_effects=True),
        name="gather_rows",
    )(idx.reshape(steps, GATHER_ROWS), src)


def _expert_kernel(be_ref, x_ref, wgu_ref, bgu_ref, wd_ref, bd_ref, y_ref):
    del be_ref
    de = wd_ref.shape[1]
    groups = wgu_ref.shape[1] // LANE
    x = _load_tile_rows(x_ref, EXPERT_BLOCK, groups)
    gu = _dot(x.astype(BF16), wgu_ref[0]) + bgu_ref[0]
    glu = jnp.minimum(gu[:, :de], SWIGLU_LIMIT)
    lin = jnp.clip(gu[:, de:], -SWIGLU_LIMIT, SWIGLU_LIMIT)
    act = glu * jax.nn.sigmoid(SWIGLU_ALPHA * glu) * (lin + 1.0)
    _store_tile_rows(y_ref, _dot(act.astype(BF16), wd_ref[0]) + bd_ref[0])


def _experts(xs, block_expert, w_gu, b_gu, w_down, b_down):
    ne, d, de2 = w_gu.shape
    de = de2 // 2
    groups = d // LANE
    n_slots = xs.shape[0] // groups
    blk_rows = EXPERT_BLOCK * groups
    grid_spec = pltpu.PrefetchScalarGridSpec(
        num_scalar_prefetch=1,
        grid=(n_slots // EXPERT_BLOCK,),
        in_specs=[pl.BlockSpec((blk_rows, LANE), lambda i, be: (i, 0)),
                  pl.BlockSpec((1, d, de2), lambda i, be: (be[i], 0, 0)),
                  pl.BlockSpec((1, 1, de2), lambda i, be: (be[i], 0, 0)),
                  pl.BlockSpec((1, de, d), lambda i, be: (be[i], 0, 0)),
                  pl.BlockSpec((1, 1, d), lambda i, be: (be[i], 0, 0))],
        out_specs=pl.BlockSpec((blk_rows, LANE), lambda i, be: (i, 0)),
    )
    return pl.pallas_call(
        _expert_kernel,
        out_shape=jax.ShapeDtypeStruct((n_slots * groups, LANE), F32),
        grid_spec=grid_spec,
        compiler_params=pltpu.CompilerParams(dimension_semantics=("arbitrary",), vmem_limit_bytes=VMEM_LIMIT),
        name="experts",
    )(block_expert, xs, w_gu, b_gu.reshape(ne, 1, de2), w_down, b_down.reshape(ne, 1, d))


def _combine_kernel(x1_ref, y4_ref, p_ref, g2_ref, fg_ref, o_ref):
    groups = x1_ref.shape[-1] // LANE
    p = p_ref[0]
    y = None
    for k in range(TOP_K):
        yk = p[:, k:k + 1] * _load_tile_rows(y4_ref, TOK_BLOCK, groups, first=k * groups, stride=TOP_K * groups)
        y = yk if y is None else y + yk
    x2 = x1_ref[0] + g2_ref[0] * y
    ms = jnp.mean(x2 * x2, axis=-1, keepdims=True)
    o_ref[0] = x2 * lax.rsqrt(ms + EPS) * fg_ref[...]


def _combine(x1, y4, top_p, g2, final_g):
    b, l, d = x1.shape
    blk = lambda i, s: (i, s, 0)
    n_blk = l // TOK_BLOCK
    return pl.pallas_call(
        _combine_kernel,
        out_shape=jax.ShapeDtypeStruct((b, l, d), F32),
        grid=(b, n_blk),
        in_specs=[pl.BlockSpec((1, TOK_BLOCK, d), blk),
                  pl.BlockSpec((TOK_BLOCK * TOP_K * d // LANE, LANE), lambda i, s: (i * n_blk + s, 0)),
                  pl.BlockSpec((1, TOK_BLOCK, TOP_K), blk),
                  pl.BlockSpec((1, 1, d), lambda i, s: (i, 0, 0)),
                  pl.BlockSpec((1, d), lambda i, s: (0, 0))],
        out_specs=pl.BlockSpec((1, TOK_BLOCK, d), blk),
        compiler_params=pltpu.CompilerParams(dimension_semantics=("arbitrary", "arbitrary"),
                                             vmem_limit_bytes=VMEM_LIMIT),
        name="combine",
    )(x1, y4, top_p, g2, final_g)


def _routing(logits):
    t = logits.shape[0]
    n_assign = t * TOP_K
    n_slots = -(-(n_assign + N_EXPERTS * (EXPERT_BLOCK - 1)) // GATHER_ROWS) * GATHER_ROWS
    top_logit, top_idx = lax.top_k(logits, TOP_K)
    top_p = jax.nn.softmax(top_logit, axis=-1)
    flat_e = top_idx.reshape(-1).astype(jnp.int32)
    order = jnp.argsort(flat_e).astype(jnp.int32)
    sorted_e = flat_e[order]
    counts = jnp.bincount(flat_e, length=N_EXPERTS).astype(jnp.int32)
    start = jnp.cumsum(counts) - counts
    padded = (counts + EXPERT_BLOCK - 1) // EXPERT_BLOCK * EXPERT_BLOCK
    pad_end = jnp.cumsum(padded)
    pad_start = pad_end - padded
    dest_sorted = pad_start[sorted_e] + jnp.arange(n_assign, dtype=jnp.int32) - start[sorted_e]
    slot_tok = jnp.zeros((n_slots,), jnp.int32).at[dest_sorted].set(order // TOP_K)
    dest = jnp.zeros((n_assign,), jnp.int32).at[order].set(dest_sorted)
    block_start = jnp.arange(n_slots // EXPERT_BLOCK, dtype=jnp.int32) * EXPERT_BLOCK
    block_expert = jnp.minimum(jnp.searchsorted(pad_end, block_start, side='right'), N_EXPERTS - 1)
    return slot_tok, dest, block_expert.astype(jnp.int32), top_p


def kernel(x, c, ctx, c_ctx, w_mod, b_mod, norm_mix_g, norm_ffn_g, w_in, conv_w, a_log_f, a_log_b, dt_bias_f,
           dt_bias_b, gdn_norm_g, gla_gate_w_f, gla_gate_w_b, gla_gate_b_f, gla_gate_b_b, gla_norm_g, w_out_a,
           w_out_b, w_out, w_router, b_router, w_gu, b_gu, w_down, b_down, final_norm_g):
    assert w_mod.shape[0] == 1, "single-layer kernel"
    b, l, d = x.shape
    n_ctx = ctx.shape[1]
    assert n_ctx == TOK_BLOCK and l % CONV_ROWS == 0 and d == GDN_HEADS * GDN_DIM
    gw = GDN_HEADS * GDN_DIM
    kw = GLA_HEADS * GLA_DK
    vw = GLA_HEADS * GLA_DV

    pad_rows = -(b + 1) % 8
    cc = jnp.concatenate([c, c_ctx[None], jnp.zeros((pad_rows, d), F32)], axis=0)
    mod = _modulation(cc, w_mod[0], b_mod[0])
    sh1, sc1, g1, sh2, sc2, g2 = [mod[:b, i * d:(i + 1) * d].reshape(b, 1, d) for i in range(6)]
    sh1c, sc1c = mod[b:b + 1, 0:d], mod[b:b + 1, d:2 * d]

    offs = {}
    o = 0
    for name, width in (("qa", gw), ("ka", gw), ("va", gw), ("za", gw), ("beta", 2 * GDN_HEADS),
                        ("dec", 2 * GDN_HEADS), ("qb", kw), ("kb", kw), ("vb", vw), ("rb", vw),
                        ("lr", 2 * GLA_RANK), ("gates", 2 * d)):
        offs[name] = (o, o + width)
        o += width
    wi = w_in[0]
    cols = lambda n: wi[:, offs[n][0]:offs[n][1]]
    w_main = jnp.concatenate([cols(n) for n in ("qa", "ka", "va", "za", "qb", "kb", "vb", "rb", "gates")],
                             axis=1).astype(BF16)
    zpad = lambda n: jnp.zeros((d, n), F32)
    w_small = jnp.concatenate([cols("beta"), cols("dec"), zpad(LANE - 4 * GDN_HEADS),
                               cols("lr"), zpad(LANE - 2 * GLA_RANK)], axis=1)
    lane_pad = lambda v, lo: jnp.zeros((1, LANE), F32).at[0, lo:lo + v.shape[0]].set(v)
    coef_a = lane_pad(-jnp.exp(jnp.concatenate([a_log_f[0], a_log_b[0]])), 2 * GDN_HEADS)
    coef_b = lane_pad(jnp.concatenate([dt_bias_f[0], dt_bias_b[0]]), 2 * GDN_HEADS)

    p, gd, lr = _input_projection(x, ctx, sh1, sc1, sh1c, sc1c, norm_mix_g, w_main, w_small, coef_a, coef_b)
    tot = p.shape[1]
    n_chunks = tot // CHUNK
    n_ctx_chunks = n_ctx // CHUNK

    qkv = _conv_qkv(p, conv_w[0].reshape(9, 3 * gw), n_ctx, l)
    gd_rows = gd[:, :, :4 * GDN_HEADS].reshape(b, n_chunks, CHUNK, 4 * GDN_HEADS).transpose(0, 1, 3, 2)
    oaf, oab = _gdn_scan(qkv, gd, gd_rows, n_ctx_chunks)

    embed = lambda w, lo: jnp.zeros((LANE, kw), F32).at[lo:lo + GLA_RANK].set(w)
    obf, obb = _gla_scan(p, lr, embed(gla_gate_w_f[0], 0), embed(gla_gate_w_b[0], GLA_RANK),
                         gla_gate_b_f, gla_gate_b_b, n_ctx_chunks, qk_col=4, v_col=5)

    wr = jnp.zeros((d, LANE), F32).at[:, :N_EXPERTS].set(w_router[0])
    br = jnp.zeros((1, LANE), F32).at[0, :N_EXPERTS].set(b_router[0])
    x1, h2, logits = _merge(oaf, oab, obf, obb, p, x, g1, sh2, sc2,
                            jnp.tile(gdn_norm_g, (1, GDN_HEADS)), jnp.tile(gla_norm_g, (1, GLA_HEADS)), norm_ffn_g,
                            w_out_a[0].astype(BF16), w_out_b[0].astype(BF16), w_out[0].astype(BF16), wr, br,
                            cols=(3, 6, 7, 8))

    return _channel_mixing(x1, h2, logits, g2, w_gu, b_gu, w_down, b_down, final_norm_g)


def _channel_mixing(x1, h2, logits, g2, w_gu, b_gu, w_down, b_down, final_norm_g):
    b, l, d = x1.shape
    t = b * l
    slot_tok, dest, block_expert, top_p = _routing(logits.reshape(t, LANE)[:, :N_EXPERTS])
    xs = _gather_rows(h2, slot_tok)
    yb = _experts(xs, block_expert, w_gu[0].astype(BF16), b_gu[0], w_down[0].astype(BF16), b_down[0])
    y4 = _gather_rows(yb, dest)
    return _combine(x1, y4, top_p.reshape(b, l, TOP_K), g2, final_norm_g.reshape(1, d))
```

```python
import functools

import jax
import jax.numpy as jnp
from jax import lax
from jax.experimental import pallas as pl
from jax.experimental.pallas import tpu as pltpu

F32 = jnp.float32
BF16 = jnp.bfloat16
HIGHEST = lax.Precision.HIGHEST

EPS = 1e-6
GRID_W = 64
CHUNK = 64
TOK_BLOCK = 256
CONV_ROWS = 512
GDN_HEADS = 8
GDN_DIM = 128
GLA_HEADS = 4
GLA_DK = 128
GLA_DV = 256
GLA_RANK = 16
GLA_NORMALIZER = 16.0
N_EXPERTS = 32
TOP_K = 4
SWIGLU_LIMIT = 7.0
SWIGLU_ALPHA = 1.702
EXPERT_BLOCK = 512
LANE = 128
SUBLANE = 8
VMEM_LIMIT = 56 * 1024 * 1024


def _silu(t):
    return t * jax.nn.sigmoid(t)


def _softplus(t):
    return jnp.maximum(t, 0.0) + jnp.log1p(jnp.exp(-jnp.abs(t)))


def _dot(a, b, **kw):
    return jnp.dot(a, b, preferred_element_type=F32, **kw)


def _dot_nt(a, b, **kw):
    return lax.dot_general(a, b, (((1,), (1,)), ((), ())), preferred_element_type=F32, **kw)


def _dot_tn(a, b, **kw):
    return lax.dot_general(a, b, (((0,), (0,)), ((), ())), preferred_element_type=F32, **kw)


def _store_tile_rows(ref, val):
    rows, width = val.shape
    groups = width // LANE
    for s in range(groups):
        ref[pl.ds(s, rows, stride=groups), :] = val[:, s * LANE:(s + 1) * LANE]


def _load_tile_rows(ref, rows, groups, first=0, stride=None):
    stride = groups if stride is None else stride
    return jnp.concatenate([ref[pl.ds(first + s, rows, stride=stride), :] for s in range(groups)], axis=-1)


def _mod_kernel(c_ref, w_ref, b_ref, o_ref):
    o_ref[...] = _dot(_silu(c_ref[...]), w_ref[...], precision=HIGHEST) + b_ref[...]


def _modulation(cc, w_mod, b_mod):
    rows, d = cc.shape
    n = w_mod.shape[1]
    tn = n // 4
    return pl.pallas_call(
        _mod_kernel,
        out_shape=jax.ShapeDtypeStruct((rows, n), F32),
        grid=(n // tn,),
        in_specs=[pl.BlockSpec((rows, d), lambda j: (0, 0)),
                  pl.BlockSpec((d, tn), lambda j: (0, j)),
                  pl.BlockSpec((1, tn), lambda j: (0, j))],
        out_specs=pl.BlockSpec((rows, tn), lambda j: (0, j)),
        compiler_params=pltpu.CompilerParams(vmem_limit_bytes=VMEM_LIMIT),
        name="mod",
    )(cc, w_mod, b_mod.reshape(1, n))


def _inproj_kernel(x_ref, ctx_ref, shl_ref, scl_ref, shc_ref, scc_ref, g_ref, w_ref, ws_ref, ca_ref, cb_ref,
                   p_ref, gd_ref, lr_ref):
    is_ctx = pl.program_id(1) == 0
    xin = jnp.where(is_ctx, ctx_ref[0], x_ref[0])
    shift = jnp.where(is_ctx, shc_ref[...], shl_ref[0])
    scale = jnp.where(is_ctx, scc_ref[...], scl_ref[0])
    ms = jnp.mean(xin * xin, axis=-1, keepdims=True)
    h = (xin * lax.rsqrt(ms + EPS) * g_ref[...]) * (1.0 + scale) + shift
    hb = h.astype(BF16)
    n_main = w_ref.shape[1]
    for j in range(n_main // 1024):
        p_ref[0, :, j * 1024:(j + 1) * 1024] = _dot(hb, w_ref[:, j * 1024:(j + 1) * 1024])
    sm = _dot(h, ws_ref[...], precision=HIGHEST)
    raw = sm[:, :LANE]
    lane = lax.broadcasted_iota(jnp.int32, raw.shape, 1)
    beta = jax.nn.sigmoid(raw)
    g = ca_ref[...] * _softplus(raw + cb_ref[...])
    gd_ref[0] = jnp.where(lane < 2 * GDN_HEADS, beta, jnp.where(lane < 4 * GDN_HEADS, g, 0.0))
    lr_ref[0] = sm[:, LANE:]


def _input_projection(x, ctx, sh_lat, sc_lat, sh_ctx, sc_ctx, norm_g, w_main, w_small, coef_a, coef_b):
    b, l, d = x.shape
    n_blk = l // TOK_BLOCK + 1
    n_main = w_main.shape[1]
    tot = n_blk * TOK_BLOCK
    row3 = lambda i, s: (i, 0, 0)
    const2 = lambda i, s: (0, 0)
    blk = lambda i, s: (i, s, 0)
    return pl.pallas_call(
        _inproj_kernel,
        out_shape=(jax.ShapeDtypeStruct((b, tot, n_main), F32),
                   jax.ShapeDtypeStruct((b, tot, LANE), F32),
                   jax.ShapeDtypeStruct((b, tot, LANE), F32)),
        grid=(b, n_blk),
        in_specs=[pl.BlockSpec((1, TOK_BLOCK, d), lambda i, s: (i, jnp.maximum(s - 1, 0), 0)),
                  pl.BlockSpec((1, TOK_BLOCK, d), row3),
                  pl.BlockSpec((1, 1, d), row3), pl.BlockSpec((1, 1, d), row3),
                  pl.BlockSpec((1, d), const2), pl.BlockSpec((1, d), const2),
                  pl.BlockSpec((1, d), const2),
                  pl.BlockSpec((d, n_main), const2, pipeline_mode=pl.Buffered(1)),
                  pl.BlockSpec((d, 2 * LANE), const2),
                  pl.BlockSpec((1, LANE), const2), pl.BlockSpec((1, LANE), const2)],
        out_specs=(pl.BlockSpec((1, TOK_BLOCK, n_main), blk),
                   pl.BlockSpec((1, TOK_BLOCK, LANE), blk),
                   pl.BlockSpec((1, TOK_BLOCK, LANE), blk)),
        compiler_params=pltpu.CompilerParams(dimension_semantics=("arbitrary", "arbitrary"),
                                             vmem_limit_bytes=VMEM_LIMIT),
        name="inproj",
    )(x, ctx, sh_lat, sc_lat, sh_ctx, sc_ctx, norm_g, w_main, w_small, coef_a, coef_b)


def _conv_kernel(p_ref, w_ref, o_ref, pad_ref, *, n_ctx, n_lat, tc, qk_cols):
    is_qk = pl.program_id(1) * tc < qk_cols
    w = w_ref[...]

    def post(y):
        s = _silu(y)
        outs = []
        for hh in range(tc // GDN_DIM):
            sh = s[:, hh * GDN_DIM:(hh + 1) * GDN_DIM]
            fac = lax.rsqrt(jnp.sum(sh * sh, axis=-1, keepdims=True) + EPS)
            outs.append(sh * jnp.where(is_qk, fac, 1.0))
        return outs[0] if len(outs) == 1 else jnp.concatenate(outs, axis=-1)

    xc = p_ref[0, 0:n_ctx, :]
    ridx = lax.broadcasted_iota(jnp.int32, (n_ctx, 1), 0)
    yc = (jnp.where(ridx == 0, 0.0, pltpu.roll(xc * w[3:4], 1, 0)) + xc * w[4:5]
          + jnp.where(ridx == n_ctx - 1, 0.0, pltpu.roll(xc * w[5:6], n_ctx - 1, 0)))
    o_ref[0, 0:n_ctx, :] = post(yc)

    zeros = jnp.zeros((GRID_W, tc), F32)
    pad_ref[0:GRID_W, :] = zeros
    pad_ref[GRID_W + n_lat:GRID_W + n_lat + GRID_W, :] = zeros
    pad_ref[GRID_W:GRID_W + n_lat, :] = p_ref[0, n_ctx:n_ctx + n_lat, :]
    rows = min(CONV_ROWS, n_lat)
    col = lax.broadcasted_iota(jnp.int32, (rows, 1), 0) % GRID_W

    def body(i, carry):
        s = pl.multiple_of(i * rows, rows)
        up = pad_ref[pl.ds(s, rows), :]
        mid = pad_ref[pl.ds(s + GRID_W, rows), :]
        dn = pad_ref[pl.ds(s + 2 * GRID_W, rows), :]
        a0 = up * w[0:1] + mid * w[3:4] + dn * w[6:7]
        a1 = up * w[1:2] + mid * w[4:5] + dn * w[7:8]
        a2 = up * w[2:3] + mid * w[5:6] + dn * w[8:9]
        y = (jnp.where(col == 0, 0.0, pltpu.roll(a0, 1, 0)) + a1
             + jnp.where(col == GRID_W - 1, 0.0, pltpu.roll(a2, rows - 1, 0)))
        o_ref[0, pl.ds(n_ctx + s, rows), :] = post(y)
        return carry

    lax.fori_loop(0, n_lat // rows, body, 0)


def _conv_qkv(p, conv_w9, n_ctx, n_lat):
    b, tot, _ = p.shape
    c = conv_w9.shape[1]
    tc = 2 * GDN_DIM
    kern = functools.partial(_conv_kernel, n_ctx=n_ctx, n_lat=n_lat, tc=tc, qk_cols=2 * GDN_HEADS * GDN_DIM)
    return pl.pallas_call(
        kern,
        out_shape=jax.ShapeDtypeStruct((b, tot, c), F32),
        grid=(b, c // tc),
        in_specs=[pl.BlockSpec((1, tot, tc), lambda i, j: (i, 0, j)),
                  pl.BlockSpec((9, tc), lambda i, j: (0, j))],
        out_specs=pl.BlockSpec((1, tot, tc), lambda i, j: (i, 0, j)),
        scratch_shapes=[pltpu.VMEM((n_lat + 2 * GRID_W, tc), F32)],
        compiler_params=pltpu.CompilerParams(dimension_semantics=("arbitrary", "arbitrary"),
                                             vmem_limit_bytes=VMEM_LIMIT),
        name="conv",
    )(p, conv_w9)


def _bwd_chunk(n, n_ctx_chunks, n_chunks):
    return jnp.where(n < n_ctx_chunks, n_ctx_chunks - 1 - n, n_chunks + n_ctx_chunks - 1 - n)


def _tri_masks():
    r = lax.broadcasted_iota(jnp.int32, (CHUNK, CHUNK), 0)
    c = lax.broadcasted_iota(jnp.int32, (CHUNK, CHUNK), 1)
    return r >= c, r <= c, r == c


INV_BLOCK = 16


def _unit_triangular_inverse(n, same_blk, eye_f):
    hp = functools.partial(_dot, precision=HIGHEST)
    n_diag = jnp.where(same_blk, n, 0.0)
    d_inv = eye_f + n_diag
    pw = n_diag
    for _ in range(INV_BLOCK.bit_length() - 2):
        pw = hp(pw, pw)
        d_inv = hp(d_inv, eye_f + pw)
    p = hp(d_inv, n - n_diag)
    out = eye_f + p
    pw = p
    for _ in range((CHUNK // INV_BLOCK).bit_length() - 2):
        pw = hp(pw, pw)
        out = hp(out, eye_f + pw)
    return hp(out, d_inv)


def _gdn_dir(q_ref, k_ref, v_ref, gc_ref, gr_ref, o_ref, s_ref, d, incl, incl_t, eye, same_blk):
    scale = GDN_DIM ** -0.5
    gcol = gc_ref[0]
    grow = gr_ref[0, 0]
    cum_c = _dot(incl.astype(F32), gcol, precision=HIGHEST)
    cum_r = _dot(grow, incl_t.astype(F32), precision=HIGHEST)
    last = CHUNK - 1 if d == 0 else 0
    strict = jnp.logical_and(incl, jnp.logical_not(eye))
    eye_f = eye.astype(F32)
    for h in range(GDN_HEADS):
        bl = GDN_HEADS * d + h
        gl = 2 * GDN_HEADS + bl
        sl = slice(h * GDN_DIM, (h + 1) * GDN_DIM)
        gc = cum_c[:, gl:gl + 1]
        gr = cum_r[gl:gl + 1, :]
        beta_c = gcol[:, bl:bl + 1]
        beta_r = grow[bl:bl + 1, :]
        q = q_ref[0, :, sl]
        k = k_ref[0, :, sl]
        v = v_ref[0, :, sl]
        dec = jnp.exp(jnp.where(incl, gc - gr, -jnp.inf))
        qk = jnp.concatenate([q, k], axis=0).astype(BF16)
        gram = _dot_nt(qk, k.astype(BF16))
        nm = jnp.where(strict, -(gram[CHUNK:] * beta_c * dec), 0.0)
        t = _unit_triangular_inverse(nm, same_blk, eye_f)
        s_old = s_ref[bl]
        qks = _dot(qk, s_old.astype(BF16))
        eg = jnp.exp(gc)
        x = v - eg * qks[CHUNK:]
        v_new = _dot((t * beta_r).astype(BF16), x.astype(BF16))
        attn = gram[:CHUNK] * scale * dec
        v_new_b = v_new.astype(BF16)
        o_ref[0, :, sl] = (scale * eg) * qks[:CHUNK] + _dot(attn.astype(BF16), v_new_b)
        g_tot = gc[last:last + 1, :]
        kd = k * jnp.exp(g_tot - gc)
        s_ref[bl] = s_old * jnp.exp(g_tot) + _dot_tn(kd.astype(BF16), v_new_b)


def _gdn_kernel(qf, kf, vf, qb, kb, vb, gcf, gcb, grf, grb, of, ob, s_ref):
    @pl.when(pl.program_id(1) == 0)
    def _():
        s_ref[...] = jnp.zeros(s_ref.shape, F32)

    lower, upper, eye = _tri_masks()
    r = lax.broadcasted_iota(jnp.int32, (CHUNK, CHUNK), 0)
    c = lax.broadcasted_iota(jnp.int32, (CHUNK, CHUNK), 1)
    same_blk = (r // INV_BLOCK) == (c // INV_BLOCK)
    _gdn_dir(qf, kf, vf, gcf, grf, of, s_ref, 0, lower, upper, eye, same_blk)
    _gdn_dir(qb, kb, vb, gcb, grb, ob, s_ref, 1, upper, lower, eye, same_blk)


def _gdn_scan(qkv, gd, gd_rows, n_ctx_chunks):
    b, tot, _ = qkv.shape
    n_chunks = tot // CHUNK
    w = GDN_HEADS * GDN_DIM
    bwd = functools.partial(_bwd_chunk, n_ctx_chunks=n_ctx_chunks, n_chunks=n_chunks)

    def tok(col, rev):
        if rev:
            return pl.BlockSpec((1, CHUNK, w), lambda i, n: (i, bwd(n), col))
        return pl.BlockSpec((1, CHUNK, w), lambda i, n: (i, n, col))

    gcol_f = pl.BlockSpec((1, CHUNK, LANE), lambda i, n: (i, n, 0))
    gcol_b = pl.BlockSpec((1, CHUNK, LANE), lambda i, n: (i, bwd(n), 0))
    grow_f = pl.BlockSpec((1, 1, 4 * GDN_HEADS, CHUNK), lambda i, n: (i, n, 0, 0))
    grow_b = pl.BlockSpec((1, 1, 4 * GDN_HEADS, CHUNK), lambda i, n: (i, bwd(n), 0, 0))
    out = jax.ShapeDtypeStruct((b, tot, w), F32)
    return pl.pallas_call(
        _gdn_kernel,
        out_shape=(out, out),
        grid=(b, n_chunks),
        in_specs=[tok(0, False), tok(1, False), tok(2, False), tok(0, True), tok(1, True), tok(2, True),
                  gcol_f, gcol_b, grow_f, grow_b],
        out_specs=(tok(0, False), tok(0, True)),
        scratch_shapes=[pltpu.VMEM((2 * GDN_HEADS, GDN_DIM, GDN_DIM), F32)],
        compiler_params=pltpu.CompilerParams(dimension_semantics=("arbitrary", "arbitrary"),
                                             vmem_limit_bytes=VMEM_LIMIT),
        name="gdn",
    )(qkv, qkv, qkv, qkv, qkv, qkv, gd, gd, gd_rows, gd_rows)


def _gla_dir(qk_ref, v_ref, lr_ref, w_ref, b_ref, o_ref, s_ref, d, incl):
    scale = GLA_DK ** -0.5
    kw = GLA_HEADS * GLA_DK
    z = _dot(lr_ref[0], w_ref[...], precision=HIGHEST) + b_ref[...]
    log_a = -_softplus(-z) * (1.0 / GLA_NORMALIZER)
    cum = _dot(incl.astype(F32), log_a, precision=HIGHEST)
    centre = CHUNK // 2 if d == 0 else CHUNK - 1 - CHUNK // 2
    last = CHUNK - 1 if d == 0 else 0
    for h in range(GLA_HEADS):
        idx = GLA_HEADS * d + h
        q = qk_ref[0, :, h * GLA_DK:(h + 1) * GLA_DK] * scale
        k = qk_ref[0, :, kw + h * GLA_DK:kw + (h + 1) * GLA_DK]
        v = v_ref[0, :, h * GLA_DV:(h + 1) * GLA_DV].astype(BF16)
        gc = cum[:, h * GLA_DK:(h + 1) * GLA_DK]
        g_ref = gc[centre:centre + 1, :]
        g_tot = gc[last:last + 1, :]
        attn = _dot_nt((q * jnp.exp(gc - g_ref)).astype(BF16), (k * jnp.exp(g_ref - gc)).astype(BF16))
        attn = jnp.where(incl, attn, 0.0)
        s_old = s_ref[idx]
        o = _dot(attn.astype(BF16), v) + _dot_nt((q * jnp.exp(gc)).astype(BF16), s_old.astype(BF16))
        o_ref[0, :, h * GLA_DV:(h + 1) * GLA_DV] = o
        kd = k * jnp.exp(g_tot - gc)
        s_ref[idx] = s_old * jnp.exp(g_tot) + _dot_tn(v, kd.astype(BF16))


def _gla_kernel(qkf, vf, qkb, vb, lrf, lrb, wf, wb, bf, bb, of, ob, s_ref):
    @pl.when(pl.program_id(1) == 0)
    def _():
        s_ref[...] = jnp.zeros(s_ref.shape, F32)

    lower, upper, _ = _tri_masks()
    _gla_dir(qkf, vf, lrf, wf, bf, of, s_ref, 0, lower)
    _gla_dir(qkb, vb, lrb, wb, bb, ob, s_ref, 1, upper)


def _gla_scan(p, lr, wf, wb, bf, bb, n_ctx_chunks, qk_col, v_col):
    b, tot, _ = p.shape
    n_chunks = tot // CHUNK
    w = GLA_HEADS * GLA_DV
    bwd = functools.partial(_bwd_chunk, n_ctx_chunks=n_ctx_chunks, n_chunks=n_chunks)

    def tok(width, col, rev):
        if rev:
            return pl.BlockSpec((1, CHUNK, width), lambda i, n: (i, bwd(n), col))
        return pl.BlockSpec((1, CHUNK, width), lambda i, n: (i, n, col))

    const2 = lambda i, n: (0, 0)
    out = jax.ShapeDtypeStruct((b, tot, w), F32)
    return pl.pallas_call(
        _gla_kernel,
        out_shape=(out, out),
        grid=(b, n_chunks),
        in_specs=[tok(w, qk_col, False), tok(w, v_col, False), tok(w, qk_col, True), tok(w, v_col, True),
                  tok(LANE, 0, False), tok(LANE, 0, True),
                  pl.BlockSpec(wf.shape, const2), pl.BlockSpec(wb.shape, const2),
                  pl.BlockSpec(bf.shape, const2), pl.BlockSpec(bb.shape, const2)],
        out_specs=(tok(w, 0, False), tok(w, 0, True)),
        scratch_shapes=[pltpu.VMEM((2 * GLA_HEADS, GLA_DV, GLA_DK), F32)],
        compiler_params=pltpu.CompilerParams(dimension_semantics=("arbitrary", "arbitrary"),
                                             vmem_limit_bytes=VMEM_LIMIT),
        name="gla",
    )(p, p, p, p, lr, lr, wf, wb, bf, bb)


def _head_rms(o, g, width):
    outs = []
    for h in range(o.shape[-1] // width):
        oh = o[:, h * width:(h + 1) * width]
        ms = jnp.mean(oh * oh, axis=-1, keepdims=True)
        outs.append(oh * lax.rsqrt(ms + EPS))
    return jnp.concatenate(outs, axis=-1) * g


def _merge_kernel(oaf, oab, obf, obb, z_ref, r_ref, ga_ref, gb_ref, x_ref, g1_ref, sh2_ref, sc2_ref,
                  na_ref, nb_ref, nf_ref, woa_ref, wob_ref, wo_ref, wr_ref, br_ref,
                  x1_ref, h2_ref, lg_ref):
    ya_in = _head_rms(oaf[0] + oab[0], na_ref[...], GDN_DIM) * _silu(z_ref[0])
    ya = _dot(ya_in.astype(BF16), woa_ref[...])
    yb_in = _head_rms(obf[0] + obb[0], nb_ref[...], GLA_DV) * _silu(r_ref[0])
    yb = _dot(yb_in.astype(BF16), wob_ref[...])
    y = jax.nn.sigmoid(ga_ref[0]) * ya + jax.nn.sigmoid(gb_ref[0]) * yb
    x1 = x_ref[0] + g1_ref[0] * _dot(y.astype(BF16), wo_ref[...])
    x1_ref[0] = x1
    ms = jnp.mean(x1 * x1, axis=-1, keepdims=True)
    h2 = (x1 * lax.rsqrt(ms + EPS) * nf_ref[...]) * (1.0 + sc2_ref[0]) + sh2_ref[0]
    _store_tile_rows(h2_ref, h2)
    lg_ref[0] = _dot(h2, wr_ref[...], precision=HIGHEST) + br_ref[...]


def _merge(oaf, oab, obf, obb, p, x, g1, sh2, sc2, na, nb, nf, woa, wob, wo, wr, br, cols):
    b, l, d = x.shape
    z_col, r_col, ga_col, gb_col = cols
    lat = lambda i, s: (i, s + 1, 0)
    row3 = lambda i, s: (i, 0, 0)
    const2 = lambda i, s: (0, 0)
    blk = lambda i, s: (i, s, 0)
    tb = (1, TOK_BLOCK, d)
    pcol = lambda c: pl.BlockSpec(tb, lambda i, s: (i, s + 1, c))
    vec = pl.BlockSpec((1, 1, d), row3)
    cvec = pl.BlockSpec((1, d), const2)
    wspec = pl.BlockSpec((d, d), const2)
    groups = d // LANE
    n_blk = l // TOK_BLOCK
    return pl.pallas_call(
        _merge_kernel,
        out_shape=(jax.ShapeDtypeStruct((b, l, d), F32),
                   jax.ShapeDtypeStruct((b * l * groups, LANE), F32),
                   jax.ShapeDtypeStruct((b, l, LANE), F32)),
        grid=(b, n_blk),
        in_specs=[pl.BlockSpec(tb, lat)] * 4 + [pcol(z_col), pcol(r_col), pcol(ga_col), pcol(gb_col),
                  pl.BlockSpec(tb, blk), vec, vec, vec, cvec, cvec, cvec, wspec, wspec, wspec,
                  pl.BlockSpec((d, LANE), const2), pl.BlockSpec((1, LANE), const2)],
        out_specs=(pl.BlockSpec(tb, blk),
                   pl.BlockSpec((TOK_BLOCK * groups, LANE), lambda i, s: (i * n_blk + s, 0)),
                   pl.BlockSpec((1, TOK_BLOCK, LANE), blk)),
        compiler_params=pltpu.CompilerParams(dimension_semantics=("arbitrary", "arbitrary"),
                                             vmem_limit_bytes=VMEM_LIMIT),
        name="merge",
    )(oaf, oab, obf, obb, p, p, p, p, x, g1, sh2, sc2, na, nb, nf, woa, wob, wo, wr, br)


def _row_copy(src_ref, dst_ref, sem, src_row, dst_row):
    src = pl.multiple_of(src_row * SUBLANE, SUBLANE)
    dst = pl.multiple_of(dst_row * SUBLANE, SUBLANE)
    return pltpu.make_async_copy(src_ref.at[pl.ds(src, SUBLANE)], dst_ref.at[pl.ds(dst, SUBLANE)], sem)


def _gather_step(step, n_steps, rows, idx_hbm, src_hbm, idx_smem, buf, idx_sem, row_sem):
    def idx_copy(s):
        return pltpu.make_async_copy(idx_hbm.at[s], idx_smem.at[s % 2], idx_sem.at[s % 2])

    def issue(s):
        slot = s % 2

        def body(r, carry):
            _row_copy(src_hbm, buf.at[slot], row_sem.at[slot], idx_smem[slot, r], r).start()
            return carry

        lax.fori_loop(0, rows, body, 0)

    @pl.when(step == 0)
    def _():
        idx_copy(0).start()
        idx_copy(0).wait()
        issue(0)

        @pl.when(n_steps > 1)
        def _():
            idx_copy(1).start()

    @pl.when(step + 1 < n_steps)
    def _():
        idx_copy(step + 1).wait()
        issue(step + 1)

        @pl.when(step + 2 < n_steps)
        def _():
            idx_copy(step + 2).start()

    slot = step % 2
    pltpu.make_async_copy(src_hbm.at[pl.ds(0, rows * SUBLANE)], buf.at[slot], row_sem.at[slot]).wait()
    return buf.at[slot]


def _gather_scratch(rows):
    return [pltpu.SMEM((2, rows), jnp.int32), pltpu.VMEM((2, rows * SUBLANE, LANE), F32),
            pltpu.SemaphoreType.DMA((2,)), pltpu.SemaphoreType.DMA((2,))]


def _expert_kernel(be_ref, tok_ref, h_ref, wgu_ref, bgu_ref, wd_ref, bd_ref, y_ref, *gather_scratch):
    del be_ref
    de = wd_ref.shape[1]
    groups = wgu_ref.shape[1] // LANE
    x_ref = _gather_step(pl.program_id(0), pl.num_programs(0), EXPERT_BLOCK, tok_ref, h_ref, *gather_scratch)
    x = _load_tile_rows(x_ref, EXPERT_BLOCK, groups)
    gu = _dot(x.astype(BF16), wgu_ref[0]) + bgu_ref[0]
    glu = jnp.minimum(gu[:, :de], SWIGLU_LIMIT)
    lin = jnp.clip(gu[:, de:], -SWIGLU_LIMIT, SWIGLU_LIMIT)
    act = glu * jax.nn.sigmoid(SWIGLU_ALPHA * glu) * (lin + 1.0)
    _store_tile_rows(y_ref, _dot(act.astype(BF16), wd_ref[0]) + bd_ref[0])


def _experts(h, slot_tok, block_expert, w_gu, b_gu, w_down, b_down):
    ne, d, de2 = w_gu.shape
    de = de2 // 2
    groups = d // LANE
    assert groups == SUBLANE
    n_slots = slot_tok.shape[0]
    blk_rows = EXPERT_BLOCK * groups
    grid_spec = pltpu.PrefetchScalarGridSpec(
        num_scalar_prefetch=1,
        grid=(n_slots // EXPERT_BLOCK,),
        in_specs=[pl.BlockSpec(memory_space=pl.ANY), pl.BlockSpec(memory_space=pl.ANY),
                  pl.BlockSpec((1, d, de2), lambda i, be: (be[i], 0, 0)),
                  pl.BlockSpec((1, 1, de2), lambda i, be: (be[i], 0, 0)),
                  pl.BlockSpec((1, de, d), lambda i, be: (be[i], 0, 0)),
                  pl.BlockSpec((1, 1, d), lambda i, be: (be[i], 0, 0))],
        out_specs=pl.BlockSpec((blk_rows, LANE), lambda i, be: (i, 0)),
        scratch_shapes=_gather_scratch(EXPERT_BLOCK),
    )
    return pl.pallas_call(
        _expert_kernel,
        out_shape=jax.ShapeDtypeStruct((n_slots * groups, LANE), F32),
        grid_spec=grid_spec,
        compiler_params=pltpu.CompilerParams(dimension_semantics=("arbitrary",), vmem_limit_bytes=VMEM_LIMIT),
        name="experts",
    )(block_expert, slot_tok.reshape(n_slots // EXPERT_BLOCK, EXPERT_BLOCK), h,
      w_gu, b_gu.reshape(ne, 1, de2), w_down, b_down.reshape(ne, 1, d))


def _combine_kernel(dest_ref, yb_ref, x1_ref, p_ref, g2_ref, fg_ref, o_ref, *gather_scratch):
    groups = x1_ref.shape[-1] // LANE
    step = pl.program_id(0) * pl.num_programs(1) + pl.program_id(1)
    n_steps = pl.num_programs(0) * pl.num_programs(1)
    y4_ref = _gather_step(step, n_steps, TOK_BLOCK * TOP_K, dest_ref, yb_ref, *gather_scratch)
    p = p_ref[0]
    y = None
    for k in range(TOP_K):
        yk = p[:, k:k + 1] * _load_tile_rows(y4_ref, TOK_BLOCK, groups, first=k * groups, stride=TOP_K * groups)
        y = yk if y is None else y + yk
    x2 = x1_ref[0] + g2_ref[0] * y
    ms = jnp.mean(x2 * x2, axis=-1, keepdims=True)
    o_ref[0] = x2 * lax.rsqrt(ms + EPS) * fg_ref[...]


def _combine(x1, yb, dest, top_p, g2, final_g):
    b, l, d = x1.shape
    assert d // LANE == SUBLANE
    blk = lambda i, s: (i, s, 0)
    n_blk = l // TOK_BLOCK
    rows = TOK_BLOCK * TOP_K
    return pl.pallas_call(
        _combine_kernel,
        out_shape=jax.ShapeDtypeStruct((b, l, d), F32),
        grid=(b, n_blk),
        in_specs=[pl.BlockSpec(memory_space=pl.ANY), pl.BlockSpec(memory_space=pl.ANY),
                  pl.BlockSpec((1, TOK_BLOCK, d), blk),
                  pl.BlockSpec((1, TOK_BLOCK, TOP_K), blk),
                  pl.BlockSpec((1, 1, d), lambda i, s: (i, 0, 0)),
                  pl.BlockSpec((1, d), lambda i, s: (0, 0))],
        out_specs=pl.BlockSpec((1, TOK_BLOCK, d), blk),
        scratch_shapes=_gather_scratch(rows),
        compiler_params=pltpu.CompilerParams(dimension_semantics=("arbitrary", "arbitrary"),
                                             vmem_limit_bytes=VMEM_LIMIT),
        name="combine",
    )(dest.reshape(b * n_blk, rows), yb, x1, top_p, g2, final_g)


def _routing(logits):
    t = logits.shape[0]
    n_assign = t * TOP_K
    n_slots = -(-(n_assign + N_EXPERTS * (EXPERT_BLOCK - 1)) // EXPERT_BLOCK) * EXPERT_BLOCK
    top_logit, top_idx = lax.top_k(logits, TOP_K)
    top_p = jax.nn.softmax(top_logit, axis=-1)
    flat_e = top_idx.reshape(-1).astype(jnp.int32)
    order = jnp.argsort(flat_e).astype(jnp.int32)
    sorted_e = flat_e[order]
    counts = jnp.bincount(flat_e, length=N_EXPERTS).astype(jnp.int32)
    start = jnp.cumsum(counts) - counts
    padded = (counts + EXPERT_BLOCK - 1) // EXPERT_BLOCK * EXPERT_BLOCK
    pad_end = jnp.cumsum(padded)
    pad_start = pad_end - padded
    dest_sorted = pad_start[sorted_e] + jnp.arange(n_assign, dtype=jnp.int32) - start[sorted_e]
    slot_tok = jnp.zeros((n_slots,), jnp.int32).at[dest_sorted].set(order // TOP_K)
    dest = jnp.zeros((n_assign,), jnp.int32).at[order].set(dest_sorted)
    block_start = jnp.arange(n_slots // EXPERT_BLOCK, dtype=jnp.int32) * EXPERT_BLOCK
    block_expert = jnp.minimum(jnp.searchsorted(pad_end, block_start, side='right'), N_EXPERTS - 1)
    return slot_tok, dest, block_expert.astype(jnp.int32), top_p


def kernel(x, c, ctx, c_ctx, w_mod, b_mod, norm_mix_g, norm_ffn_g, w_in, conv_w, a_log_f, a_log_b, dt_bias_f,
           dt_bias_b, gdn_norm_g, gla_gate_w_f, gla_gate_w_b, gla_gate_b_f, gla_gate_b_b, gla_norm_g, w_out_a,
           w_out_b, w_out, w_router, b_router, w_gu, b_gu, w_down, b_down, final_norm_g):
    assert w_mod.shape[0] == 1, "single-layer kernel"
    b, l, d = x.shape
    n_ctx = ctx.shape[1]
    assert n_ctx == TOK_BLOCK and l % CONV_ROWS == 0 and d == GDN_HEADS * GDN_DIM
    gw = GDN_HEADS * GDN_DIM
    kw = GLA_HEADS * GLA_DK
    vw = GLA_HEADS * GLA_DV

    pad_rows = -(b + 1) % 8
    cc = jnp.concatenate([c, c_ctx[None], jnp.zeros((pad_rows, d), F32)], axis=0)
    mod = _modulation(cc, w_mod[0], b_mod[0])
    sh1, sc1, g1, sh2, sc2, g2 = [mod[:b, i * d:(i + 1) * d].reshape(b, 1, d) for i in range(6)]
    sh1c, sc1c = mod[b:b + 1, 0:d], mod[b:b + 1, d:2 * d]

    offs = {}
    o = 0
    for name, width in (("qa", gw), ("ka", gw), ("va", gw), ("za", gw), ("beta", 2 * GDN_HEADS),
                        ("dec", 2 * GDN_HEADS), ("qb", kw), ("kb", kw), ("vb", vw), ("rb", vw),
                        ("lr", 2 * GLA_RANK), ("gates", 2 * d)):
        offs[name] = (o, o + width)
        o += width
    wi = w_in[0]
    cols = lambda n: wi[:, offs[n][0]:offs[n][1]]
    w_main = jnp.concatenate([cols(n) for n in ("qa", "ka", "va", "za", "qb", "kb", "vb", "rb", "gates")],
                             axis=1).astype(BF16)
    zpad = lambda n: jnp.zeros((d, n), F32)
    w_small = jnp.concatenate([cols("beta"), cols("dec"), zpad(LANE - 4 * GDN_HEADS),
                               cols("lr"), zpad(LANE - 2 * GLA_RANK)], axis=1)
    lane_pad = lambda v, lo: jnp.zeros((1, LANE), F32).at[0, lo:lo + v.shape[0]].set(v)
    coef_a = lane_pad(-jnp.exp(jnp.concatenate([a_log_f[0], a_log_b[0]])), 2 * GDN_HEADS)
    coef_b = lane_pad(jnp.concatenate([dt_bias_f[0], dt_bias_b[0]]), 2 * GDN_HEADS)

    p, gd, lr = _input_projection(x, ctx, sh1, sc1, sh1c, sc1c, norm_mix_g, w_main, w_small, coef_a, coef_b)
    tot = p.shape[1]
    n_chunks = tot // CHUNK
    n_ctx_chunks = n_ctx // CHUNK

    qkv = _conv_qkv(p, conv_w[0].reshape(9, 3 * gw), n_ctx, l)
    gd_rows = gd[:, :, :4 * GDN_HEADS].reshape(b, n_chunks, CHUNK, 4 * GDN_HEADS).transpose(0, 1, 3, 2)
    oaf, oab = _gdn_scan(qkv, gd, gd_rows, n_ctx_chunks)

    embed = lambda w, lo: jnp.zeros((LANE, kw), F32).at[lo:lo + GLA_RANK].set(w)
    obf, obb = _gla_scan(p, lr, embed(gla_gate_w_f[0], 0), embed(gla_gate_w_b[0], GLA_RANK),
                         gla_gate_b_f, gla_gate_b_b, n_ctx_chunks, qk_col=4, v_col=5)

    wr = jnp.zeros((d, LANE), F32).at[:, :N_EXPERTS].set(w_router[0])
    br = jnp.zeros((1, LANE), F32).at[0, :N_EXPERTS].set(b_router[0])
    x1, h2, logits = _merge(oaf, oab, obf, obb, p, x, g1, sh2, sc2,
                            jnp.tile(gdn_norm_g, (1, GDN_HEADS)), jnp.tile(gla_norm_g, (1, GLA_HEADS)), norm_ffn_g,
                            w_out_a[0].astype(BF16), w_out_b[0].astype(BF16), w_out[0].astype(BF16), wr, br,
                            cols=(3, 6, 7, 8))

    return _channel_mixing(x1, h2, logits, g2, w_gu, b_gu, w_down, b_down, final_norm_g)


def _channel_mixing(x1, h2, logits, g2, w_gu, b_gu, w_down, b_down, final_norm_g):
    b, l, d = x1.shape
    t = b * l
    slot_tok, dest, block_expert, top_p = _routing(logits.reshape(t, LANE)[:, :N_EXPERTS])
    yb = _experts(h2, slot_tok, block_expert, w_gu[0].astype(BF16), b_gu[0], w_down[0].astype(BF16), b_down[0])
    return _combine(x1, yb, dest, top_p.reshape(b, l, TOP_K), g2, final_norm_g.reshape(1, d))
```

```python
import functools

import jax
import jax.numpy as jnp
from jax import lax
from jax.experimental import pallas as pl
from jax.experimental.pallas import tpu as pltpu

F32 = jnp.float32
BF16 = jnp.bfloat16
HIGHEST = lax.Precision.HIGHEST

EPS = 1e-6
GRID_W = 64
CHUNK = 64
TOK_BLOCK = 256
CONV_ROWS = 512
GDN_HEADS = 8
GDN_DIM = 128
GLA_HEADS = 4
GLA_DK = 128
GLA_DV = 256
GLA_RANK = 16
GLA_NORMALIZER = 16.0
N_EXPERTS = 32
TOP_K = 4
SWIGLU_LIMIT = 7.0
SWIGLU_ALPHA = 1.702
EXPERT_BLOCK = 512
LANE = 128
SUBLANE = 8
VMEM_LIMIT = 56 * 1024 * 1024


def _silu(t):
    return t * jax.nn.sigmoid(t)


def _softplus(t):
    return jnp.maximum(t, 0.0) + jnp.log1p(jnp.exp(-jnp.abs(t)))


def _dot(a, b, **kw):
    return jnp.dot(a, b, preferred_element_type=F32, **kw)


def _dot_nt(a, b, **kw):
    return lax.dot_general(a, b, (((1,), (1,)), ((), ())), preferred_element_type=F32, **kw)


def _dot_tn(a, b, **kw):
    return lax.dot_general(a, b, (((0,), (0,)), ((), ())), preferred_element_type=F32, **kw)


def _store_tile_rows(ref, val):
    rows, width = val.shape
    groups = width // LANE
    for s in range(groups):
        ref[pl.ds(s, rows, stride=groups), :] = val[:, s * LANE:(s + 1) * LANE]


def _load_tile_rows(ref, rows, groups, first=0, stride=None):
    stride = groups if stride is None else stride
    return jnp.concatenate([ref[pl.ds(first + s, rows, stride=stride), :] for s in range(groups)], axis=-1)


def _mod_kernel(c_ref, w_ref, b_ref, o_ref):
    o_ref[...] = _dot(_silu(c_ref[...]), w_ref[...], precision=HIGHEST) + b_ref[...]


def _modulation(cc, w_mod, b_mod):
    rows, d = cc.shape
    n = w_mod.shape[1]
    tn = n // 4
    return pl.pallas_call(
        _mod_kernel,
        out_shape=jax.ShapeDtypeStruct((rows, n), F32),
        grid=(n // tn,),
        in_specs=[pl.BlockSpec((rows, d), lambda j: (0, 0)),
                  pl.BlockSpec((d, tn), lambda j: (0, j)),
                  pl.BlockSpec((1, tn), lambda j: (0, j))],
        out_specs=pl.BlockSpec((rows, tn), lambda j: (0, j)),
        compiler_params=pltpu.CompilerParams(vmem_limit_bytes=VMEM_LIMIT),
        name="mod",
    )(cc, w_mod, b_mod.reshape(1, n))


def _inproj_kernel(x_ref, ctx_ref, shl_ref, scl_ref, shc_ref, scc_ref, g_ref, w_ref, ws_ref, ca_ref, cb_ref,
                   p_ref, gd_ref, lr_ref):
    is_ctx = pl.program_id(1) == 0
    xin = jnp.where(is_ctx, ctx_ref[0], x_ref[0])
    shift = jnp.where(is_ctx, shc_ref[...], shl_ref[0])
    scale = jnp.where(is_ctx, scc_ref[...], scl_ref[0])
    ms = jnp.mean(xin * xin, axis=-1, keepdims=True)
    h = (xin * lax.rsqrt(ms + EPS) * g_ref[...]) * (1.0 + scale) + shift
    hb = h.astype(BF16)
    n_main = w_ref.shape[1]
    for j in range(n_main // 1024):
        p_ref[0, :, j * 1024:(j + 1) * 1024] = _dot(hb, w_ref[:, j * 1024:(j + 1) * 1024])
    sm = _dot(h, ws_ref[...], precision=HIGHEST)
    raw = sm[:, :LANE]
    lane = lax.broadcasted_iota(jnp.int32, raw.shape, 1)
    beta = jax.nn.sigmoid(raw)
    g = ca_ref[...] * _softplus(raw + cb_ref[...])
    gd_ref[0] = jnp.where(lane < 2 * GDN_HEADS, beta, jnp.where(lane < 4 * GDN_HEADS, g, 0.0))
    lr_ref[0] = sm[:, LANE:]


def _input_projection(x, ctx, sh_lat, sc_lat, sh_ctx, sc_ctx, norm_g, w_main, w_small, coef_a, coef_b):
    b, l, d = x.shape
    n_blk = l // TOK_BLOCK + 1
    n_main = w_main.shape[1]
    tot = n_blk * TOK_BLOCK
    row3 = lambda i, s: (i, 0, 0)
    const2 = lambda i, s: (0, 0)
    blk = lambda i, s: (i, s, 0)
    return pl.pallas_call(
        _inproj_kernel,
        out_shape=(jax.ShapeDtypeStruct((b, tot, n_main), F32),
                   jax.ShapeDtypeStruct((b, tot, LANE), F32),
                   jax.ShapeDtypeStruct((b, tot, LANE), F32)),
        grid=(b, n_blk),
        in_specs=[pl.BlockSpec((1, TOK_BLOCK, d), lambda i, s: (i, jnp.maximum(s - 1, 0), 0)),
                  pl.BlockSpec((1, TOK_BLOCK, d), row3),
                  pl.BlockSpec((1, 1, d), row3), pl.BlockSpec((1, 1, d), row3),
                  pl.BlockSpec((1, d), const2), pl.BlockSpec((1, d), const2),
                  pl.BlockSpec((1, d), const2),
                  pl.BlockSpec((d, n_main), const2, pipeline_mode=pl.Buffered(1)),
                  pl.BlockSpec((d, 2 * LANE), const2),
                  pl.BlockSpec((1, LANE), const2), pl.BlockSpec((1, LANE), const2)],
        out_specs=(pl.BlockSpec((1, TOK_BLOCK, n_main), blk),
                   pl.BlockSpec((1, TOK_BLOCK, LANE), blk),
                   pl.BlockSpec((1, TOK_BLOCK, LANE), blk)),
        compiler_params=pltpu.CompilerParams(dimension_semantics=("arbitrary", "arbitrary"),
                                             vmem_limit_bytes=VMEM_LIMIT),
        name="inproj",
    )(x, ctx, sh_lat, sc_lat, sh_ctx, sc_ctx, norm_g, w_main, w_small, coef_a, coef_b)


def _conv_kernel(p_ref, w_ref, o_ref, pad_ref, *, n_ctx, n_lat, tc, qk_cols):
    is_qk = pl.program_id(1) * tc < qk_cols
    w = w_ref[...]

    def post(y):
        s = _silu(y)
        outs = []
        for hh in range(tc // GDN_DIM):
            sh = s[:, hh * GDN_DIM:(hh + 1) * GDN_DIM]
            fac = lax.rsqrt(jnp.sum(sh * sh, axis=-1, keepdims=True) + EPS)
            outs.append(sh * jnp.where(is_qk, fac, 1.0))
        return outs[0] if len(outs) == 1 else jnp.concatenate(outs, axis=-1)

    xc = p_ref[0, 0:n_ctx, :]
    ridx = lax.broadcasted_iota(jnp.int32, (n_ctx, 1), 0)
    yc = (jnp.where(ridx == 0, 0.0, pltpu.roll(xc * w[3:4], 1, 0)) + xc * w[4:5]
          + jnp.where(ridx == n_ctx - 1, 0.0, pltpu.roll(xc * w[5:6], n_ctx - 1, 0)))
    o_ref[0, 0:n_ctx, :] = post(yc)

    zeros = jnp.zeros((GRID_W, tc), F32)
    pad_ref[0:GRID_W, :] = zeros
    pad_ref[GRID_W + n_lat:GRID_W + n_lat + GRID_W, :] = zeros
    pad_ref[GRID_W:GRID_W + n_lat, :] = p_ref[0, n_ctx:n_ctx + n_lat, :]
    rows = min(CONV_ROWS, n_lat)
    col = lax.broadcasted_iota(jnp.int32, (rows, 1), 0) % GRID_W

    def body(i, carry):
        s = pl.multiple_of(i * rows, rows)
        up = pad_ref[pl.ds(s, rows), :]
        mid = pad_ref[pl.ds(s + GRID_W, rows), :]
        dn = pad_ref[pl.ds(s + 2 * GRID_W, rows), :]
        a0 = up * w[0:1] + mid * w[3:4] + dn * w[6:7]
        a1 = up * w[1:2] + mid * w[4:5] + dn * w[7:8]
        a2 = up * w[2:3] + mid * w[5:6] + dn * w[8:9]
        y = (jnp.where(col == 0, 0.0, pltpu.roll(a0, 1, 0)) + a1
             + jnp.where(col == GRID_W - 1, 0.0, pltpu.roll(a2, rows - 1, 0)))
        o_ref[0, pl.ds(n_ctx + s, rows), :] = post(y)
        return carry

    lax.fori_loop(0, n_lat // rows, body, 0)


def _conv_qkv(p, conv_w9, n_ctx, n_lat):
    b, tot, _ = p.shape
    c = conv_w9.shape[1]
    tc = 2 * GDN_DIM
    kern = functools.partial(_conv_kernel, n_ctx=n_ctx, n_lat=n_lat, tc=tc, qk_cols=2 * GDN_HEADS * GDN_DIM)
    return pl.pallas_call(
        kern,
        out_shape=jax.ShapeDtypeStruct((b, tot, c), F32),
        grid=(b, c // tc),
        in_specs=[pl.BlockSpec((1, tot, tc), lambda i, j: (i, 0, j)),
                  pl.BlockSpec((9, tc), lambda i, j: (0, j))],
        out_specs=pl.BlockSpec((1, tot, tc), lambda i, j: (i, 0, j)),
        scratch_shapes=[pltpu.VMEM((n_lat + 2 * GRID_W, tc), F32)],
        compiler_params=pltpu.CompilerParams(dimension_semantics=("arbitrary", "arbitrary"),
                                             vmem_limit_bytes=VMEM_LIMIT),
        name="conv",
    )(p, conv_w9)


def _bwd_chunk(n, n_ctx_chunks, n_chunks):
    return jnp.where(n < n_ctx_chunks, n_ctx_chunks - 1 - n, n_chunks + n_ctx_chunks - 1 - n)


def _tri_masks():
    r = lax.broadcasted_iota(jnp.int32, (CHUNK, CHUNK), 0)
    c = lax.broadcasted_iota(jnp.int32, (CHUNK, CHUNK), 1)
    return r >= c, r <= c, r == c


PREP_BATCH = 4
PREP_CHUNKS = TOK_BLOCK // CHUNK
SLAB = CHUNK + SUBLANE


def _substitute(n_scr, t_scr, reverse):
    t_scr[...] = jnp.zeros(t_scr.shape, F32)
    n_seg = CHUNK // SUBLANE
    sub = lax.broadcasted_iota(jnp.int32, (SUBLANE, LANE), 0)
    for seg in (reversed(range(n_seg)) if reverse else range(n_seg)):
        js = range(SUBLANE * seg, CHUNK) if reverse else range(0, SUBLANE * (seg + 1))
        acc_groups = range(seg, n_seg) if reverse else range(0, seg + 1)

        def body(step, carry, seg=seg, js=js, acc_groups=acc_groups):
            i = SUBLANE * seg + (SUBLANE - 1 - step if reverse else step)
            base = pl.multiple_of(i * CHUNK, CHUNK)
            acc = {g: jnp.zeros((SUBLANE, LANE), F32) for g in acc_groups}
            for j in js:
                nij = jnp.broadcast_to(n_scr[pl.ds(base + j, 1), :], (SUBLANE, LANE))
                for g in (range(j // SUBLANE, n_seg) if reverse else range(0, j // SUBLANE + 1)):
                    acc[g] = acc[g] + nij * t_scr[j * CHUNK + g * SUBLANE:j * CHUNK + (g + 1) * SUBLANE, :]
            acc[seg] = acc[seg] + jnp.where(sub + SUBLANE * seg == i, 1.0, 0.0)
            for g in acc_groups:
                t_scr[pl.ds(base + g * SUBLANE, SUBLANE), :] = acc[g]
            return carry

        lax.fori_loop(0, SUBLANE, body, 0)


def _gdn_prep_kernel(q_ref, k_ref, gc_ref, gr_ref, ta_ref, g_scr, n_scr, t_scr, r_scr):
    scale = GDN_DIM ** -0.5
    r = lax.broadcasted_iota(jnp.int32, (CHUNK, LANE), 0)
    c = lax.broadcasted_iota(jnp.int32, (CHUNK, LANE), 1)
    incl = (jnp.logical_and(r >= c, c < CHUNK), jnp.logical_and(r <= c, c < CHUNK))
    strict = (jnp.logical_and(r > c, c < CHUNK), jnp.logical_and(r < c, c < CHUNK))
    rs = lax.broadcasted_iota(jnp.int32, (CHUNK, CHUNK), 0)
    cs = lax.broadcasted_iota(jnp.int32, (CHUNK, CHUNK), 1)
    tri_c = ((rs >= cs).astype(F32), (rs <= cs).astype(F32))
    rl = lax.broadcasted_iota(jnp.int32, (LANE, LANE), 0)
    cl = lax.broadcasted_iota(jnp.int32, (LANE, LANE), 1)
    live = jnp.logical_and(rl < CHUNK, cl < CHUNK)
    tri_r = (jnp.logical_and(rl <= cl, live).astype(F32), jnp.logical_and(rl >= cl, live).astype(F32))
    k_zero = jnp.zeros((CHUNK, GDN_DIM), BF16)

    def per_chunk(fn):
        def outer(bb, carry):
            def inner(ch, carry2):
                fn(bb, ch, pl.multiple_of(ch * CHUNK, CHUNK), (bb * PREP_CHUNKS + ch) * GDN_HEADS)
                return carry2
            return lax.fori_loop(0, PREP_CHUNKS, inner, carry)
        lax.fori_loop(0, PREP_BATCH, outer, 0)

    def build(bb, ch, row0, p0):
        gcol = gc_ref[bb, pl.ds(row0, CHUNK), :]
        grow = gr_ref[bb, ch]
        cum_c = [_dot(tri_c[d], gcol, precision=HIGHEST) for d in (0, 1)]
        cum_r = [_dot(grow, tri_r[d], precision=HIGHEST) for d in (0, 1)]
        for h in range(GDN_HEADS):
            sl = slice(h * GDN_DIM, (h + 1) * GDN_DIM)
            kb = k_ref[bb, pl.ds(row0, CHUNK), sl].astype(BF16)
            qk = jnp.concatenate([q_ref[bb, pl.ds(row0, CHUNK), sl].astype(BF16), kb], axis=0)
            gram = _dot_nt(qk, jnp.concatenate([kb, k_zero], axis=0))
            for d in (0, 1):
                bl = GDN_HEADS * d + h
                gl = 2 * GDN_HEADS + bl
                dec = jnp.exp(jnp.where(incl[d], cum_c[d][:, gl:gl + 1] - cum_r[d][gl:gl + 1, :], -jnp.inf))
                nm = jnp.where(strict[d], -(gram[CHUNK:] * gcol[:, bl:bl + 1] * dec), 0.0)
                g_scr[d, pl.ds(pl.multiple_of((p0 + h) * SLAB, SUBLANE), CHUNK), :] = nm
                ta_ref[bb, ch, d, h, CHUNK:2 * CHUNK, :] = (gram[:CHUNK] * scale * dec)[:, :CHUNK].astype(BF16)

    per_chunk(build)

    for d in (0, 1):
        def to_lanes(i, carry, d=d):
            blk = g_scr[d, pl.ds(i, LANE, stride=SLAB), :]
            n_scr[pl.ds(pl.multiple_of(i * CHUNK, CHUNK), CHUNK), :] = blk.T[:CHUNK]
            return carry

        lax.fori_loop(0, CHUNK, to_lanes, 0)
        _substitute(n_scr, t_scr, reverse=(d == 1))

        def from_lanes(i, carry):
            blk = t_scr[pl.ds(pl.multiple_of(i * CHUNK, CHUNK), LANE), :]
            r_scr[pl.ds(i, LANE, stride=SLAB), :] = blk.T
            return carry

        lax.fori_loop(0, CHUNK, from_lanes, 0)

        def finish(bb, ch, row0, p0, d=d):
            grow = gr_ref[bb, ch]
            for h in range(GDN_HEADS):
                bl = GDN_HEADS * d + h
                t = r_scr[pl.ds(pl.multiple_of((p0 + h) * SLAB, SUBLANE), CHUNK), :]
                ta_ref[bb, ch, d, h, 0:CHUNK, :] = (t * grow[bl:bl + 1, :])[:, :CHUNK].astype(BF16)

        per_chunk(finish)


def _gdn_prep(qkv, gd, gd_rows):
    b, tot, _ = qkv.shape
    n_chunks = tot // CHUNK
    w = GDN_HEADS * GDN_DIM
    assert PREP_BATCH * PREP_CHUNKS * GDN_HEADS == LANE and b % PREP_BATCH == 0
    return pl.pallas_call(
        _gdn_prep_kernel,
        out_shape=jax.ShapeDtypeStruct((b, n_chunks, 2, GDN_HEADS, 2 * CHUNK, CHUNK), BF16),
        grid=(b // PREP_BATCH, tot // TOK_BLOCK),
        in_specs=[pl.BlockSpec((PREP_BATCH, TOK_BLOCK, w), lambda i, s: (i, s, 0)),
                  pl.BlockSpec((PREP_BATCH, TOK_BLOCK, w), lambda i, s: (i, s, 1)),
                  pl.BlockSpec((PREP_BATCH, TOK_BLOCK, LANE), lambda i, s: (i, s, 0)),
                  pl.BlockSpec((PREP_BATCH, PREP_CHUNKS, 4 * GDN_HEADS, LANE), lambda i, s: (i, s, 0, 0))],
        out_specs=pl.BlockSpec((PREP_BATCH, PREP_CHUNKS, 2, GDN_HEADS, 2 * CHUNK, CHUNK),
                               lambda i, s: (i, s, 0, 0, 0, 0)),
        scratch_shapes=[pltpu.VMEM((2, LANE * SLAB, LANE), F32),
                        pltpu.VMEM((CHUNK * CHUNK, LANE), F32),
                        pltpu.VMEM((CHUNK * CHUNK + CHUNK, LANE), F32),
                        pltpu.VMEM((LANE * SLAB, LANE), F32)],
        compiler_params=pltpu.CompilerParams(dimension_semantics=("arbitrary", "arbitrary"),
                                             vmem_limit_bytes=VMEM_LIMIT),
        name="gdn_prep",
    )(qkv, qkv, gd, gd_rows)


def _gdn_dir(q_ref, k_ref, v_ref, gc_ref, ta_ref, o_ref, s_ref, d, incl):
    scale = GDN_DIM ** -0.5
    cum_c = _dot(incl.astype(F32), gc_ref[0], precision=HIGHEST)
    last = CHUNK - 1 if d == 0 else 0
    for h in range(GDN_HEADS):
        bl = GDN_HEADS * d + h
        gl = 2 * GDN_HEADS + bl
        sl = slice(h * GDN_DIM, (h + 1) * GDN_DIM)
        gc = cum_c[:, gl:gl + 1]
        k = k_ref[0, :, sl]
        qk = jnp.concatenate([q_ref[0, :, sl], k], axis=0).astype(BF16)
        ta = ta_ref[0, 0, 0, h]
        s_old = s_ref[bl]
        qks = _dot(qk, s_old.astype(BF16))
        eg = jnp.exp(gc)
        x = v_ref[0, :, sl] - eg * qks[CHUNK:]
        v_new = _dot(ta[:CHUNK], x.astype(BF16)).astype(BF16)
        o_ref[0, :, sl] = (scale * eg) * qks[:CHUNK] + _dot(ta[CHUNK:], v_new)
        g_tot = gc[last:last + 1, :]
        kd = k * jnp.exp(g_tot - gc)
        s_ref[bl] = s_old * jnp.exp(g_tot) + _dot_tn(kd.astype(BF16), v_new)


def _gdn_kernel(qf, kf, vf, qb, kb, vb, gcf, gcb, taf, tab, of, ob, s_ref):
    @pl.when(pl.program_id(1) == 0)
    def _():
        s_ref[...] = jnp.zeros(s_ref.shape, F32)

    lower, upper, _ = _tri_masks()
    _gdn_dir(qf, kf, vf, gcf, taf, of, s_ref, 0, lower)
    _gdn_dir(qb, kb, vb, gcb, tab, ob, s_ref, 1, upper)


def _gdn_scan(qkv, gd, ta, n_ctx_chunks):
    b, tot, _ = qkv.shape
    n_chunks = tot // CHUNK
    w = GDN_HEADS * GDN_DIM
    bwd = functools.partial(_bwd_chunk, n_ctx_chunks=n_ctx_chunks, n_chunks=n_chunks)

    def tok(col, rev):
        if rev:
            return pl.BlockSpec((1, CHUNK, w), lambda i, n: (i, bwd(n), col))
        return pl.BlockSpec((1, CHUNK, w), lambda i, n: (i, n, col))

    gcol_f = pl.BlockSpec((1, CHUNK, LANE), lambda i, n: (i, n, 0))
    gcol_b = pl.BlockSpec((1, CHUNK, LANE), lambda i, n: (i, bwd(n), 0))
    ta_blk = (1, 1, 1, GDN_HEADS, 2 * CHUNK, CHUNK)
    ta_f = pl.BlockSpec(ta_blk, lambda i, n: (i, n, 0, 0, 0, 0))
    ta_b = pl.BlockSpec(ta_blk, lambda i, n: (i, bwd(n), 1, 0, 0, 0))
    out = jax.ShapeDtypeStruct((b, tot, w), F32)
    return pl.pallas_call(
        _gdn_kernel,
        out_shape=(out, out),
        grid=(b, n_chunks),
        in_specs=[tok(0, False), tok(1, False), tok(2, False), tok(0, True), tok(1, True), tok(2, True),
                  gcol_f, gcol_b, ta_f, ta_b],
        out_specs=(tok(0, False), tok(0, True)),
        scratch_shapes=[pltpu.VMEM((2 * GDN_HEADS, GDN_DIM, GDN_DIM), F32)],
        compiler_params=pltpu.CompilerParams(dimension_semantics=("arbitrary", "arbitrary"),
                                             vmem_limit_bytes=VMEM_LIMIT),
        name="gdn",
    )(qkv, qkv, qkv, qkv, qkv, qkv, gd, gd, ta, ta)


def _gla_dir(qk_ref, v_ref, lr_ref, w_ref, b_ref, o_ref, s_ref, d, incl):
    scale = GLA_DK ** -0.5
    kw = GLA_HEADS * GLA_DK
    z = _dot(lr_ref[0], w_ref[...], precision=HIGHEST) + b_ref[...]
    log_a = -_softplus(-z) * (1.0 / GLA_NORMALIZER)
    cum = _dot(incl.astype(F32), log_a, precision=HIGHEST)
    centre = CHUNK // 2 if d == 0 else CHUNK - 1 - CHUNK // 2
    last = CHUNK - 1 if d == 0 else 0
    for h in range(GLA_HEADS):
        idx = GLA_HEADS * d + h
        q = qk_ref[0, :, h * GLA_DK:(h + 1) * GLA_DK] * scale
        k = qk_ref[0, :, kw + h * GLA_DK:kw + (h + 1) * GLA_DK]
        v = v_ref[0, :, h * GLA_DV:(h + 1) * GLA_DV].astype(BF16)
        gc = cum[:, h * GLA_DK:(h + 1) * GLA_DK]
        g_ref = gc[centre:centre + 1, :]
        g_tot = gc[last:last + 1, :]
        attn = _dot_nt((q * jnp.exp(gc - g_ref)).astype(BF16), (k * jnp.exp(g_ref - gc)).astype(BF16))
        attn = jnp.where(incl, attn, 0.0)
        s_old = s_ref[idx]
        o = _dot(attn.astype(BF16), v) + _dot_nt((q * jnp.exp(gc)).astype(BF16), s_old.astype(BF16))
        o_ref[0, :, h * GLA_DV:(h + 1) * GLA_DV] = o
        kd = k * jnp.exp(g_tot - gc)
        s_ref[idx] = s_old * jnp.exp(g_tot) + _dot_tn(v, kd.astype(BF16))


def _gla_kernel(qkf, vf, qkb, vb, lrf, lrb, wf, wb, bf, bb, of, ob, s_ref):
    @pl.when(pl.program_id(1) == 0)
    def _():
        s_ref[...] = jnp.zeros(s_ref.shape, F32)

    lower, upper, _ = _tri_masks()
    _gla_dir(qkf, vf, lrf, wf, bf, of, s_ref, 0, lower)
    _gla_dir(qkb, vb, lrb, wb, bb, ob, s_ref, 1, upper)


def _gla_scan(p, lr, wf, wb, bf, bb, n_ctx_chunks, qk_col, v_col):
    b, tot, _ = p.shape
    n_chunks = tot // CHUNK
    w = GLA_HEADS * GLA_DV
    bwd = functools.partial(_bwd_chunk, n_ctx_chunks=n_ctx_chunks, n_chunks=n_chunks)

    def tok(width, col, rev):
        if rev:
            return pl.BlockSpec((1, CHUNK, width), lambda i, n: (i, bwd(n), col))
        return pl.BlockSpec((1, CHUNK, width), lambda i, n: (i, n, col))

    const2 = lambda i, n: (0, 0)
    out = jax.ShapeDtypeStruct((b, tot, w), F32)
    return pl.pallas_call(
        _gla_kernel,
        out_shape=(out, out),
        grid=(b, n_chunks),
        in_specs=[tok(w, qk_col, False), tok(w, v_col, False), tok(w, qk_col, True), tok(w, v_col, True),
                  tok(LANE, 0, False), tok(LANE, 0, True),
                  pl.BlockSpec(wf.shape, const2), pl.BlockSpec(wb.shape, const2),
                  pl.BlockSpec(bf.shape, const2), pl.BlockSpec(bb.shape, const2)],
        out_specs=(tok(w, 0, False), tok(w, 0, True)),
        scratch_shapes=[pltpu.VMEM((2 * GLA_HEADS, GLA_DV, GLA_DK), F32)],
        compiler_params=pltpu.CompilerParams(dimension_semantics=("arbitrary", "arbitrary"),
                                             vmem_limit_bytes=VMEM_LIMIT),
        name="gla",
    )(p, p, p, p, lr, lr, wf, wb, bf, bb)


def _head_rms(o, g, width):
    outs = []
    for h in range(o.shape[-1] // width):
        oh = o[:, h * width:(h + 1) * width]
        ms = jnp.mean(oh * oh, axis=-1, keepdims=True)
        outs.append(oh * lax.rsqrt(ms + EPS))
    return jnp.concatenate(outs, axis=-1) * g


def _merge_kernel(oaf, oab, obf, obb, z_ref, r_ref, ga_ref, gb_ref, x_ref, g1_ref, sh2_ref, sc2_ref,
                  na_ref, nb_ref, nf_ref, woa_ref, wob_ref, wo_ref, wr_ref, br_ref,
                  x1_ref, h2_ref, lg_ref):
    ya_in = _head_rms(oaf[0] + oab[0], na_ref[...], GDN_DIM) * _silu(z_ref[0])
    ya = _dot(ya_in.astype(BF16), woa_ref[...])
    yb_in = _head_rms(obf[0] + obb[0], nb_ref[...], GLA_DV) * _silu(r_ref[0])
    yb = _dot(yb_in.astype(BF16), wob_ref[...])
    y = jax.nn.sigmoid(ga_ref[0]) * ya + jax.nn.sigmoid(gb_ref[0]) * yb
    x1 = x_ref[0] + g1_ref[0] * _dot(y.astype(BF16), wo_ref[...])
    x1_ref[0] = x1
    ms = jnp.mean(x1 * x1, axis=-1, keepdims=True)
    h2 = (x1 * lax.rsqrt(ms + EPS) * nf_ref[...]) * (1.0 + sc2_ref[0]) + sh2_ref[0]
    _store_tile_rows(h2_ref, h2)
    lg_ref[0] = _dot(h2, wr_ref[...], precision=HIGHEST) + br_ref[...]


def _merge(oaf, oab, obf, obb, p, x, g1, sh2, sc2, na, nb, nf, woa, wob, wo, wr, br, cols):
    b, l, d = x.shape
    z_col, r_col, ga_col, gb_col = cols
    lat = lambda i, s: (i, s + 1, 0)
    row3 = lambda i, s: (i, 0, 0)
    const2 = lambda i, s: (0, 0)
    blk = lambda i, s: (i, s, 0)
    tb = (1, TOK_BLOCK, d)
    pcol = lambda c: pl.BlockSpec(tb, lambda i, s: (i, s + 1, c))
    vec = pl.BlockSpec((1, 1, d), row3)
    cvec = pl.BlockSpec((1, d), const2)
    wspec = pl.BlockSpec((d, d), const2)
    groups = d // LANE
    n_blk = l // TOK_BLOCK
    return pl.pallas_call(
        _merge_kernel,
        out_shape=(jax.ShapeDtypeStruct((b, l, d), F32),
                   jax.ShapeDtypeStruct((b * l * groups, LANE), F32),
                   jax.ShapeDtypeStruct((b, l, LANE), F32)),
        grid=(b, n_blk),
        in_specs=[pl.BlockSpec(tb, lat)] * 4 + [pcol(z_col), pcol(r_col), pcol(ga_col), pcol(gb_col),
                  pl.BlockSpec(tb, blk), vec, vec, vec, cvec, cvec, cvec, wspec, wspec, wspec,
                  pl.BlockSpec((d, LANE), const2), pl.BlockSpec((1, LANE), const2)],
        out_specs=(pl.BlockSpec(tb, blk),
                   pl.BlockSpec((TOK_BLOCK * groups, LANE), lambda i, s: (i * n_blk + s, 0)),
                   pl.BlockSpec((1, TOK_BLOCK, LANE), blk)),
        compiler_params=pltpu.CompilerParams(dimension_semantics=("arbitrary", "arbitrary"),
                                             vmem_limit_bytes=VMEM_LIMIT),
        name="merge",
    )(oaf, oab, obf, obb, p, p, p, p, x, g1, sh2, sc2, na, nb, nf, woa, wob, wo, wr, br)


def _row_copy(src_ref, dst_ref, sem, src_row, dst_row):
    src = pl.multiple_of(src_row * SUBLANE, SUBLANE)
    dst = pl.multiple_of(dst_row * SUBLANE, SUBLANE)
    return pltpu.make_async_copy(src_ref.at[pl.ds(src, SUBLANE)], dst_ref.at[pl.ds(dst, SUBLANE)], sem)


def _gather_step(step, n_steps, rows, idx_hbm, src_hbm, idx_smem, buf, idx_sem, row_sem):
    def idx_copy(s):
        return pltpu.make_async_copy(idx_hbm.at[s], idx_smem.at[s % 2], idx_sem.at[s % 2])

    def issue(s):
        slot = s % 2

        def body(r, carry):
            _row_copy(src_hbm, buf.at[slot], row_sem.at[slot], idx_smem[slot, r], r).start()
            return carry

        lax.fori_loop(0, rows, body, 0)

    @pl.when(step == 0)
    def _():
        idx_copy(0).start()
        idx_copy(0).wait()
        issue(0)

        @pl.when(n_steps > 1)
        def _():
            idx_copy(1).start()

    @pl.when(step + 1 < n_steps)
    def _():
        idx_copy(step + 1).wait()
        issue(step + 1)

        @pl.when(step + 2 < n_steps)
        def _():
            idx_copy(step + 2).start()

    slot = step % 2
    pltpu.make_async_copy(src_hbm.at[pl.ds(0, rows * SUBLANE)], buf.at[slot], row_sem.at[slot]).wait()
    return buf.at[slot]


def _gather_scratch(rows):
    return [pltpu.SMEM((2, rows), jnp.int32), pltpu.VMEM((2, rows * SUBLANE, LANE), F32),
            pltpu.SemaphoreType.DMA((2,)), pltpu.SemaphoreType.DMA((2,))]


def _expert_kernel(be_ref, tok_ref, h_ref, wgu_ref, bgu_ref, wd_ref, bd_ref, y_ref, *gather_scratch):
    del be_ref
    de = wd_ref.shape[1]
    groups = wgu_ref.shape[1] // LANE
    x_ref = _gather_step(pl.program_id(0), pl.num_programs(0), EXPERT_BLOCK, tok_ref, h_ref, *gather_scratch)
    x = _load_tile_rows(x_ref, EXPERT_BLOCK, groups)
    gu = _dot(x.astype(BF16), wgu_ref[0]) + bgu_ref[0]
    glu = jnp.minimum(gu[:, :de], SWIGLU_LIMIT)
    lin = jnp.clip(gu[:, de:], -SWIGLU_LIMIT, SWIGLU_LIMIT)
    act = glu * jax.nn.sigmoid(SWIGLU_ALPHA * glu) * (lin + 1.0)
    _store_tile_rows(y_ref, _dot(act.astype(BF16), wd_ref[0]) + bd_ref[0])


def _experts(h, slot_tok, block_expert, w_gu, b_gu, w_down, b_down):
    ne, d, de2 = w_gu.shape
    de = de2 // 2
    groups = d // LANE
    assert groups == SUBLANE
    n_slots = slot_tok.shape[0]
    blk_rows = EXPERT_BLOCK * groups
    grid_spec = pltpu.PrefetchScalarGridSpec(
        num_scalar_prefetch=1,
        grid=(n_slots // EXPERT_BLOCK,),
        in_specs=[pl.BlockSpec(memory_space=pl.ANY), pl.BlockSpec(memory_space=pl.ANY),
                  pl.BlockSpec((1, d, de2), lambda i, be: (be[i], 0, 0)),
                  pl.BlockSpec((1, 1, de2), lambda i, be: (be[i], 0, 0)),
                  pl.BlockSpec((1, de, d), lambda i, be: (be[i], 0, 0)),
                  pl.BlockSpec((1, 1, d), lambda i, be: (be[i], 0, 0))],
        out_specs=pl.BlockSpec((blk_rows, LANE), lambda i, be: (i, 0)),
        scratch_shapes=_gather_scratch(EXPERT_BLOCK),
    )
    return pl.pallas_call(
        _expert_kernel,
        out_shape=jax.ShapeDtypeStruct((n_slots * groups, LANE), F32),
        grid_spec=grid_spec,
        compiler_params=pltpu.CompilerParams(dimension_semantics=("arbitrary",), vmem_limit_bytes=VMEM_LIMIT),
        name="experts",
    )(block_expert, slot_tok.reshape(n_slots // EXPERT_BLOCK, EXPERT_BLOCK), h,
      w_gu, b_gu.reshape(ne, 1, de2), w_down, b_down.reshape(ne, 1, d))


def _combine_kernel(dest_ref, yb_ref, x1_ref, p_ref, g2_ref, fg_ref, o_ref, *gather_scratch):
    groups = x1_ref.shape[-1] // LANE
    step = pl.program_id(0) * pl.num_programs(1) + pl.program_id(1)
    n_steps = pl.num_programs(0) * pl.num_programs(1)
    y4_ref = _gather_step(step, n_steps, TOK_BLOCK * TOP_K, dest_ref, yb_ref, *gather_scratch)
    p = p_ref[0]
    y = None
    for k in range(TOP_K):
        yk = p[:, k:k + 1] * _load_tile_rows(y4_ref, TOK_BLOCK, groups, first=k * groups, stride=TOP_K * groups)
        y = yk if y is None else y + yk
    x2 = x1_ref[0] + g2_ref[0] * y
    ms = jnp.mean(x2 * x2, axis=-1, keepdims=True)
    o_ref[0] = x2 * lax.rsqrt(ms + EPS) * fg_ref[...]


def _combine(x1, yb, dest, top_p, g2, final_g):
    b, l, d = x1.shape
    assert d // LANE == SUBLANE
    blk = lambda i, s: (i, s, 0)
    n_blk = l // TOK_BLOCK
    rows = TOK_BLOCK * TOP_K
    return pl.pallas_call(
        _combine_kernel,
        out_shape=jax.ShapeDtypeStruct((b, l, d), F32),
        grid=(b, n_blk),
        in_specs=[pl.BlockSpec(memory_space=pl.ANY), pl.BlockSpec(memory_space=pl.ANY),
                  pl.BlockSpec((1, TOK_BLOCK, d), blk),
                  pl.BlockSpec((1, TOK_BLOCK, TOP_K), blk),
                  pl.BlockSpec((1, 1, d), lambda i, s: (i, 0, 0)),
                  pl.BlockSpec((1, d), lambda i, s: (0, 0))],
        out_specs=pl.BlockSpec((1, TOK_BLOCK, d), blk),
        scratch_shapes=_gather_scratch(rows),
        compiler_params=pltpu.CompilerParams(dimension_semantics=("arbitrary", "arbitrary"),
                                             vmem_limit_bytes=VMEM_LIMIT),
        name="combine",
    )(dest.reshape(b * n_blk, rows), yb, x1, top_p, g2, final_g)


def _routing(logits):
    t = logits.shape[0]
    n_assign = t * TOP_K
    n_slots = -(-(n_assign + N_EXPERTS * (EXPERT_BLOCK - 1)) // EXPERT_BLOCK) * EXPERT_BLOCK
    top_logit, top_idx = lax.top_k(logits, TOP_K)
    top_p = jax.nn.softmax(top_logit, axis=-1)
    flat_e = top_idx.reshape(-1).astype(jnp.int32)
    order = jnp.argsort(flat_e).astype(jnp.int32)
    sorted_e = flat_e[order]
    counts = jnp.bincount(flat_e, length=N_EXPERTS).astype(jnp.int32)
    start = jnp.cumsum(counts) - counts
    padded = (counts + EXPERT_BLOCK - 1) // EXPERT_BLOCK * EXPERT_BLOCK
    pad_end = jnp.cumsum(padded)
    pad_start = pad_end - padded
    dest_sorted = pad_start[sorted_e] + jnp.arange(n_assign, dtype=jnp.int32) - start[sorted_e]
    slot_tok = jnp.zeros((n_slots,), jnp.int32).at[dest_sorted].set(order // TOP_K)
    dest = jnp.zeros((n_assign,), jnp.int32).at[order].set(dest_sorted)
    block_start = jnp.arange(n_slots // EXPERT_BLOCK, dtype=jnp.int32) * EXPERT_BLOCK
    block_expert = jnp.minimum(jnp.searchsorted(pad_end, block_start, side='right'), N_EXPERTS - 1)
    return slot_tok, dest, block_expert.astype(jnp.int32), top_p


def kernel(x, c, ctx, c_ctx, w_mod, b_mod, norm_mix_g, norm_ffn_g, w_in, conv_w, a_log_f, a_log_b, dt_bias_f,
           dt_bias_b, gdn_norm_g, gla_gate_w_f, gla_gate_w_b, gla_gate_b_f, gla_gate_b_b, gla_norm_g, w_out_a,
           w_out_b, w_out, w_router, b_router, w_gu, b_gu, w_down, b_down, final_norm_g):
    assert w_mod.shape[0] == 1, "single-layer kernel"
    b, l, d = x.shape
    n_ctx = ctx.shape[1]
    assert n_ctx == TOK_BLOCK and l % CONV_ROWS == 0 and d == GDN_HEADS * GDN_DIM
    gw = GDN_HEADS * GDN_DIM
    kw = GLA_HEADS * GLA_DK
    vw = GLA_HEADS * GLA_DV

    pad_rows = -(b + 1) % 8
    cc = jnp.concatenate([c, c_ctx[None], jnp.zeros((pad_rows, d), F32)], axis=0)
    mod = _modulation(cc, w_mod[0], b_mod[0])
    sh1, sc1, g1, sh2, sc2, g2 = [mod[:b, i * d:(i + 1) * d].reshape(b, 1, d) for i in range(6)]
    sh1c, sc1c = mod[b:b + 1, 0:d], mod[b:b + 1, d:2 * d]

    offs = {}
    o = 0
    for name, width in (("qa", gw), ("ka", gw), ("va", gw), ("za", gw), ("beta", 2 * GDN_HEADS),
                        ("dec", 2 * GDN_HEADS), ("qb", kw), ("kb", kw), ("vb", vw), ("rb", vw),
                        ("lr", 2 * GLA_RANK), ("gates", 2 * d)):
        offs[name] = (o, o + width)
        o += width
    wi = w_in[0]
    cols = lambda n: wi[:, offs[n][0]:offs[n][1]]
    w_main = jnp.concatenate([cols(n) for n in ("qa", "ka", "va", "za", "qb", "kb", "vb", "rb", "gates")],
                             axis=1).astype(BF16)
    zpad = lambda n: jnp.zeros((d, n), F32)
    w_small = jnp.concatenate([cols("beta"), cols("dec"), zpad(LANE - 4 * GDN_HEADS),
                               cols("lr"), zpad(LANE - 2 * GLA_RANK)], axis=1)
    lane_pad = lambda v, lo: jnp.zeros((1, LANE), F32).at[0, lo:lo + v.shape[0]].set(v)
    coef_a = lane_pad(-jnp.exp(jnp.concatenate([a_log_f[0], a_log_b[0]])), 2 * GDN_HEADS)
    coef_b = lane_pad(jnp.concatenate([dt_bias_f[0], dt_bias_b[0]]), 2 * GDN_HEADS)

    p, gd, lr = _input_projection(x, ctx, sh1, sc1, sh1c, sc1c, norm_mix_g, w_main, w_small, coef_a, coef_b)
    tot = p.shape[1]
    n_chunks = tot // CHUNK
    n_ctx_chunks = n_ctx // CHUNK

    qkv = _conv_qkv(p, conv_w[0].reshape(9, 3 * gw), n_ctx, l)
    gd_rows = gd[:, :, :4 * GDN_HEADS].reshape(b, n_chunks, CHUNK, 4 * GDN_HEADS).transpose(0, 1, 3, 2)
    gd_rows = jnp.pad(gd_rows, ((0, 0), (0, 0), (0, 0), (0, LANE - CHUNK)))
    oaf, oab = _gdn_scan(qkv, gd, _gdn_prep(qkv, gd, gd_rows), n_ctx_chunks)

    embed = lambda w, lo: jnp.zeros((LANE, kw), F32).at[lo:lo + GLA_RANK].set(w)
    obf, obb = _gla_scan(p, lr, embed(gla_gate_w_f[0], 0), embed(gla_gate_w_b[0], GLA_RANK),
                         gla_gate_b_f, gla_gate_b_b, n_ctx_chunks, qk_col=4, v_col=5)

    wr = jnp.zeros((d, LANE), F32).at[:, :N_EXPERTS].set(w_router[0])
    br = jnp.zeros((1, LANE), F32).at[0, :N_EXPERTS].set(b_router[0])
    x1, h2, logits = _merge(oaf, oab, obf, obb, p, x, g1, sh2, sc2,
                            jnp.tile(gdn_norm_g, (1, GDN_HEADS)), jnp.tile(gla_norm_g, (1, GLA_HEADS)), norm_ffn_g,
                            w_out_a[0].astype(BF16), w_out_b[0].astype(BF16), w_out[0].astype(BF16), wr, br,
                            cols=(3, 6, 7, 8))

    return _channel_mixing(x1, h2, logits, g2, w_gu, b_gu, w_down, b_down, final_norm_g)


def _channel_mixing(x1, h2, logits, g2, w_gu, b_gu, w_down, b_down, final_norm_g):
    b, l, d = x1.shape
    t = b * l
    slot_tok, dest, block_expert, top_p = _routing(logits.reshape(t, LANE)[:, :N_EXPERTS])
    yb = _experts(h2, slot_tok, block_expert, w_gu[0].astype(BF16), b_gu[0], w_down[0].astype(BF16), b_down[0])
    return _combine(x1, yb, dest, top_p.reshape(b, l, TOP_K), g2, final_norm_g.reshape(1, d))
```

```python
import functools

import jax
import jax.numpy as jnp
from jax import lax
from jax.experimental import pallas as pl
from jax.experimental.pallas import tpu as pltpu

F32 = jnp.float32
BF16 = jnp.bfloat16
HIGHEST = lax.Precision.HIGHEST

EPS = 1e-6
GRID_W = 64
CHUNK = 64
TOK_BLOCK = 256
CONV_ROWS = 512
GDN_HEADS = 8
GDN_DIM = 128
GLA_HEADS = 4
GLA_DK = 128
GLA_DV = 256
GLA_RANK = 16
GLA_NORMALIZER = 16.0
N_EXPERTS = 32
TOP_K = 4
SWIGLU_LIMIT = 7.0
SWIGLU_ALPHA = 1.702
EXPERT_BLOCK = 512
LANE = 128
SUBLANE = 8
VMEM_LIMIT = 56 * 1024 * 1024


def _silu(t):
    return t * jax.nn.sigmoid(t)


def _softplus(t):
    return jnp.maximum(t, 0.0) + jnp.log1p(jnp.exp(-jnp.abs(t)))


def _dot(a, b, **kw):
    return jnp.dot(a, b, preferred_element_type=F32, **kw)


def _dot_nt(a, b, **kw):
    return lax.dot_general(a, b, (((1,), (1,)), ((), ())), preferred_element_type=F32, **kw)


def _dot_tn(a, b, **kw):
    return lax.dot_general(a, b, (((0,), (0,)), ((), ())), preferred_element_type=F32, **kw)


def _store_tile_rows(ref, val):
    rows, width = val.shape
    groups = width // LANE
    for s in range(groups):
        ref[pl.ds(s, rows, stride=groups), :] = val[:, s * LANE:(s + 1) * LANE]


def _load_tile_rows(ref, rows, groups, first=0, stride=None):
    stride = groups if stride is None else stride
    return jnp.concatenate([ref[pl.ds(first + s, rows, stride=stride), :] for s in range(groups)], axis=-1)


def _mod_kernel(c_ref, w_ref, b_ref, o_ref):
    o_ref[...] = _dot(_silu(c_ref[...]), w_ref[...], precision=HIGHEST) + b_ref[...]


def _modulation(cc, w_mod, b_mod):
    rows, d = cc.shape
    n = w_mod.shape[1]
    tn = n // 4
    return pl.pallas_call(
        _mod_kernel,
        out_shape=jax.ShapeDtypeStruct((rows, n), F32),
        grid=(n // tn,),
        in_specs=[pl.BlockSpec((rows, d), lambda j: (0, 0)),
                  pl.BlockSpec((d, tn), lambda j: (0, j)),
                  pl.BlockSpec((1, tn), lambda j: (0, j))],
        out_specs=pl.BlockSpec((rows, tn), lambda j: (0, j)),
        compiler_params=pltpu.CompilerParams(vmem_limit_bytes=VMEM_LIMIT),
        name="mod",
    )(cc, w_mod, b_mod.reshape(1, n))


def _inproj_kernel(x_ref, ctx_ref, shl_ref, scl_ref, shc_ref, scc_ref, g_ref, w_ref, ws_ref, ca_ref, cb_ref,
                   p_ref, gd_ref, lr_ref):
    is_ctx = pl.program_id(1) == 0
    xin = jnp.where(is_ctx, ctx_ref[0], x_ref[0])
    shift = jnp.where(is_ctx, shc_ref[...], shl_ref[0])
    scale = jnp.where(is_ctx, scc_ref[...], scl_ref[0])
    ms = jnp.mean(xin * xin, axis=-1, keepdims=True)
    h = (xin * lax.rsqrt(ms + EPS) * g_ref[...]) * (1.0 + scale) + shift
    hb = h.astype(BF16)
    n_main = w_ref.shape[1]
    for j in range(n_main // 1024):
        p_ref[0, :, j * 1024:(j + 1) * 1024] = _dot(hb, w_ref[:, j * 1024:(j + 1) * 1024])
    sm = _dot(h, ws_ref[...], precision=HIGHEST)
    raw = sm[:, :LANE]
    lane = lax.broadcasted_iota(jnp.int32, raw.shape, 1)
    beta = jax.nn.sigmoid(raw)
    g = ca_ref[...] * _softplus(raw + cb_ref[...])
    gd_ref[0] = jnp.where(lane < 2 * GDN_HEADS, beta, jnp.where(lane < 4 * GDN_HEADS, g, 0.0))
    lr_ref[0] = sm[:, LANE:]


def _input_projection(x, ctx, sh_lat, sc_lat, sh_ctx, sc_ctx, norm_g, w_main, w_small, coef_a, coef_b):
    b, l, d = x.shape
    n_blk = l // TOK_BLOCK + 1
    n_main = w_main.shape[1]
    tot = n_blk * TOK_BLOCK
    row3 = lambda i, s: (i, 0, 0)
    const2 = lambda i, s: (0, 0)
    blk = lambda i, s: (i, s, 0)
    return pl.pallas_call(
        _inproj_kernel,
        out_shape=(jax.ShapeDtypeStruct((b, tot, n_main), F32),
                   jax.ShapeDtypeStruct((b, tot, LANE), F32),
                   jax.ShapeDtypeStruct((b, tot, LANE), F32)),
        grid=(b, n_blk),
        in_specs=[pl.BlockSpec((1, TOK_BLOCK, d), lambda i, s: (i, jnp.maximum(s - 1, 0), 0)),
                  pl.BlockSpec((1, TOK_BLOCK, d), row3),
                  pl.BlockSpec((1, 1, d), row3), pl.BlockSpec((1, 1, d), row3),
                  pl.BlockSpec((1, d), const2), pl.BlockSpec((1, d), const2),
                  pl.BlockSpec((1, d), const2),
                  pl.BlockSpec((d, n_main), const2, pipeline_mode=pl.Buffered(1)),
                  pl.BlockSpec((d, 2 * LANE), const2),
                  pl.BlockSpec((1, LANE), const2), pl.BlockSpec((1, LANE), const2)],
        out_specs=(pl.BlockSpec((1, TOK_BLOCK, n_main), blk),
                   pl.BlockSpec((1, TOK_BLOCK, LANE), blk),
                   pl.BlockSpec((1, TOK_BLOCK, LANE), blk)),
        compiler_params=pltpu.CompilerParams(dimension_semantics=("arbitrary", "arbitrary"),
                                             vmem_limit_bytes=VMEM_LIMIT),
        name="inproj",
    )(x, ctx, sh_lat, sc_lat, sh_ctx, sc_ctx, norm_g, w_main, w_small, coef_a, coef_b)


def _conv_kernel(p_ref, w_ref, o_ref, pad_ref, *, n_ctx, n_lat, tc, qk_cols):
    is_qk = pl.program_id(1) * tc < qk_cols
    w = w_ref[...]

    def post(y):
        s = _silu(y)
        outs = []
        for hh in range(tc // GDN_DIM):
            sh = s[:, hh * GDN_DIM:(hh + 1) * GDN_DIM]
            fac = lax.rsqrt(jnp.sum(sh * sh, axis=-1, keepdims=True) + EPS)
            outs.append(sh * jnp.where(is_qk, fac, 1.0))
        return outs[0] if len(outs) == 1 else jnp.concatenate(outs, axis=-1)

    xc = p_ref[0, 0:n_ctx, :]
    ridx = lax.broadcasted_iota(jnp.int32, (n_ctx, 1), 0)
    yc = (jnp.where(ridx == 0, 0.0, pltpu.roll(xc * w[3:4], 1, 0)) + xc * w[4:5]
          + jnp.where(ridx == n_ctx - 1, 0.0, pltpu.roll(xc * w[5:6], n_ctx - 1, 0)))
    o_ref[0, 0:n_ctx, :] = post(yc)

    zeros = jnp.zeros((GRID_W, tc), F32)
    pad_ref[0:GRID_W, :] = zeros
    pad_ref[GRID_W + n_lat:GRID_W + n_lat + GRID_W, :] = zeros
    pad_ref[GRID_W:GRID_W + n_lat, :] = p_ref[0, n_ctx:n_ctx + n_lat, :]
    rows = min(CONV_ROWS, n_lat)
    col = lax.broadcasted_iota(jnp.int32, (rows, 1), 0) % GRID_W

    def body(i, carry):
        s = pl.multiple_of(i * rows, rows)
        up = pad_ref[pl.ds(s, rows), :]
        mid = pad_ref[pl.ds(s + GRID_W, rows), :]
        dn = pad_ref[pl.ds(s + 2 * GRID_W, rows), :]
        a0 = up * w[0:1] + mid * w[3:4] + dn * w[6:7]
        a1 = up * w[1:2] + mid * w[4:5] + dn * w[7:8]
        a2 = up * w[2:3] + mid * w[5:6] + dn * w[8:9]
        y = (jnp.where(col == 0, 0.0, pltpu.roll(a0, 1, 0)) + a1
             + jnp.where(col == GRID_W - 1, 0.0, pltpu.roll(a2, rows - 1, 0)))
        o_ref[0, pl.ds(n_ctx + s, rows), :] = post(y)
        return carry

    lax.fori_loop(0, n_lat // rows, body, 0)


def _conv_qkv(p, conv_w9, n_ctx, n_lat):
    b, tot, _ = p.shape
    c = conv_w9.shape[1]
    tc = 2 * GDN_DIM
    kern = functools.partial(_conv_kernel, n_ctx=n_ctx, n_lat=n_lat, tc=tc, qk_cols=2 * GDN_HEADS * GDN_DIM)
    return pl.pallas_call(
        kern,
        out_shape=jax.ShapeDtypeStruct((b, tot, c), F32),
        grid=(b, c // tc),
        in_specs=[pl.BlockSpec((1, tot, tc), lambda i, j: (i, 0, j)),
                  pl.BlockSpec((9, tc), lambda i, j: (0, j))],
        out_specs=pl.BlockSpec((1, tot, tc), lambda i, j: (i, 0, j)),
        scratch_shapes=[pltpu.VMEM((n_lat + 2 * GRID_W, tc), F32)],
        compiler_params=pltpu.CompilerParams(dimension_semantics=("arbitrary", "arbitrary"),
                                             vmem_limit_bytes=VMEM_LIMIT),
        name="conv",
    )(p, conv_w9)


def _bwd_chunk(n, n_ctx_chunks, n_chunks):
    return jnp.where(n < n_ctx_chunks, n_ctx_chunks - 1 - n, n_chunks + n_ctx_chunks - 1 - n)


def _tri_masks():
    r = lax.broadcasted_iota(jnp.int32, (CHUNK, CHUNK), 0)
    c = lax.broadcasted_iota(jnp.int32, (CHUNK, CHUNK), 1)
    return r >= c, r <= c, r == c


PREP_BATCH = 4
PREP_CHUNKS = TOK_BLOCK // CHUNK
SLAB = CHUNK + SUBLANE


def _substitute(n_scr, t_scr, reverse):
    t_scr[...] = jnp.zeros(t_scr.shape, F32)
    n_seg = CHUNK // SUBLANE
    sub = lax.broadcasted_iota(jnp.int32, (SUBLANE, LANE), 0)
    for seg in (reversed(range(n_seg)) if reverse else range(n_seg)):
        js = range(SUBLANE * seg, CHUNK) if reverse else range(0, SUBLANE * (seg + 1))
        acc_groups = range(seg, n_seg) if reverse else range(0, seg + 1)

        def body(step, carry, seg=seg, js=js, acc_groups=acc_groups):
            i = SUBLANE * seg + (SUBLANE - 1 - step if reverse else step)
            base = pl.multiple_of(i * CHUNK, CHUNK)
            acc = {g: jnp.zeros((SUBLANE, LANE), F32) for g in acc_groups}
            for j in js:
                nij = jnp.broadcast_to(n_scr[pl.ds(base + j, 1), :], (SUBLANE, LANE))
                for g in (range(j // SUBLANE, n_seg) if reverse else range(0, j // SUBLANE + 1)):
                    acc[g] = acc[g] + nij * t_scr[j * CHUNK + g * SUBLANE:j * CHUNK + (g + 1) * SUBLANE, :]
            acc[seg] = acc[seg] + jnp.where(sub + SUBLANE * seg == i, 1.0, 0.0)
            for g in acc_groups:
                t_scr[pl.ds(base + g * SUBLANE, SUBLANE), :] = acc[g]
            return carry

        lax.fori_loop(0, SUBLANE, body, 0)


def _gdn_prep_kernel(q_ref, k_ref, gc_ref, gr_ref, ta_ref, g_scr, n_scr, t_scr, r_scr):
    scale = GDN_DIM ** -0.5
    r = lax.broadcasted_iota(jnp.int32, (CHUNK, LANE), 0)
    c = lax.broadcasted_iota(jnp.int32, (CHUNK, LANE), 1)
    incl = (jnp.logical_and(r >= c, c < CHUNK), jnp.logical_and(r <= c, c < CHUNK))
    strict = (jnp.logical_and(r > c, c < CHUNK), jnp.logical_and(r < c, c < CHUNK))
    rs = lax.broadcasted_iota(jnp.int32, (CHUNK, CHUNK), 0)
    cs = lax.broadcasted_iota(jnp.int32, (CHUNK, CHUNK), 1)
    tri_c = ((rs >= cs).astype(F32), (rs <= cs).astype(F32))
    rl = lax.broadcasted_iota(jnp.int32, (LANE, LANE), 0)
    cl = lax.broadcasted_iota(jnp.int32, (LANE, LANE), 1)
    live = jnp.logical_and(rl < CHUNK, cl < CHUNK)
    tri_r = (jnp.logical_and(rl <= cl, live).astype(F32), jnp.logical_and(rl >= cl, live).astype(F32))
    k_zero = jnp.zeros((CHUNK, GDN_DIM), BF16)

    def per_chunk(fn):
        def outer(bb, carry):
            def inner(ch, carry2):
                fn(bb, ch, pl.multiple_of(ch * CHUNK, CHUNK), (bb * PREP_CHUNKS + ch) * GDN_HEADS)
                return carry2
            return lax.fori_loop(0, PREP_CHUNKS, inner, carry)
        lax.fori_loop(0, PREP_BATCH, outer, 0)

    def build(bb, ch, row0, p0):
        gcol = gc_ref[bb, pl.ds(row0, CHUNK), :]
        grow = gr_ref[bb, ch]
        cum_c = [_dot(tri_c[d], gcol, precision=HIGHEST) for d in (0, 1)]
        cum_r = [_dot(grow, tri_r[d], precision=HIGHEST) for d in (0, 1)]
        for h in range(GDN_HEADS):
            sl = slice(h * GDN_DIM, (h + 1) * GDN_DIM)
            kb = k_ref[bb, pl.ds(row0, CHUNK), sl].astype(BF16)
            qk = jnp.concatenate([q_ref[bb, pl.ds(row0, CHUNK), sl].astype(BF16), kb], axis=0)
            gram = _dot_nt(qk, jnp.concatenate([kb, k_zero], axis=0))
            for d in (0, 1):
                bl = GDN_HEADS * d + h
                gl = 2 * GDN_HEADS + bl
                dec = jnp.exp(jnp.where(incl[d], cum_c[d][:, gl:gl + 1] - cum_r[d][gl:gl + 1, :], -jnp.inf))
                nm = jnp.where(strict[d], -(gram[CHUNK:] * gcol[:, bl:bl + 1] * dec), 0.0)
                g_scr[d, pl.ds(pl.multiple_of((p0 + h) * SLAB, SUBLANE), CHUNK), :] = nm
                ta_ref[bb, ch, d, h, CHUNK:2 * CHUNK, :] = (gram[:CHUNK] * scale * dec)[:, :CHUNK].astype(BF16)

    per_chunk(build)

    for d in (0, 1):
        def to_lanes(i, carry, d=d):
            blk = g_scr[d, pl.ds(i, LANE, stride=SLAB), :]
            n_scr[pl.ds(pl.multiple_of(i * CHUNK, CHUNK), CHUNK), :] = blk.T[:CHUNK]
            return carry

        lax.fori_loop(0, CHUNK, to_lanes, 0)
        _substitute(n_scr, t_scr, reverse=(d == 1))

        def from_lanes(i, carry):
            blk = t_scr[pl.ds(pl.multiple_of(i * CHUNK, CHUNK), LANE), :]
            r_scr[pl.ds(i, LANE, stride=SLAB), :] = blk.T
            return carry

        lax.fori_loop(0, CHUNK, from_lanes, 0)

        def finish(bb, ch, row0, p0, d=d):
            grow = gr_ref[bb, ch]
            for h in range(GDN_HEADS):
                bl = GDN_HEADS * d + h
                t = r_scr[pl.ds(pl.multiple_of((p0 + h) * SLAB, SUBLANE), CHUNK), :]
                ta_ref[bb, ch, d, h, 0:CHUNK, :] = (t * grow[bl:bl + 1, :])[:, :CHUNK].astype(BF16)

        per_chunk(finish)


def _gdn_prep(qkv, gd, gd_rows):
    b, tot, _ = qkv.shape
    n_chunks = tot // CHUNK
    w = GDN_HEADS * GDN_DIM
    assert PREP_BATCH * PREP_CHUNKS * GDN_HEADS == LANE and b % PREP_BATCH == 0
    return pl.pallas_call(
        _gdn_prep_kernel,
        out_shape=jax.ShapeDtypeStruct((b, n_chunks, 2, GDN_HEADS, 2 * CHUNK, CHUNK), BF16),
        grid=(b // PREP_BATCH, tot // TOK_BLOCK),
        in_specs=[pl.BlockSpec((PREP_BATCH, TOK_BLOCK, w), lambda i, s: (i, s, 0)),
                  pl.BlockSpec((PREP_BATCH, TOK_BLOCK, w), lambda i, s: (i, s, 1)),
                  pl.BlockSpec((PREP_BATCH, TOK_BLOCK, LANE), lambda i, s: (i, s, 0)),
                  pl.BlockSpec((PREP_BATCH, PREP_CHUNKS, 4 * GDN_HEADS, LANE), lambda i, s: (i, s, 0, 0))],
        out_specs=pl.BlockSpec((PREP_BATCH, PREP_CHUNKS, 2, GDN_HEADS, 2 * CHUNK, CHUNK),
                               lambda i, s: (i, s, 0, 0, 0, 0)),
        scratch_shapes=[pltpu.VMEM((2, LANE * SLAB, LANE), F32),
                        pltpu.VMEM((CHUNK * CHUNK, LANE), F32),
                        pltpu.VMEM((CHUNK * CHUNK + CHUNK, LANE), F32),
                        pltpu.VMEM((LANE * SLAB, LANE), F32)],
        compiler_params=pltpu.CompilerParams(dimension_semantics=("arbitrary", "arbitrary"),
                                             vmem_limit_bytes=VMEM_LIMIT),
        name="gdn_prep",
    )(qkv, qkv, gd, gd_rows)


def _gdn_kernel(qf, kf, vf, qb, kb, vb, gcf, gcb, taf, tab, of, ob, s_ref):
    @pl.when(pl.program_id(1) == 0)
    def _():
        s_ref[...] = jnp.zeros(s_ref.shape, F32)

    scale = GDN_DIM ** -0.5
    lower, upper, _ = _tri_masks()
    dirs = ((qf, kf, vf, gcf, taf, of, lower, CHUNK - 1), (qb, kb, vb, gcb, tab, ob, upper, 0))
    probs = []
    for d, (q_ref, k_ref, v_ref, gc_ref, ta_ref, o_ref, incl, last) in enumerate(dirs):
        cum_c = _dot(incl.astype(F32), gc_ref[0], precision=HIGHEST)
        for h in range(GDN_HEADS):
            bl = GDN_HEADS * d + h
            gl = 2 * GDN_HEADS + bl
            sl = slice(h * GDN_DIM, (h + 1) * GDN_DIM)
            gc = cum_c[:, gl:gl + 1]
            probs.append(dict(bl=bl, sl=sl, gc=gc, g_tot=gc[last:last + 1, :], h=h,
                              q_ref=q_ref, k_ref=k_ref, v_ref=v_ref, ta_ref=ta_ref, o_ref=o_ref))
    for p in probs:
        p["k"] = p["k_ref"][0, :, p["sl"]]
        qk = jnp.concatenate([p["q_ref"][0, :, p["sl"]], p["k"]], axis=0).astype(BF16)
        p["s_old"] = s_ref[p["bl"]]
        p["qks"] = _dot(qk, p["s_old"].astype(BF16))
    for p in probs:
        p["eg"] = jnp.exp(p["gc"])
        x = p["v_ref"][0, :, p["sl"]] - p["eg"] * p["qks"][CHUNK:]
        p["ta"] = p["ta_ref"][0, 0, 0, p["h"]]
        p["v_new"] = _dot(p["ta"][:CHUNK], x.astype(BF16)).astype(BF16)
    for p in probs:
        kd = p["k"] * jnp.exp(p["g_tot"] - p["gc"])
        s_ref[p["bl"]] = p["s_old"] * jnp.exp(p["g_tot"]) + _dot_tn(kd.astype(BF16), p["v_new"])
    for p in probs:
        p["o_ref"][0, :, p["sl"]] = ((scale * p["eg"]) * p["qks"][:CHUNK] + _dot(p["ta"][CHUNK:], p["v_new"]))


def _gdn_scan(qkv, gd, ta, n_ctx_chunks):
    b, tot, _ = qkv.shape
    n_chunks = tot // CHUNK
    w = GDN_HEADS * GDN_DIM
    bwd = functools.partial(_bwd_chunk, n_ctx_chunks=n_ctx_chunks, n_chunks=n_chunks)

    def tok(col, rev):
        if rev:
            return pl.BlockSpec((1, CHUNK, w), lambda i, n: (i, bwd(n), col))
        return pl.BlockSpec((1, CHUNK, w), lambda i, n: (i, n, col))

    gcol_f = pl.BlockSpec((1, CHUNK, LANE), lambda i, n: (i, n, 0))
    gcol_b = pl.BlockSpec((1, CHUNK, LANE), lambda i, n: (i, bwd(n), 0))
    ta_blk = (1, 1, 1, GDN_HEADS, 2 * CHUNK, CHUNK)
    ta_f = pl.BlockSpec(ta_blk, lambda i, n: (i, n, 0, 0, 0, 0))
    ta_b = pl.BlockSpec(ta_blk, lambda i, n: (i, bwd(n), 1, 0, 0, 0))
    out = jax.ShapeDtypeStruct((b, tot, w), F32)
    return pl.pallas_call(
        _gdn_kernel,
        out_shape=(out, out),
        grid=(b, n_chunks),
        in_specs=[tok(0, False), tok(1, False), tok(2, False), tok(0, True), tok(1, True), tok(2, True),
                  gcol_f, gcol_b, ta_f, ta_b],
        out_specs=(tok(0, False), tok(0, True)),
        scratch_shapes=[pltpu.VMEM((2 * GDN_HEADS, GDN_DIM, GDN_DIM), F32)],
        compiler_params=pltpu.CompilerParams(dimension_semantics=("arbitrary", "arbitrary"),
                                             vmem_limit_bytes=VMEM_LIMIT),
        name="gdn",
    )(qkv, qkv, qkv, qkv, qkv, qkv, gd, gd, ta, ta)


def _gla_kernel(qkf, vf, qkb, vb, lrf, lrb, wf, wb, bf, bb, of, ob, s_ref):
    @pl.when(pl.program_id(1) == 0)
    def _():
        s_ref[...] = jnp.zeros(s_ref.shape, F32)

    scale = GLA_DK ** -0.5
    kw = GLA_HEADS * GLA_DK
    lower, upper, _ = _tri_masks()
    dirs = ((qkf, vf, lrf, wf, bf, of, lower, CHUNK // 2, CHUNK - 1),
            (qkb, vb, lrb, wb, bb, ob, upper, CHUNK - 1 - CHUNK // 2, 0))
    probs = []
    for d, (qk_ref, v_ref, lr_ref, w_ref, b_ref, o_ref, incl, centre, last) in enumerate(dirs):
        z = _dot(lr_ref[0], w_ref[...], precision=HIGHEST) + b_ref[...]
        log_a = -_softplus(-z) * (1.0 / GLA_NORMALIZER)
        cum = _dot(incl.astype(F32), log_a, precision=HIGHEST)
        for h in range(GLA_HEADS):
            gc = cum[:, h * GLA_DK:(h + 1) * GLA_DK]
            probs.append(dict(idx=GLA_HEADS * d + h, gc=gc, incl=incl, o_ref=o_ref,
                              g_mid=gc[centre:centre + 1, :], g_tot=gc[last:last + 1, :],
                              q=qk_ref[0, :, h * GLA_DK:(h + 1) * GLA_DK] * scale,
                              k=qk_ref[0, :, kw + h * GLA_DK:kw + (h + 1) * GLA_DK],
                              v=v_ref[0, :, h * GLA_DV:(h + 1) * GLA_DV].astype(BF16),
                              vsl=slice(h * GLA_DV, (h + 1) * GLA_DV)))
    for p in probs:
        attn = _dot_nt((p["q"] * jnp.exp(p["gc"] - p["g_mid"])).astype(BF16),
                       (p["k"] * jnp.exp(p["g_mid"] - p["gc"])).astype(BF16))
        p["attn"] = jnp.where(p["incl"], attn, 0.0).astype(BF16)
        p["s_old"] = s_ref[p["idx"]]
    for p in probs:
        p["o_ref"][0, :, p["vsl"]] = (_dot(p["attn"], p["v"])
                                      + _dot_nt((p["q"] * jnp.exp(p["gc"])).astype(BF16), p["s_old"].astype(BF16)))
    for p in probs:
        kd = p["k"] * jnp.exp(p["g_tot"] - p["gc"])
        s_ref[p["idx"]] = p["s_old"] * jnp.exp(p["g_tot"]) + _dot_tn(p["v"], kd.astype(BF16))


def _gla_scan(p, lr, wf, wb, bf, bb, n_ctx_chunks, qk_col, v_col):
    b, tot, _ = p.shape
    n_chunks = tot // CHUNK
    w = GLA_HEADS * GLA_DV
    bwd = functools.partial(_bwd_chunk, n_ctx_chunks=n_ctx_chunks, n_chunks=n_chunks)

    def tok(width, col, rev):
        if rev:
            return pl.BlockSpec((1, CHUNK, width), lambda i, n: (i, bwd(n), col))
        return pl.BlockSpec((1, CHUNK, width), lambda i, n: (i, n, col))

    const2 = lambda i, n: (0, 0)
    out = jax.ShapeDtypeStruct((b, tot, w), F32)
    return pl.pallas_call(
        _gla_kernel,
        out_shape=(out, out),
        grid=(b, n_chunks),
        in_specs=[tok(w, qk_col, False), tok(w, v_col, False), tok(w, qk_col, True), tok(w, v_col, True),
                  tok(LANE, 0, False), tok(LANE, 0, True),
                  pl.BlockSpec(wf.shape, const2), pl.BlockSpec(wb.shape, const2),
                  pl.BlockSpec(bf.shape, const2), pl.BlockSpec(bb.shape, const2)],
        out_specs=(tok(w, 0, False), tok(w, 0, True)),
        scratch_shapes=[pltpu.VMEM((2 * GLA_HEADS, GLA_DV, GLA_DK), F32)],
        compiler_params=pltpu.CompilerParams(dimension_semantics=("arbitrary", "arbitrary"),
                                             vmem_limit_bytes=VMEM_LIMIT),
        name="gla",
    )(p, p, p, p, lr, lr, wf, wb, bf, bb)


def _head_rms(o, g, width):
    outs = []
    for h in range(o.shape[-1] // width):
        oh = o[:, h * width:(h + 1) * width]
        ms = jnp.mean(oh * oh, axis=-1, keepdims=True)
        outs.append(oh * lax.rsqrt(ms + EPS))
    return jnp.concatenate(outs, axis=-1) * g


def _merge_kernel(oaf, oab, obf, obb, z_ref, r_ref, ga_ref, gb_ref, x_ref, g1_ref, sh2_ref, sc2_ref,
                  na_ref, nb_ref, nf_ref, woa_ref, wob_ref, wo_ref, wr_ref, br_ref,
                  x1_ref, h2_ref, lg_ref):
    ya_in = _head_rms(oaf[0] + oab[0], na_ref[...], GDN_DIM) * _silu(z_ref[0])
    ya = _dot(ya_in.astype(BF16), woa_ref[...])
    yb_in = _head_rms(obf[0] + obb[0], nb_ref[...], GLA_DV) * _silu(r_ref[0])
    yb = _dot(yb_in.astype(BF16), wob_ref[...])
    y = jax.nn.sigmoid(ga_ref[0]) * ya + jax.nn.sigmoid(gb_ref[0]) * yb
    x1 = x_ref[0] + g1_ref[0] * _dot(y.astype(BF16), wo_ref[...])
    x1_ref[0] = x1
    ms = jnp.mean(x1 * x1, axis=-1, keepdims=True)
    h2 = (x1 * lax.rsqrt(ms + EPS) * nf_ref[...]) * (1.0 + sc2_ref[0]) + sh2_ref[0]
    _store_tile_rows(h2_ref, h2)
    lg_ref[0] = _dot(h2, wr_ref[...], precision=HIGHEST) + br_ref[...]


def _merge(oaf, oab, obf, obb, p, x, g1, sh2, sc2, na, nb, nf, woa, wob, wo, wr, br, cols):
    b, l, d = x.shape
    z_col, r_col, ga_col, gb_col = cols
    lat = lambda i, s: (i, s + 1, 0)
    row3 = lambda i, s: (i, 0, 0)
    const2 = lambda i, s: (0, 0)
    blk = lambda i, s: (i, s, 0)
    tb = (1, TOK_BLOCK, d)
    pcol = lambda c: pl.BlockSpec(tb, lambda i, s: (i, s + 1, c))
    vec = pl.BlockSpec((1, 1, d), row3)
    cvec = pl.BlockSpec((1, d), const2)
    wspec = pl.BlockSpec((d, d), const2)
    groups = d // LANE
    n_blk = l // TOK_BLOCK
    return pl.pallas_call(
        _merge_kernel,
        out_shape=(jax.ShapeDtypeStruct((b, l, d), F32),
                   jax.ShapeDtypeStruct((b * l * groups, LANE), F32),
                   jax.ShapeDtypeStruct((b, l, LANE), F32)),
        grid=(b, n_blk),
        in_specs=[pl.BlockSpec(tb, lat)] * 4 + [pcol(z_col), pcol(r_col), pcol(ga_col), pcol(gb_col),
                  pl.BlockSpec(tb, blk), vec, vec, vec, cvec, cvec, cvec, wspec, wspec, wspec,
                  pl.BlockSpec((d, LANE), const2), pl.BlockSpec((1, LANE), const2)],
        out_specs=(pl.BlockSpec(tb, blk),
                   pl.BlockSpec((TOK_BLOCK * groups, LANE), lambda i, s: (i * n_blk + s, 0)),
                   pl.BlockSpec((1, TOK_BLOCK, LANE), blk)),
        compiler_params=pltpu.CompilerParams(dimension_semantics=("arbitrary", "arbitrary"),
                                             vmem_limit_bytes=VMEM_LIMIT),
        name="merge",
    )(oaf, oab, obf, obb, p, p, p, p, x, g1, sh2, sc2, na, nb, nf, woa, wob, wo, wr, br)


def _row_copy(src_ref, dst_ref, sem, src_row, dst_row):
    src = pl.multiple_of(src_row * SUBLANE, SUBLANE)
    dst = pl.multiple_of(dst_row * SUBLANE, SUBLANE)
    return pltpu.make_async_copy(src_ref.at[pl.ds(src, SUBLANE)], dst_ref.at[pl.ds(dst, SUBLANE)], sem)


def _gather_step(step, n_steps, rows, idx_hbm, src_hbm, idx_smem, buf, idx_sem, row_sem):
    def idx_copy(s):
        return pltpu.make_async_copy(idx_hbm.at[s], idx_smem.at[s % 2], idx_sem.at[s % 2])

    def issue(s):
        for slot in (0, 1):
            @pl.when(s % 2 == slot)
            def _(slot=slot):
                def body(r, carry):
                    _row_copy(src_hbm, buf.at[slot], row_sem.at[slot], idx_smem[slot, r], r).start()
                    return carry

                lax.fori_loop(0, rows, body, 0, unroll=8)

    @pl.when(step == 0)
    def _():
        idx_copy(0).start()
        idx_copy(0).wait()
        issue(0)

        @pl.when(n_steps > 1)
        def _():
            idx_copy(1).start()

    @pl.when(step + 1 < n_steps)
    def _():
        idx_copy(step + 1).wait()
        issue(step + 1)

        @pl.when(step + 2 < n_steps)
        def _():
            idx_copy(step + 2).start()

    slot = step % 2
    pltpu.make_async_copy(src_hbm.at[pl.ds(0, rows * SUBLANE)], buf.at[slot], row_sem.at[slot]).wait()
    return buf.at[slot]


def _gather_scratch(rows):
    return [pltpu.SMEM((2, rows), jnp.int32), pltpu.VMEM((2, rows * SUBLANE, LANE), F32),
            pltpu.SemaphoreType.DMA((2,)), pltpu.SemaphoreType.DMA((2,))]


def _expert_kernel(be_ref, tok_ref, h_ref, wgu_ref, bgu_ref, wd_ref, bd_ref, y_ref, *gather_scratch):
    del be_ref
    de = wd_ref.shape[1]
    groups = wgu_ref.shape[1] // LANE
    x_ref = _gather_step(pl.program_id(0), pl.num_programs(0), EXPERT_BLOCK, tok_ref, h_ref, *gather_scratch)
    x = _load_tile_rows(x_ref, EXPERT_BLOCK, groups)
    gu = _dot(x.astype(BF16), wgu_ref[0]) + bgu_ref[0]
    glu = jnp.minimum(gu[:, :de], SWIGLU_LIMIT)
    lin = jnp.clip(gu[:, de:], -SWIGLU_LIMIT, SWIGLU_LIMIT)
    act = glu * jax.nn.sigmoid(SWIGLU_ALPHA * glu) * (lin + 1.0)
    _store_tile_rows(y_ref, _dot(act.astype(BF16), wd_ref[0]) + bd_ref[0])


def _experts(h, slot_tok, block_expert, w_gu, b_gu, w_down, b_down):
    ne, d, de2 = w_gu.shape
    de = de2 // 2
    groups = d // LANE
    assert groups == SUBLANE
    n_slots = slot_tok.shape[0]
    blk_rows = EXPERT_BLOCK * groups
    grid_spec = pltpu.PrefetchScalarGridSpec(
        num_scalar_prefetch=1,
        grid=(n_slots // EXPERT_BLOCK,),
        in_specs=[pl.BlockSpec(memory_space=pl.ANY), pl.BlockSpec(memory_space=pl.ANY),
                  pl.BlockSpec((1, d, de2), lambda i, be: (be[i], 0, 0)),
                  pl.BlockSpec((1, 1, de2), lambda i, be: (be[i], 0, 0)),
                  pl.BlockSpec((1, de, d), lambda i, be: (be[i], 0, 0)),
                  pl.BlockSpec((1, 1, d), lambda i, be: (be[i], 0, 0))],
        out_specs=pl.BlockSpec((blk_rows, LANE), lambda i, be: (i, 0)),
        scratch_shapes=_gather_scratch(EXPERT_BLOCK),
    )
    return pl.pallas_call(
        _expert_kernel,
        out_shape=jax.ShapeDtypeStruct((n_slots * groups, LANE), F32),
        grid_spec=grid_spec,
        compiler_params=pltpu.CompilerParams(dimension_semantics=("arbitrary",), vmem_limit_bytes=VMEM_LIMIT),
        name="experts",
    )(block_expert, slot_tok.reshape(n_slots // EXPERT_BLOCK, EXPERT_BLOCK), h,
      w_gu, b_gu.reshape(ne, 1, de2), w_down, b_down.reshape(ne, 1, d))


def _combine_kernel(dest_ref, yb_ref, x1_ref, p_ref, g2_ref, fg_ref, o_ref, *gather_scratch):
    groups = x1_ref.shape[-1] // LANE
    step = pl.program_id(0) * pl.num_programs(1) + pl.program_id(1)
    n_steps = pl.num_programs(0) * pl.num_programs(1)
    y4_ref = _gather_step(step, n_steps, TOK_BLOCK * TOP_K, dest_ref, yb_ref, *gather_scratch)
    p = p_ref[0]
    y = None
    for k in range(TOP_K):
        yk = p[:, k:k + 1] * _load_tile_rows(y4_ref, TOK_BLOCK, groups, first=k * groups, stride=TOP_K * groups)
        y = yk if y is None else y + yk
    x2 = x1_ref[0] + g2_ref[0] * y
    ms = jnp.mean(x2 * x2, axis=-1, keepdims=True)
    o_ref[0] = x2 * lax.rsqrt(ms + EPS) * fg_ref[...]


def _combine(x1, yb, dest, top_p, g2, final_g):
    b, l, d = x1.shape
    assert d // LANE == SUBLANE
    blk = lambda i, s: (i, s, 0)
    n_blk = l // TOK_BLOCK
    rows = TOK_BLOCK * TOP_K
    return pl.pallas_call(
        _combine_kernel,
        out_shape=jax.ShapeDtypeStruct((b, l, d), F32),
        grid=(b, n_blk),
        in_specs=[pl.BlockSpec(memory_space=pl.ANY), pl.BlockSpec(memory_space=pl.ANY),
                  pl.BlockSpec((1, TOK_BLOCK, d), blk),
                  pl.BlockSpec((1, TOK_BLOCK, TOP_K), blk),
                  pl.BlockSpec((1, 1, d), lambda i, s: (i, 0, 0)),
                  pl.BlockSpec((1, d), lambda i, s: (0, 0))],
        out_specs=pl.BlockSpec((1, TOK_BLOCK, d), blk),
        scratch_shapes=_gather_scratch(rows),
        compiler_params=pltpu.CompilerParams(dimension_semantics=("arbitrary", "arbitrary"),
                                             vmem_limit_bytes=VMEM_LIMIT),
        name="combine",
    )(dest.reshape(b * n_blk, rows), yb, x1, top_p, g2, final_g)


def _routing(logits):
    t = logits.shape[0]
    n_assign = t * TOP_K
    n_slots = -(-(n_assign + N_EXPERTS * (EXPERT_BLOCK - 1)) // EXPERT_BLOCK) * EXPERT_BLOCK
    top_logit, top_idx = lax.top_k(logits, TOP_K)
    top_p = jax.nn.softmax(top_logit, axis=-1)
    flat_e = top_idx.reshape(-1).astype(jnp.int32)
    experts = jnp.arange(N_EXPERTS, dtype=jnp.int32)
    lookup = lambda table, e: jnp.sum(jnp.where(e[:, None] == experts[None, :], table[None, :], 0), axis=1)
    pos = jnp.arange(n_assign, dtype=jnp.int32)
    sorted_e, order = lax.sort((flat_e, pos), num_keys=1)
    counts = jnp.sum((flat_e[:, None] == experts[None, :]).astype(jnp.int32), axis=0)
    start = jnp.cumsum(counts) - counts
    padded = (counts + EXPERT_BLOCK - 1) // EXPERT_BLOCK * EXPERT_BLOCK
    pad_end = jnp.cumsum(padded)
    pad_start = pad_end - padded
    dest_sorted = pos + lookup(pad_start - start, sorted_e)
    _, dest = lax.sort((order, dest_sorted), num_keys=1)
    block_start = jnp.arange(n_slots // EXPERT_BLOCK, dtype=jnp.int32) * EXPERT_BLOCK
    block_expert = jnp.minimum(jnp.sum((pad_end[None, :] <= block_start[:, None]).astype(jnp.int32), axis=1),
                               N_EXPERTS - 1)
    slot_e = jnp.repeat(block_expert, EXPERT_BLOCK)
    within = jnp.arange(n_slots, dtype=jnp.int32) - lookup(pad_start, slot_e)
    src = jnp.clip(lookup(start, slot_e) + within, 0, n_assign - 1)
    slot_tok = jnp.where(within < lookup(counts, slot_e), order[src] // TOP_K, 0)
    return slot_tok, dest, block_expert, top_p


def kernel(x, c, ctx, c_ctx, w_mod, b_mod, norm_mix_g, norm_ffn_g, w_in, conv_w, a_log_f, a_log_b, dt_bias_f,
           dt_bias_b, gdn_norm_g, gla_gate_w_f, gla_gate_w_b, gla_gate_b_f, gla_gate_b_b, gla_norm_g, w_out_a,
           w_out_b, w_out, w_router, b_router, w_gu, b_gu, w_down, b_down, final_norm_g):
    assert w_mod.shape[0] == 1, "single-layer kernel"
    b, l, d = x.shape
    n_ctx = ctx.shape[1]
    assert n_ctx == TOK_BLOCK and l % CONV_ROWS == 0 and d == GDN_HEADS * GDN_DIM
    gw = GDN_HEADS * GDN_DIM
    kw = GLA_HEADS * GLA_DK
    vw = GLA_HEADS * GLA_DV

    pad_rows = -(b + 1) % 8
    cc = jnp.concatenate([c, c_ctx[None], jnp.zeros((pad_rows, d), F32)], axis=0)
    mod = _modulation(cc, w_mod[0], b_mod[0])
    sh1, sc1, g1, sh2, sc2, g2 = [mod[:b, i * d:(i + 1) * d].reshape(b, 1, d) for i in range(6)]
    sh1c, sc1c = mod[b:b + 1, 0:d], mod[b:b + 1, d:2 * d]

    offs = {}
    o = 0
    for name, width in (("qa", gw), ("ka", gw), ("va", gw), ("za", gw), ("beta", 2 * GDN_HEADS),
                        ("dec", 2 * GDN_HEADS), ("qb", kw), ("kb", kw), ("vb", vw), ("rb", vw),
                        ("lr", 2 * GLA_RANK), ("gates", 2 * d)):
        offs[name] = (o, o + width)
        o += width
    wi = w_in[0]
    cols = lambda n: wi[:, offs[n][0]:offs[n][1]]
    w_main = jnp.concatenate([cols(n) for n in ("qa", "ka", "va", "za", "qb", "kb", "vb", "rb", "gates")],
                             axis=1).astype(BF16)
    zpad = lambda n: jnp.zeros((d, n), F32)
    w_small = jnp.concatenate([cols("beta"), cols("dec"), zpad(LANE - 4 * GDN_HEADS),
                               cols("lr"), zpad(LANE - 2 * GLA_RANK)], axis=1)
    lane_pad = lambda v, lo: jnp.zeros((1, LANE), F32).at[0, lo:lo + v.shape[0]].set(v)
    coef_a = lane_pad(-jnp.exp(jnp.concatenate([a_log_f[0], a_log_b[0]])), 2 * GDN_HEADS)
    coef_b = lane_pad(jnp.concatenate([dt_bias_f[0], dt_bias_b[0]]), 2 * GDN_HEADS)

    p, gd, lr = _input_projection(x, ctx, sh1, sc1, sh1c, sc1c, norm_mix_g, w_main, w_small, coef_a, coef_b)
    tot = p.shape[1]
    n_chunks = tot // CHUNK
    n_ctx_chunks = n_ctx // CHUNK

    qkv = _conv_qkv(p, conv_w[0].reshape(9, 3 * gw), n_ctx, l)
    gd_rows = gd[:, :, :4 * GDN_HEADS].reshape(b, n_chunks, CHUNK, 4 * GDN_HEADS).transpose(0, 1, 3, 2)
    gd_rows = jnp.pad(gd_rows, ((0, 0), (0, 0), (0, 0), (0, LANE - CHUNK)))
    oaf, oab = _gdn_scan(qkv, gd, _gdn_prep(qkv, gd, gd_rows), n_ctx_chunks)

    embed = lambda w, lo: jnp.zeros((LANE, kw), F32).at[lo:lo + GLA_RANK].set(w)
    obf, obb = _gla_scan(p, lr, embed(gla_gate_w_f[0], 0), embed(gla_gate_w_b[0], GLA_RANK),
                         gla_gate_b_f, gla_gate_b_b, n_ctx_chunks, qk_col=4, v_col=5)

    wr = jnp.zeros((d, LANE), F32).at[:, :N_EXPERTS].set(w_router[0])
    br = jnp.zeros((1, LANE), F32).at[0, :N_EXPERTS].set(b_router[0])
    x1, h2, logits = _merge(oaf, oab, obf, obb, p, x, g1, sh2, sc2,
                            jnp.tile(gdn_norm_g, (1, GDN_HEADS)), jnp.tile(gla_norm_g, (1, GLA_HEADS)), norm_ffn_g,
                            w_out_a[0].astype(BF16), w_out_b[0].astype(BF16), w_out[0].astype(BF16), wr, br,
                            cols=(3, 6, 7, 8))

    return _channel_mixing(x1, h2, logits, g2, w_gu, b_gu, w_down, b_down, final_norm_g)


def _channel_mixing(x1, h2, logits, g2, w_gu, b_gu, w_down, b_down, final_norm_g):
    b, l, d = x1.shape
    t = b * l
    slot_tok, dest, block_expert, top_p = _routing(logits.reshape(t, LANE)[:, :N_EXPERTS])
    yb = _experts(h2, slot_tok, block_expert, w_gu[0].astype(BF16), b_gu[0], w_down[0].astype(BF16), b_down[0])
    return _combine(x1, yb, dest, top_p.reshape(b, l, TOP_K), g2, final_norm_g.reshape(1, d))
```

```python
import functools

import jax
import jax.numpy as jnp
from jax import lax
from jax.experimental import pallas as pl
from jax.experimental.pallas import tpu as pltpu

F32 = jnp.float32
BF16 = jnp.bfloat16
HIGHEST = lax.Precision.HIGHEST

EPS = 1e-6
GRID_W = 64
CHUNK = 64
TOK_BLOCK = 256
CONV_ROWS = 512
GDN_HEADS = 8
GDN_DIM = 128
GLA_HEADS = 4
GLA_DK = 128
GLA_DV = 256
GLA_RANK = 16
GLA_NORMALIZER = 16.0
N_EXPERTS = 32
TOP_K = 4
SWIGLU_LIMIT = 7.0
SWIGLU_ALPHA = 1.702
EXPERT_BLOCK = 512
LANE = 128
SUBLANE = 8
VMEM_LIMIT = 56 * 1024 * 1024


def _silu(t):
    return t * jax.nn.sigmoid(t)


def _softplus(t):
    return jnp.maximum(t, 0.0) + jnp.log1p(jnp.exp(-jnp.abs(t)))


def _dot(a, b, **kw):
    return jnp.dot(a, b, preferred_element_type=F32, **kw)


def _dot_nt(a, b, **kw):
    return lax.dot_general(a, b, (((1,), (1,)), ((), ())), preferred_element_type=F32, **kw)


def _dot_tn(a, b, **kw):
    return lax.dot_general(a, b, (((0,), (0,)), ((), ())), preferred_element_type=F32, **kw)


def _store_tile_rows(ref, val):
    rows, width = val.shape
    groups = width // LANE
    for s in range(groups):
        ref[pl.ds(s, rows, stride=groups), :] = val[:, s * LANE:(s + 1) * LANE]


def _load_tile_rows(ref, rows, groups, first=0, stride=None):
    stride = groups if stride is None else stride
    return jnp.concatenate([ref[pl.ds(first + s, rows, stride=stride), :] for s in range(groups)], axis=-1)


def _mod_kernel(c_ref, w_ref, b_ref, o_ref):
    o_ref[...] = _dot(_silu(c_ref[...]), w_ref[...], precision=HIGHEST) + b_ref[...]


def _modulation(cc, w_mod, b_mod):
    rows, d = cc.shape
    n = w_mod.shape[1]
    tn = n // 4
    return pl.pallas_call(
        _mod_kernel,
        out_shape=jax.ShapeDtypeStruct((rows, n), F32),
        grid=(n // tn,),
        in_specs=[pl.BlockSpec((rows, d), lambda j: (0, 0)),
                  pl.BlockSpec((d, tn), lambda j: (0, j)),
                  pl.BlockSpec((1, tn), lambda j: (0, j))],
        out_specs=pl.BlockSpec((rows, tn), lambda j: (0, j)),
        compiler_params=pltpu.CompilerParams(vmem_limit_bytes=VMEM_LIMIT),
        name="mod",
    )(cc, w_mod, b_mod.reshape(1, n))


def _inproj_kernel(x_ref, ctx_ref, shl_ref, scl_ref, shc_ref, scc_ref, g_ref, w_ref, ws_ref, ca_ref, cb_ref,
                   p_ref, gd_ref):
    is_ctx = pl.program_id(1) == 0
    xin = jnp.where(is_ctx, ctx_ref[0], x_ref[0])
    shift = jnp.where(is_ctx, shc_ref[...], shl_ref[0])
    scale = jnp.where(is_ctx, scc_ref[...], scl_ref[0])
    ms = jnp.mean(xin * xin, axis=-1, keepdims=True)
    h = (xin * lax.rsqrt(ms + EPS) * g_ref[...]) * (1.0 + scale) + shift
    hb = h.astype(BF16)
    n_main = w_ref.shape[1]
    for j in range(n_main // 1024):
        p_ref[0, :, j * 1024:(j + 1) * 1024] = _dot(hb, w_ref[:, j * 1024:(j + 1) * 1024])
    h_lo = (h - hb.astype(F32)).astype(BF16)
    raw = _dot(hb, ws_ref[0]) + (_dot(h_lo, ws_ref[0]) + _dot(hb, ws_ref[1]))
    lane = lax.broadcasted_iota(jnp.int32, raw.shape, 1)
    beta = jax.nn.sigmoid(raw)
    g = ca_ref[...] * _softplus(raw + cb_ref[...])
    gd_ref[0] = jnp.where(lane < 2 * GDN_HEADS, beta, jnp.where(lane < 4 * GDN_HEADS, g, raw))


def _input_projection(x, ctx, sh_lat, sc_lat, sh_ctx, sc_ctx, norm_g, w_main, w_small, coef_a, coef_b):
    b, l, d = x.shape
    n_blk = l // TOK_BLOCK + 1
    n_main = w_main.shape[1]
    tot = n_blk * TOK_BLOCK
    row3 = lambda i, s: (i, 0, 0)
    const2 = lambda i, s: (0, 0)
    blk = lambda i, s: (i, s, 0)
    return pl.pallas_call(
        _inproj_kernel,
        out_shape=(jax.ShapeDtypeStruct((b, tot, n_main), F32),
                   jax.ShapeDtypeStruct((b, tot, LANE), F32)),
        grid=(b, n_blk),
        in_specs=[pl.BlockSpec((1, TOK_BLOCK, d), lambda i, s: (i, jnp.maximum(s - 1, 0), 0)),
                  pl.BlockSpec((1, TOK_BLOCK, d), row3),
                  pl.BlockSpec((1, 1, d), row3), pl.BlockSpec((1, 1, d), row3),
                  pl.BlockSpec((1, d), const2), pl.BlockSpec((1, d), const2),
                  pl.BlockSpec((1, d), const2),
                  pl.BlockSpec((d, n_main), const2, pipeline_mode=pl.Buffered(1)),
                  pl.BlockSpec((2, d, LANE), lambda i, s: (0, 0, 0)),
                  pl.BlockSpec((1, LANE), const2), pl.BlockSpec((1, LANE), const2)],
        out_specs=(pl.BlockSpec((1, TOK_BLOCK, n_main), blk),
                   pl.BlockSpec((1, TOK_BLOCK, LANE), blk)),
        compiler_params=pltpu.CompilerParams(dimension_semantics=("arbitrary", "arbitrary"),
                                             vmem_limit_bytes=VMEM_LIMIT),
        name="inproj",
    )(x, ctx, sh_lat, sc_lat, sh_ctx, sc_ctx, norm_g, w_main, w_small, coef_a, coef_b)


def _conv_kernel(p_ref, w_ref, o_ref, pad_ref, *, n_ctx, n_lat, tc, qk_cols):
    is_qk = pl.program_id(1) * tc < qk_cols
    w = w_ref[...]

    def post(y):
        s = _silu(y)
        outs = []
        for hh in range(tc // GDN_DIM):
            sh = s[:, hh * GDN_DIM:(hh + 1) * GDN_DIM]
            fac = lax.rsqrt(jnp.sum(sh * sh, axis=-1, keepdims=True) + EPS)
            outs.append(sh * jnp.where(is_qk, fac, 1.0))
        return outs[0] if len(outs) == 1 else jnp.concatenate(outs, axis=-1)

    xc = p_ref[0, 0:n_ctx, :]
    ridx = lax.broadcasted_iota(jnp.int32, (n_ctx, 1), 0)
    yc = (jnp.where(ridx == 0, 0.0, pltpu.roll(xc * w[3:4], 1, 0)) + xc * w[4:5]
          + jnp.where(ridx == n_ctx - 1, 0.0, pltpu.roll(xc * w[5:6], n_ctx - 1, 0)))
    o_ref[0, 0:n_ctx, :] = post(yc)

    zeros = jnp.zeros((GRID_W, tc), F32)
    pad_ref[0:GRID_W, :] = zeros
    pad_ref[GRID_W + n_lat:GRID_W + n_lat + GRID_W, :] = zeros
    pad_ref[GRID_W:GRID_W + n_lat, :] = p_ref[0, n_ctx:n_ctx + n_lat, :]
    rows = min(CONV_ROWS, n_lat)
    col = lax.broadcasted_iota(jnp.int32, (rows, 1), 0) % GRID_W

    def body(i, carry):
        s = pl.multiple_of(i * rows, rows)
        up = pad_ref[pl.ds(s, rows), :]
        mid = pad_ref[pl.ds(s + GRID_W, rows), :]
        dn = pad_ref[pl.ds(s + 2 * GRID_W, rows), :]
        a0 = up * w[0:1] + mid * w[3:4] + dn * w[6:7]
        a1 = up * w[1:2] + mid * w[4:5] + dn * w[7:8]
        a2 = up * w[2:3] + mid * w[5:6] + dn * w[8:9]
        y = (jnp.where(col == 0, 0.0, pltpu.roll(a0, 1, 0)) + a1
             + jnp.where(col == GRID_W - 1, 0.0, pltpu.roll(a2, rows - 1, 0)))
        o_ref[0, pl.ds(n_ctx + s, rows), :] = post(y)
        return carry

    lax.fori_loop(0, n_lat // rows, body, 0)


def _conv_qkv(p, conv_w9, n_ctx, n_lat):
    b, tot, _ = p.shape
    c = conv_w9.shape[1]
    tc = 2 * GDN_DIM
    kern = functools.partial(_conv_kernel, n_ctx=n_ctx, n_lat=n_lat, tc=tc, qk_cols=2 * GDN_HEADS * GDN_DIM)
    return pl.pallas_call(
        kern,
        out_shape=jax.ShapeDtypeStruct((b, tot, c), F32),
        grid=(b, c // tc),
        in_specs=[pl.BlockSpec((1, tot, tc), lambda i, j: (i, 0, j)),
                  pl.BlockSpec((9, tc), lambda i, j: (0, j))],
        out_specs=pl.BlockSpec((1, tot, tc), lambda i, j: (i, 0, j)),
        scratch_shapes=[pltpu.VMEM((n_lat + 2 * GRID_W, tc), F32)],
        compiler_params=pltpu.CompilerParams(dimension_semantics=("arbitrary", "arbitrary"),
                                             vmem_limit_bytes=VMEM_LIMIT),
        name="conv",
    )(p, conv_w9)


def _bwd_chunk(n, n_ctx_chunks, n_chunks):
    return jnp.where(n < n_ctx_chunks, n_ctx_chunks - 1 - n, n_chunks + n_ctx_chunks - 1 - n)


def _cumsum_rows(x, reverse):
    n = x.shape[0]
    row = lax.broadcasted_iota(jnp.int32, (n, 1), 0)
    s = 1
    while s < n:
        if reverse:
            x = x + jnp.where(row < n - s, pltpu.roll(x, n - s, 0), 0.0)
        else:
            x = x + jnp.where(row >= s, pltpu.roll(x, s, 0), 0.0)
        s *= 2
    return x


def _tri_masks():
    r = lax.broadcasted_iota(jnp.int32, (CHUNK, CHUNK), 0)
    c = lax.broadcasted_iota(jnp.int32, (CHUNK, CHUNK), 1)
    return r >= c, r <= c, r == c


PREP_BATCH = 4
PREP_CHUNKS = TOK_BLOCK // CHUNK
SLAB = CHUNK + SUBLANE


def _substitute(n_scr, t_scr, reverse):
    n_seg = CHUNK // SUBLANE
    sub = lax.broadcasted_iota(jnp.int32, (SUBLANE, LANE), 0)
    for seg in (reversed(range(n_seg)) if reverse else range(n_seg)):
        js = range(SUBLANE * seg, CHUNK) if reverse else range(0, SUBLANE * (seg + 1))
        acc_groups = range(seg, n_seg) if reverse else range(0, seg + 1)
        t_scr[SUBLANE * seg * CHUNK:SUBLANE * (seg + 1) * CHUNK, :] = jnp.zeros((SUBLANE * CHUNK, LANE), F32)

        def body(step, carry, seg=seg, js=js, acc_groups=acc_groups):
            i = SUBLANE * seg + (SUBLANE - 1 - step if reverse else step)
            base = pl.multiple_of(i * CHUNK, CHUNK)
            acc = {g: jnp.zeros((SUBLANE, LANE), F32) for g in acc_groups}
            for j in js:
                nij = jnp.broadcast_to(n_scr[pl.ds(base + j, 1), :], (SUBLANE, LANE))
                for g in (range(j // SUBLANE, n_seg) if reverse else range(0, j // SUBLANE + 1)):
                    acc[g] = acc[g] + nij * t_scr[j * CHUNK + g * SUBLANE:j * CHUNK + (g + 1) * SUBLANE, :]
            acc[seg] = acc[seg] + jnp.where(sub + SUBLANE * seg == i, 1.0, 0.0)
            for g in acc_groups:
                t_scr[pl.ds(base + g * SUBLANE, SUBLANE), :] = acc[g]
            return carry

        lax.fori_loop(0, SUBLANE, body, 0)


def _gdn_prep_kernel(q_ref, k_ref, gc_ref, gr_ref, ta_ref, g_scr, n_scr, t_scr, r_scr):
    scale = GDN_DIM ** -0.5
    r = lax.broadcasted_iota(jnp.int32, (CHUNK, LANE), 0)
    c = lax.broadcasted_iota(jnp.int32, (CHUNK, LANE), 1)
    incl = (jnp.logical_and(r >= c, c < CHUNK), jnp.logical_and(r <= c, c < CHUNK))
    strict = (jnp.logical_and(r > c, c < CHUNK), jnp.logical_and(r < c, c < CHUNK))
    rl = lax.broadcasted_iota(jnp.int32, (LANE, LANE), 0)
    cl = lax.broadcasted_iota(jnp.int32, (LANE, LANE), 1)
    live = jnp.logical_and(rl < CHUNK, cl < CHUNK)
    tri_r = (jnp.logical_and(rl <= cl, live).astype(F32), jnp.logical_and(rl >= cl, live).astype(F32))
    k_zero = jnp.zeros((CHUNK, GDN_DIM), BF16)

    def per_chunk(fn):
        def outer(bb, carry):
            def inner(ch, carry2):
                fn(bb, ch, pl.multiple_of(ch * CHUNK, CHUNK), (bb * PREP_CHUNKS + ch) * GDN_HEADS)
                return carry2
            return lax.fori_loop(0, PREP_CHUNKS, inner, carry)
        lax.fori_loop(0, PREP_BATCH, outer, 0)

    def build(bb, ch, row0, p0):
        gcol = gc_ref[bb, pl.ds(row0, CHUNK), :]
        grow = gr_ref[bb, ch]
        cum_c = [_cumsum_rows(gcol, reverse=(d == 1)) for d in (0, 1)]
        cum_r = [_dot(grow, tri_r[d], precision=HIGHEST) for d in (0, 1)]
        for h in range(GDN_HEADS):
            sl = slice(h * GDN_DIM, (h + 1) * GDN_DIM)
            kb = k_ref[bb, pl.ds(row0, CHUNK), sl].astype(BF16)
            qk = jnp.concatenate([q_ref[bb, pl.ds(row0, CHUNK), sl].astype(BF16), kb], axis=0)
            gram = _dot_nt(qk, jnp.concatenate([kb, k_zero], axis=0))
            for d in (0, 1):
                bl = GDN_HEADS * d + h
                gl = 2 * GDN_HEADS + bl
                dec = jnp.exp(jnp.where(incl[d], cum_c[d][:, gl:gl + 1] - cum_r[d][gl:gl + 1, :], -jnp.inf))
                nm = jnp.where(strict[d], -(gram[CHUNK:] * gcol[:, bl:bl + 1] * dec), 0.0)
                g_scr[d, pl.ds(pl.multiple_of((p0 + h) * SLAB, SUBLANE), CHUNK), :] = nm
                ta_ref[bb, ch, d, h, CHUNK:2 * CHUNK, :] = (gram[:CHUNK] * scale * dec)[:, :CHUNK].astype(BF16)

    per_chunk(build)

    for d in (0, 1):
        def to_lanes(i, carry, d=d):
            blk = g_scr[d, pl.ds(i, LANE, stride=SLAB), :]
            n_scr[pl.ds(pl.multiple_of(i * CHUNK, CHUNK), CHUNK), :] = blk.T[:CHUNK]
            return carry

        lax.fori_loop(0, CHUNK, to_lanes, 0, unroll=8)
        _substitute(n_scr, t_scr, reverse=(d == 1))

        def from_lanes(i, carry):
            blk = t_scr[pl.ds(pl.multiple_of(i * CHUNK, CHUNK), CHUNK), :]
            r_scr[pl.ds(i, LANE, stride=SLAB), :] = blk.T
            return carry

        lax.fori_loop(0, CHUNK, from_lanes, 0, unroll=8)

        def finish(bb, ch, row0, p0, d=d):
            grow = gr_ref[bb, ch]
            for h in range(GDN_HEADS):
                bl = GDN_HEADS * d + h
                t = r_scr[pl.ds(pl.multiple_of((p0 + h) * SLAB, SUBLANE), CHUNK), :]
                ta_ref[bb, ch, d, h, 0:CHUNK, :] = (t * grow[bl:bl + 1, :CHUNK]).astype(BF16)

        per_chunk(finish)


def _gdn_prep(qkv, gd, gd_rows):
    b, tot, _ = qkv.shape
    n_chunks = tot // CHUNK
    w = GDN_HEADS * GDN_DIM
    assert PREP_BATCH * PREP_CHUNKS * GDN_HEADS == LANE and b % PREP_BATCH == 0
    return pl.pallas_call(
        _gdn_prep_kernel,
        out_shape=jax.ShapeDtypeStruct((b, n_chunks, 2, GDN_HEADS, 2 * CHUNK, CHUNK), BF16),
        grid=(b // PREP_BATCH, tot // TOK_BLOCK),
        in_specs=[pl.BlockSpec((PREP_BATCH, TOK_BLOCK, w), lambda i, s: (i, s, 0)),
                  pl.BlockSpec((PREP_BATCH, TOK_BLOCK, w), lambda i, s: (i, s, 1)),
                  pl.BlockSpec((PREP_BATCH, TOK_BLOCK, LANE), lambda i, s: (i, s, 0)),
                  pl.BlockSpec((PREP_BATCH, PREP_CHUNKS, 4 * GDN_HEADS, LANE), lambda i, s: (i, s, 0, 0))],
        out_specs=pl.BlockSpec((PREP_BATCH, PREP_CHUNKS, 2, GDN_HEADS, 2 * CHUNK, CHUNK),
                               lambda i, s: (i, s, 0, 0, 0, 0)),
        scratch_shapes=[pltpu.VMEM((2, LANE * SLAB, LANE), F32),
                        pltpu.VMEM((CHUNK * CHUNK, LANE), F32),
                        pltpu.VMEM((CHUNK * CHUNK, LANE), F32),
                        pltpu.VMEM((LANE * SLAB, CHUNK), F32)],
        compiler_params=pltpu.CompilerParams(dimension_semantics=("arbitrary", "arbitrary"),
                                             vmem_limit_bytes=VMEM_LIMIT),
        name="gdn_prep",
    )(qkv, qkv, gd, gd_rows)


def _gdn_kernel(qf, kf, vf, qb, kb, vb, gcf, gcb, taf, tab, of, ob, s_ref):
    @pl.when(pl.program_id(1) == 0)
    def _():
        s_ref[...] = jnp.zeros(s_ref.shape, F32)

    scale = GDN_DIM ** -0.5
    dirs = ((qf, kf, vf, gcf, taf, of, CHUNK - 1), (qb, kb, vb, gcb, tab, ob, 0))
    probs = []
    for d, (q_ref, k_ref, v_ref, gc_ref, ta_ref, o_ref, last) in enumerate(dirs):
        cum_c = _cumsum_rows(gc_ref[0], reverse=(d == 1))
        for h in range(GDN_HEADS):
            bl = GDN_HEADS * d + h
            gl = 2 * GDN_HEADS + bl
            sl = slice(h * GDN_DIM, (h + 1) * GDN_DIM)
            gc = cum_c[:, gl:gl + 1]
            probs.append(dict(bl=bl, sl=sl, gc=gc, g_tot=gc[last:last + 1, :], h=h,
                              q_ref=q_ref, k_ref=k_ref, v_ref=v_ref, ta_ref=ta_ref, o_ref=o_ref))
    for p in probs:
        p["k"] = p["k_ref"][0, :, p["sl"]]
        qk = jnp.concatenate([p["q_ref"][0, :, p["sl"]], p["k"]], axis=0).astype(BF16)
        p["s_old"] = s_ref[p["bl"]]
        p["qks"] = _dot(qk, p["s_old"].astype(BF16))
    for p in probs:
        p["eg"] = jnp.exp(p["gc"])
        x = p["v_ref"][0, :, p["sl"]] - p["eg"] * p["qks"][CHUNK:]
        p["ta"] = p["ta_ref"][0, 0, 0, p["h"]]
        p["v_new"] = _dot(p["ta"][:CHUNK], x.astype(BF16)).astype(BF16)
    for p in probs:
        kd = p["k"] * jnp.exp(p["g_tot"] - p["gc"])
        s_ref[p["bl"]] = p["s_old"] * jnp.exp(p["g_tot"]) + _dot_tn(kd.astype(BF16), p["v_new"])
    for p in probs:
        p["o_ref"][0, :, p["sl"]] = ((scale * p["eg"]) * p["qks"][:CHUNK] + _dot(p["ta"][CHUNK:], p["v_new"]))


def _gdn_scan(qkv, gd, ta, n_ctx_chunks):
    b, tot, _ = qkv.shape
    n_chunks = tot // CHUNK
    w = GDN_HEADS * GDN_DIM
    bwd = functools.partial(_bwd_chunk, n_ctx_chunks=n_ctx_chunks, n_chunks=n_chunks)

    def tok(col, rev):
        if rev:
            return pl.BlockSpec((1, CHUNK, w), lambda i, n: (i, bwd(n), col))
        return pl.BlockSpec((1, CHUNK, w), lambda i, n: (i, n, col))

    gcol_f = pl.BlockSpec((1, CHUNK, LANE), lambda i, n: (i, n, 0))
    gcol_b = pl.BlockSpec((1, CHUNK, LANE), lambda i, n: (i, bwd(n), 0))
    ta_blk = (1, 1, 1, GDN_HEADS, 2 * CHUNK, CHUNK)
    ta_f = pl.BlockSpec(ta_blk, lambda i, n: (i, n, 0, 0, 0, 0))
    ta_b = pl.BlockSpec(ta_blk, lambda i, n: (i, bwd(n), 1, 0, 0, 0))
    out = jax.ShapeDtypeStruct((b, tot, w), F32)
    return pl.pallas_call(
        _gdn_kernel,
        out_shape=(out, out),
        grid=(b, n_chunks),
        in_specs=[tok(0, False), tok(1, False), tok(2, False), tok(0, True), tok(1, True), tok(2, True),
                  gcol_f, gcol_b, ta_f, ta_b],
        out_specs=(tok(0, False), tok(0, True)),
        scratch_shapes=[pltpu.VMEM((2 * GDN_HEADS, GDN_DIM, GDN_DIM), F32)],
        compiler_params=pltpu.CompilerParams(dimension_semantics=("arbitrary", "arbitrary"),
                                             vmem_limit_bytes=VMEM_LIMIT),
        name="gdn",
    )(qkv, qkv, qkv, qkv, qkv, qkv, gd, gd, ta, ta)


def _gla_kernel(qkf, vf, qkb, vb, lrf, lrb, wf, wb, bf, bb, of, ob, s_ref):
    @pl.when(pl.program_id(1) == 0)
    def _():
        s_ref[...] = jnp.zeros(s_ref.shape, F32)

    scale = GLA_DK ** -0.5
    kw = GLA_HEADS * GLA_DK
    lower, upper, _ = _tri_masks()
    dirs = ((qkf, vf, lrf, wf, bf, of, lower, CHUNK // 2, CHUNK - 1),
            (qkb, vb, lrb, wb, bb, ob, upper, CHUNK - 1 - CHUNK // 2, 0))
    probs = []
    for d, (qk_ref, v_ref, lr_ref, w_ref, b_ref, o_ref, incl, centre, last) in enumerate(dirs):
        lo = 4 * GDN_HEADS + d * GLA_RANK
        z = _dot(lr_ref[0, :, lo:lo + GLA_RANK], w_ref[...], precision=HIGHEST) + b_ref[...]
        log_a = -_softplus(-z) * (1.0 / GLA_NORMALIZER)
        cum = _cumsum_rows(log_a, reverse=(d == 1))
        for h in range(GLA_HEADS):
            gc = cum[:, h * GLA_DK:(h + 1) * GLA_DK]
            probs.append(dict(idx=GLA_HEADS * d + h, gc=gc, incl=incl, o_ref=o_ref,
                              g_mid=gc[centre:centre + 1, :], g_tot=gc[last:last + 1, :],
                              q=qk_ref[0, :, h * GLA_DK:(h + 1) * GLA_DK] * scale,
                              k=qk_ref[0, :, kw + h * GLA_DK:kw + (h + 1) * GLA_DK],
                              v=v_ref[0, :, h * GLA_DV:(h + 1) * GLA_DV].astype(BF16),
                              vsl=slice(h * GLA_DV, (h + 1) * GLA_DV)))
    for p in probs:
        attn = _dot_nt((p["q"] * jnp.exp(p["gc"] - p["g_mid"])).astype(BF16),
                       (p["k"] * jnp.exp(p["g_mid"] - p["gc"])).astype(BF16))
        p["attn"] = jnp.where(p["incl"], attn, 0.0).astype(BF16)
        p["s_old"] = s_ref[p["idx"]]
    for p in probs:
        p["o_ref"][0, :, p["vsl"]] = (_dot(p["attn"], p["v"])
                                      + _dot_nt((p["q"] * jnp.exp(p["gc"])).astype(BF16), p["s_old"].astype(BF16)))
    for p in probs:
        kd = p["k"] * jnp.exp(p["g_tot"] - p["gc"])
        s_ref[p["idx"]] = p["s_old"] * jnp.exp(p["g_tot"]) + _dot_tn(p["v"], kd.astype(BF16))


def _gla_scan(p, lr, wf, wb, bf, bb, n_ctx_chunks, qk_col, v_col):
    b, tot, _ = p.shape
    n_chunks = tot // CHUNK
    w = GLA_HEADS * GLA_DV
    bwd = functools.partial(_bwd_chunk, n_ctx_chunks=n_ctx_chunks, n_chunks=n_chunks)

    def tok(width, col, rev):
        if rev:
            return pl.BlockSpec((1, CHUNK, width), lambda i, n: (i, bwd(n), col))
        return pl.BlockSpec((1, CHUNK, width), lambda i, n: (i, n, col))

    const2 = lambda i, n: (0, 0)
    out = jax.ShapeDtypeStruct((b, tot, w), F32)
    return pl.pallas_call(
        _gla_kernel,
        out_shape=(out, out),
        grid=(b, n_chunks),
        in_specs=[tok(w, qk_col, False), tok(w, v_col, False), tok(w, qk_col, True), tok(w, v_col, True),
                  tok(LANE, 0, False), tok(LANE, 0, True),
                  pl.BlockSpec(wf.shape, const2), pl.BlockSpec(wb.shape, const2),
                  pl.BlockSpec(bf.shape, const2), pl.BlockSpec(bb.shape, const2)],
        out_specs=(tok(w, 0, False), tok(w, 0, True)),
        scratch_shapes=[pltpu.VMEM((2 * GLA_HEADS, GLA_DV, GLA_DK), F32)],
        compiler_params=pltpu.CompilerParams(dimension_semantics=("arbitrary", "arbitrary"),
                                             vmem_limit_bytes=VMEM_LIMIT),
        name="gla",
    )(p, p, p, p, lr, lr, wf, wb, bf, bb)


def _head_rms(o, g, width):
    outs = []
    for h in range(o.shape[-1] // width):
        oh = o[:, h * width:(h + 1) * width]
        ms = jnp.mean(oh * oh, axis=-1, keepdims=True)
        outs.append(oh * lax.rsqrt(ms + EPS))
    return jnp.concatenate(outs, axis=-1) * g


def _merge_kernel(oaf, oab, obf, obb, z_ref, r_ref, ga_ref, gb_ref, x_ref, g1_ref, sh2_ref, sc2_ref,
                  na_ref, nb_ref, nf_ref, woa_ref, wob_ref, wo_ref, wr_ref, br_ref,
                  x1_ref, h2_ref, lg_ref):
    ya_in = _head_rms(oaf[0] + oab[0], na_ref[...], GDN_DIM) * _silu(z_ref[0])
    ya = _dot(ya_in.astype(BF16), woa_ref[...])
    yb_in = _head_rms(obf[0] + obb[0], nb_ref[...], GLA_DV) * _silu(r_ref[0])
    yb = _dot(yb_in.astype(BF16), wob_ref[...])
    y = jax.nn.sigmoid(ga_ref[0]) * ya + jax.nn.sigmoid(gb_ref[0]) * yb
    x1 = x_ref[0] + g1_ref[0] * _dot(y.astype(BF16), wo_ref[...])
    x1_ref[0] = x1
    ms = jnp.mean(x1 * x1, axis=-1, keepdims=True)
    h2 = (x1 * lax.rsqrt(ms + EPS) * nf_ref[...]) * (1.0 + sc2_ref[0]) + sh2_ref[0]
    _store_tile_rows(h2_ref, h2)
    lg_ref[0] = _dot(h2, wr_ref[...], precision=HIGHEST) + br_ref[...]


def _merge(oaf, oab, obf, obb, p, x, g1, sh2, sc2, na, nb, nf, woa, wob, wo, wr, br, cols):
    b, l, d = x.shape
    z_col, r_col, ga_col, gb_col = cols
    lat = lambda i, s: (i, s + 1, 0)
    row3 = lambda i, s: (i, 0, 0)
    const2 = lambda i, s: (0, 0)
    blk = lambda i, s: (i, s, 0)
    tb = (1, TOK_BLOCK, d)
    pcol = lambda c: pl.BlockSpec(tb, lambda i, s: (i, s + 1, c))
    vec = pl.BlockSpec((1, 1, d), row3)
    cvec = pl.BlockSpec((1, d), const2)
    wspec = pl.BlockSpec((d, d), const2)
    groups = d // LANE
    n_blk = l // TOK_BLOCK
    return pl.pallas_call(
        _merge_kernel,
        out_shape=(jax.ShapeDtypeStruct((b, l, d), F32),
                   jax.ShapeDtypeStruct((b * l * groups, LANE), F32),
                   jax.ShapeDtypeStruct((b, l, LANE), F32)),
        grid=(b, n_blk),
        in_specs=[pl.BlockSpec(tb, lat)] * 4 + [pcol(z_col), pcol(r_col), pcol(ga_col), pcol(gb_col),
                  pl.BlockSpec(tb, blk), vec, vec, vec, cvec, cvec, cvec, wspec, wspec, wspec,
                  pl.BlockSpec((d, LANE), const2), pl.BlockSpec((1, LANE), const2)],
        out_specs=(pl.BlockSpec(tb, blk),
                   pl.BlockSpec((TOK_BLOCK * groups, LANE), lambda i, s: (i * n_blk + s, 0)),
                   pl.BlockSpec((1, TOK_BLOCK, LANE), blk)),
        compiler_params=pltpu.CompilerParams(dimension_semantics=("arbitrary", "arbitrary"),
                                             vmem_limit_bytes=VMEM_LIMIT),
        name="merge",
    )(oaf, oab, obf, obb, p, p, p, p, x, g1, sh2, sc2, na, nb, nf, woa, wob, wo, wr, br)


def _row_copy(src_ref, dst_ref, sem, src_row, dst_row):
    src = pl.multiple_of(src_row * SUBLANE, SUBLANE)
    dst = pl.multiple_of(dst_row * SUBLANE, SUBLANE)
    return pltpu.make_async_copy(src_ref.at[pl.ds(src, SUBLANE)], dst_ref.at[pl.ds(dst, SUBLANE)], sem)


def _gather_step(step, n_steps, rows, idx_hbm, src_hbm, idx_smem, buf, idx_sem, row_sem):
    def idx_copy(s):
        return pltpu.make_async_copy(idx_hbm.at[s], idx_smem.at[s % 2], idx_sem.at[s % 2])

    def issue(s):
        for slot in (0, 1):
            @pl.when(s % 2 == slot)
            def _(slot=slot):
                def body(r, carry):
                    _row_copy(src_hbm, buf.at[slot], row_sem.at[slot], idx_smem[slot, r], r).start()
                    return carry

                lax.fori_loop(0, rows, body, 0, unroll=8)

    @pl.when(step == 0)
    def _():
        idx_copy(0).start()
        idx_copy(0).wait()
        issue(0)

        @pl.when(n_steps > 1)
        def _():
            idx_copy(1).start()

    @pl.when(step + 1 < n_steps)
    def _():
        idx_copy(step + 1).wait()
        issue(step + 1)

        @pl.when(step + 2 < n_steps)
        def _():
            idx_copy(step + 2).start()

    slot = step % 2
    pltpu.make_async_copy(src_hbm.at[pl.ds(0, rows * SUBLANE)], buf.at[slot], row_sem.at[slot]).wait()
    return buf.at[slot]


def _gather_scratch(rows):
    return [pltpu.SMEM((2, rows), jnp.int32), pltpu.VMEM((2, rows * SUBLANE, LANE), F32),
            pltpu.SemaphoreType.DMA((2,)), pltpu.SemaphoreType.DMA((2,))]


def _expert_kernel(be_ref, tok_ref, h_ref, wgu_ref, bgu_ref, wd_ref, bd_ref, y_ref, *gather_scratch):
    del be_ref
    de = wd_ref.shape[1]
    groups = wgu_ref.shape[1] // LANE
    x_ref = _gather_step(pl.program_id(0), pl.num_programs(0), EXPERT_BLOCK, tok_ref, h_ref, *gather_scratch)
    x = _load_tile_rows(x_ref, EXPERT_BLOCK, groups)
    gu = _dot(x.astype(BF16), wgu_ref[0]) + bgu_ref[0]
    glu = jnp.minimum(gu[:, :de], SWIGLU_LIMIT)
    lin = jnp.clip(gu[:, de:], -SWIGLU_LIMIT, SWIGLU_LIMIT)
    act = glu * jax.nn.sigmoid(SWIGLU_ALPHA * glu) * (lin + 1.0)
    _store_tile_rows(y_ref, _dot(act.astype(BF16), wd_ref[0]) + bd_ref[0])


def _experts(h, slot_tok, block_expert, w_gu, b_gu, w_down, b_down):
    ne, d, de2 = w_gu.shape
    de = de2 // 2
    groups = d // LANE
    assert groups == SUBLANE
    n_slots = slot_tok.shape[0]
    blk_rows = EXPERT_BLOCK * groups
    grid_spec = pltpu.PrefetchScalarGridSpec(
        num_scalar_prefetch=1,
        grid=(n_slots // EXPERT_BLOCK,),
        in_specs=[pl.BlockSpec(memory_space=pl.ANY), pl.BlockSpec(memory_space=pl.ANY),
                  pl.BlockSpec((1, d, de2), lambda i, be: (be[i], 0, 0)),
                  pl.BlockSpec((1, 1, de2), lambda i, be: (be[i], 0, 0)),
                  pl.BlockSpec((1, de, d), lambda i, be: (be[i], 0, 0)),
                  pl.BlockSpec((1, 1, d), lambda i, be: (be[i], 0, 0))],
        out_specs=pl.BlockSpec((blk_rows, LANE), lambda i, be: (i, 0)),
        scratch_shapes=_gather_scratch(EXPERT_BLOCK),
    )
    return pl.pallas_call(
        _expert_kernel,
        out_shape=jax.ShapeDtypeStruct((n_slots * groups, LANE), F32),
        grid_spec=grid_spec,
        compiler_params=pltpu.CompilerParams(dimension_semantics=("arbitrary",), vmem_limit_bytes=VMEM_LIMIT),
        name="experts",
    )(block_expert, slot_tok.reshape(n_slots // EXPERT_BLOCK, EXPERT_BLOCK), h,
      w_gu, b_gu.reshape(ne, 1, de2), w_down, b_down.reshape(ne, 1, d))


def _combine_kernel(dest_ref, yb_ref, x1_ref, p_ref, g2_ref, fg_ref, o_ref, *gather_scratch):
    groups = x1_ref.shape[-1] // LANE
    step = pl.program_id(0) * pl.num_programs(1) + pl.program_id(1)
    n_steps = pl.num_programs(0) * pl.num_programs(1)
    y4_ref = _gather_step(step, n_steps, TOK_BLOCK * TOP_K, dest_ref, yb_ref, *gather_scratch)
    p = p_ref[0]
    y = None
    for k in range(TOP_K):
        yk = p[:, k:k + 1] * _load_tile_rows(y4_ref, TOK_BLOCK, groups, first=k * groups, stride=TOP_K * groups)
        y = yk if y is None else y + yk
    x2 = x1_ref[0] + g2_ref[0] * y
    ms = jnp.mean(x2 * x2, axis=-1, keepdims=True)
    o_ref[0] = x2 * lax.rsqrt(ms + EPS) * fg_ref[...]


def _combine(x1, yb, dest, top_p, g2, final_g):
    b, l, d = x1.shape
    assert d // LANE == SUBLANE
    blk = lambda i, s: (i, s, 0)
    n_blk = l // TOK_BLOCK
    rows = TOK_BLOCK * TOP_K
    return pl.pallas_call(
        _combine_kernel,
        out_shape=jax.ShapeDtypeStruct((b, l, d), F32),
        grid=(b, n_blk),
        in_specs=[pl.BlockSpec(memory_space=pl.ANY), pl.BlockSpec(memory_space=pl.ANY),
                  pl.BlockSpec((1, TOK_BLOCK, d), blk),
                  pl.BlockSpec((1, TOK_BLOCK, TOP_K), blk),
                  pl.BlockSpec((1, 1, d), lambda i, s: (i, 0, 0)),
                  pl.BlockSpec((1, d), lambda i, s: (0, 0))],
        out_specs=pl.BlockSpec((1, TOK_BLOCK, d), blk),
        scratch_shapes=_gather_scratch(rows),
        compiler_params=pltpu.CompilerParams(dimension_semantics=("arbitrary", "arbitrary"),
                                             vmem_limit_bytes=VMEM_LIMIT),
        name="combine",
    )(dest.reshape(b * n_blk, rows), yb, x1, top_p, g2, final_g)


def _routing(logits):
    t = logits.shape[0]
    n_assign = t * TOP_K
    n_slots = -(-(n_assign + N_EXPERTS * (EXPERT_BLOCK - 1)) // EXPERT_BLOCK) * EXPERT_BLOCK
    top_logit, top_idx = lax.top_k(logits, TOP_K)
    top_p = jax.nn.softmax(top_logit, axis=-1)
    flat_e = top_idx.reshape(-1).astype(jnp.int32)
    experts = jnp.arange(N_EXPERTS, dtype=jnp.int32)
    lookup = lambda table, e: jnp.sum(jnp.where(e[:, None] == experts[None, :], table[None, :], 0), axis=1)
    pos = jnp.arange(n_assign, dtype=jnp.int32)
    sorted_e, order = lax.sort((flat_e, pos), num_keys=1)
    counts = jnp.sum((flat_e[:, None] == experts[None, :]).astype(jnp.int32), axis=0)
    start = jnp.cumsum(counts) - counts
    padded = (counts + EXPERT_BLOCK - 1) // EXPERT_BLOCK * EXPERT_BLOCK
    pad_end = jnp.cumsum(padded)
    pad_start = pad_end - padded
    dest_sorted = pos + lookup(pad_start - start, sorted_e)
    _, dest = lax.sort((order, dest_sorted), num_keys=1)
    block_start = jnp.arange(n_slots // EXPERT_BLOCK, dtype=jnp.int32) * EXPERT_BLOCK
    block_expert = jnp.minimum(jnp.sum((pad_end[None, :] <= block_start[:, None]).astype(jnp.int32), axis=1),
                               N_EXPERTS - 1)
    slot_e = jnp.repeat(block_expert, EXPERT_BLOCK)
    within = jnp.arange(n_slots, dtype=jnp.int32) - lookup(pad_start, slot_e)
    src = jnp.clip(lookup(start, slot_e) + within, 0, n_assign - 1)
    slot_tok = jnp.where(within < lookup(counts, slot_e), order[src] // TOP_K, 0)
    return slot_tok, dest, block_expert, top_p


def kernel(x, c, ctx, c_ctx, w_mod, b_mod, norm_mix_g, norm_ffn_g, w_in, conv_w, a_log_f, a_log_b, dt_bias_f,
           dt_bias_b, gdn_norm_g, gla_gate_w_f, gla_gate_w_b, gla_gate_b_f, gla_gate_b_b, gla_norm_g, w_out_a,
           w_out_b, w_out, w_router, b_router, w_gu, b_gu, w_down, b_down, final_norm_g):
    assert w_mod.shape[0] == 1, "single-layer kernel"
    b, l, d = x.shape
    n_ctx = ctx.shape[1]
    assert n_ctx == TOK_BLOCK and l % CONV_ROWS == 0 and d == GDN_HEADS * GDN_DIM
    gw = GDN_HEADS * GDN_DIM
    kw = GLA_HEADS * GLA_DK
    vw = GLA_HEADS * GLA_DV

    pad_rows = -(b + 1) % 8
    cc = jnp.concatenate([c, c_ctx[None], jnp.zeros((pad_rows, d), F32)], axis=0)
    mod = _modulation(cc, w_mod[0], b_mod[0])
    sh1, sc1, g1, sh2, sc2, g2 = [mod[:b, i * d:(i + 1) * d].reshape(b, 1, d) for i in range(6)]
    sh1c, sc1c = mod[b:b + 1, 0:d], mod[b:b + 1, d:2 * d]

    offs = {}
    o = 0
    for name, width in (("qa", gw), ("ka", gw), ("va", gw), ("za", gw), ("beta", 2 * GDN_HEADS),
                        ("dec", 2 * GDN_HEADS), ("qb", kw), ("kb", kw), ("vb", vw), ("rb", vw),
                        ("lr", 2 * GLA_RANK), ("gates", 2 * d)):
        offs[name] = (o, o + width)
        o += width
    wi = w_in[0]
    cols = lambda n: wi[:, offs[n][0]:offs[n][1]]
    w_main = jnp.concatenate([cols(n) for n in ("qa", "ka", "va", "za", "qb", "kb", "vb", "rb", "gates")],
                             axis=1).astype(BF16)
    zpad = lambda n: jnp.zeros((d, n), F32)
    w_small = jnp.concatenate([cols("beta"), cols("dec"), cols("lr"),
                               zpad(LANE - 4 * GDN_HEADS - 2 * GLA_RANK)], axis=1)
    w_small_hi = w_small.astype(BF16)
    w_small = jnp.stack([w_small_hi, (w_small - w_small_hi.astype(F32)).astype(BF16)])
    lane_pad = lambda v, lo: jnp.zeros((1, LANE), F32).at[0, lo:lo + v.shape[0]].set(v)
    coef_a = lane_pad(-jnp.exp(jnp.concatenate([a_log_f[0], a_log_b[0]])), 2 * GDN_HEADS)
    coef_b = lane_pad(jnp.concatenate([dt_bias_f[0], dt_bias_b[0]]), 2 * GDN_HEADS)

    p, gd = _input_projection(x, ctx, sh1, sc1, sh1c, sc1c, norm_mix_g, w_main, w_small, coef_a, coef_b)
    tot = p.shape[1]
    n_chunks = tot // CHUNK
    n_ctx_chunks = n_ctx // CHUNK

    qkv = _conv_qkv(p, conv_w[0].reshape(9, 3 * gw), n_ctx, l)
    gd_rows = gd[:, :, :4 * GDN_HEADS].reshape(b, n_chunks, CHUNK, 4 * GDN_HEADS).transpose(0, 1, 3, 2)
    gd_rows = jnp.pad(gd_rows, ((0, 0), (0, 0), (0, 0), (0, LANE - CHUNK)))
    oaf, oab = _gdn_scan(qkv, gd, _gdn_prep(qkv, gd, gd_rows), n_ctx_chunks)

    obf, obb = _gla_scan(p, gd, gla_gate_w_f[0], gla_gate_w_b[0], gla_gate_b_f, gla_gate_b_b, n_ctx_chunks,
                         qk_col=4, v_col=5)

    wr = jnp.zeros((d, LANE), F32).at[:, :N_EXPERTS].set(w_router[0])
    br = jnp.zeros((1, LANE), F32).at[0, :N_EXPERTS].set(b_router[0])
    x1, h2, logits = _merge(oaf, oab, obf, obb, p, x, g1, sh2, sc2,
                            jnp.tile(gdn_norm_g, (1, GDN_HEADS)), jnp.tile(gla_norm_g, (1, GLA_HEADS)), norm_ffn_g,
                            w_out_a[0].astype(BF16), w_out_b[0].astype(BF16), w_out[0].astype(BF16), wr, br,
                            cols=(3, 6, 7, 8))

    return _channel_mixing(x1, h2, logits, g2, w_gu, b_gu, w_down, b_down, final_norm_g)


def _channel_mixing(x1, h2, logits, g2, w_gu, b_gu, w_down, b_down, final_norm_g):
    b, l, d = x1.shape
    t = b * l
    slot_tok, dest, block_expert, top_p = _routing(logits.reshape(t, LANE)[:, :N_EXPERTS])
    yb = _experts(h2, slot_tok, block_expert, w_gu[0].astype(BF16), b_gu[0], w_down[0].astype(BF16), b_down[0])
    return _combine(x1, yb, dest, top_p.reshape(b, l, TOP_K), g2, final_norm_g.reshape(1, d))
```

```python
import functools

import jax
import jax.numpy as jnp
from jax import lax
from jax.experimental import pallas as pl
from jax.experimental.pallas import tpu as pltpu

F32 = jnp.float32
BF16 = jnp.bfloat16
HIGHEST = lax.Precision.HIGHEST

EPS = 1e-6
GRID_W = 64
CHUNK = 64
TOK_BLOCK = 256
CONV_ROWS = 512
GDN_HEADS = 8
GDN_DIM = 128
GLA_HEADS = 4
GLA_DK = 128
GLA_DV = 256
GLA_RANK = 16
GLA_NORMALIZER = 16.0
N_EXPERTS = 32
TOP_K = 4
SWIGLU_LIMIT = 7.0
SWIGLU_ALPHA = 1.702
EXPERT_BLOCK = 512
EXPERT_PHASES = 4
LANE = 128
SUBLANE = 8
VMEM_LIMIT = 56 * 1024 * 1024


def _silu(t):
    return t * jax.nn.sigmoid(t)


def _softplus(t):
    return jnp.maximum(t, 0.0) + jnp.log1p(jnp.exp(-jnp.abs(t)))


def _dot(a, b, **kw):
    return jnp.dot(a, b, preferred_element_type=F32, **kw)


def _dot_nt(a, b, **kw):
    return lax.dot_general(a, b, (((1,), (1,)), ((), ())), preferred_element_type=F32, **kw)


def _dot_tn(a, b, **kw):
    return lax.dot_general(a, b, (((0,), (0,)), ((), ())), preferred_element_type=F32, **kw)


def _store_tile_rows(ref, val):
    rows, width = val.shape
    groups = width // LANE
    for s in range(groups):
        ref[pl.ds(s, rows, stride=groups), :] = val[:, s * LANE:(s + 1) * LANE]


def _load_tile_rows(ref, rows, groups, first=0, stride=None):
    stride = groups if stride is None else stride
    return jnp.concatenate([ref[pl.ds(first + s, rows, stride=stride), :] for s in range(groups)], axis=-1)


def _mod_kernel(c_ref, w_ref, b_ref, o_ref):
    o_ref[...] = _dot(_silu(c_ref[...]), w_ref[...], precision=HIGHEST) + b_ref[...]


def _modulation(cc, w_mod, b_mod):
    rows, d = cc.shape
    n = w_mod.shape[1]
    tn = n // 4
    return pl.pallas_call(
        _mod_kernel,
        out_shape=jax.ShapeDtypeStruct((rows, n), F32),
        grid=(n // tn,),
        in_specs=[pl.BlockSpec((rows, d), lambda j: (0, 0)),
                  pl.BlockSpec((d, tn), lambda j: (0, j)),
                  pl.BlockSpec((1, tn), lambda j: (0, j))],
        out_specs=pl.BlockSpec((rows, tn), lambda j: (0, j)),
        compiler_params=pltpu.CompilerParams(vmem_limit_bytes=VMEM_LIMIT),
        name="mod",
    )(cc, w_mod, b_mod.reshape(1, n))


def _inproj_kernel(x_ref, ctx_ref, shl_ref, scl_ref, shc_ref, scc_ref, g_ref, w_ref, ws_ref, ca_ref, cb_ref,
                   p_ref, gd_ref):
    is_ctx = pl.program_id(1) == 0
    xin = jnp.where(is_ctx, ctx_ref[0], x_ref[0])
    shift = jnp.where(is_ctx, shc_ref[...], shl_ref[0])
    scale = jnp.where(is_ctx, scc_ref[...], scl_ref[0])
    ms = jnp.mean(xin * xin, axis=-1, keepdims=True)
    h = (xin * lax.rsqrt(ms + EPS) * g_ref[...]) * (1.0 + scale) + shift
    hb = h.astype(BF16)
    n_main = w_ref.shape[1]
    for j in range(n_main // 1024):
        p_ref[0, :, j * 1024:(j + 1) * 1024] = _dot(hb, w_ref[:, j * 1024:(j + 1) * 1024]).astype(p_ref.dtype)
    h_lo = (h - hb.astype(F32)).astype(BF16)
    raw = _dot(hb, ws_ref[0]) + (_dot(h_lo, ws_ref[0]) + _dot(hb, ws_ref[1]))
    lane = lax.broadcasted_iota(jnp.int32, raw.shape, 1)
    beta = jax.nn.sigmoid(raw)
    g = ca_ref[...] * _softplus(raw + cb_ref[...])
    gd_ref[0] = jnp.where(lane < 2 * GDN_HEADS, beta, jnp.where(lane < 4 * GDN_HEADS, g, raw))


def _input_projection(x, ctx, sh_lat, sc_lat, sh_ctx, sc_ctx, norm_g, w_main, w_small, coef_a, coef_b):
    b, l, d = x.shape
    n_blk = l // TOK_BLOCK + 1
    n_main = w_main.shape[1]
    tot = n_blk * TOK_BLOCK
    row3 = lambda i, s: (i, 0, 0)
    const2 = lambda i, s: (0, 0)
    blk = lambda i, s: (i, s, 0)
    return pl.pallas_call(
        _inproj_kernel,
        out_shape=(jax.ShapeDtypeStruct((b, tot, n_main), BF16),
                   jax.ShapeDtypeStruct((b, tot, LANE), F32)),
        grid=(b, n_blk),
        in_specs=[pl.BlockSpec((1, TOK_BLOCK, d), lambda i, s: (i, jnp.maximum(s - 1, 0), 0)),
                  pl.BlockSpec((1, TOK_BLOCK, d), row3),
                  pl.BlockSpec((1, 1, d), row3), pl.BlockSpec((1, 1, d), row3),
                  pl.BlockSpec((1, d), const2), pl.BlockSpec((1, d), const2),
                  pl.BlockSpec((1, d), const2),
                  pl.BlockSpec((d, n_main), const2, pipeline_mode=pl.Buffered(1)),
                  pl.BlockSpec((2, d, LANE), lambda i, s: (0, 0, 0)),
                  pl.BlockSpec((1, LANE), const2), pl.BlockSpec((1, LANE), const2)],
        out_specs=(pl.BlockSpec((1, TOK_BLOCK, n_main), blk),
                   pl.BlockSpec((1, TOK_BLOCK, LANE), blk)),
        compiler_params=pltpu.CompilerParams(dimension_semantics=("arbitrary", "arbitrary"),
                                             vmem_limit_bytes=VMEM_LIMIT),
        name="inproj",
    )(x, ctx, sh_lat, sc_lat, sh_ctx, sc_ctx, norm_g, w_main, w_small, coef_a, coef_b)


def _conv_kernel(p_ref, w_ref, o_ref, pad_ref, *, n_ctx, n_lat, tc, qk_cols):
    is_qk = pl.program_id(1) * tc < qk_cols
    w = w_ref[...]

    def post(y):
        s = _silu(y)
        outs = []
        for hh in range(tc // GDN_DIM):
            sh = s[:, hh * GDN_DIM:(hh + 1) * GDN_DIM]
            fac = lax.rsqrt(jnp.sum(sh * sh, axis=-1, keepdims=True) + EPS)
            outs.append(sh * jnp.where(is_qk, fac, 1.0))
        return outs[0] if len(outs) == 1 else jnp.concatenate(outs, axis=-1)

    xc = p_ref[0, 0:n_ctx, :].astype(F32)
    ridx = lax.broadcasted_iota(jnp.int32, (n_ctx, 1), 0)
    yc = (jnp.where(ridx == 0, 0.0, pltpu.roll(xc * w[3:4], 1, 0)) + xc * w[4:5]
          + jnp.where(ridx == n_ctx - 1, 0.0, pltpu.roll(xc * w[5:6], n_ctx - 1, 0)))
    o_ref[0, 0:n_ctx, :] = post(yc).astype(o_ref.dtype)

    zeros = jnp.zeros((GRID_W, tc), F32)
    pad_ref[0:GRID_W, :] = zeros
    pad_ref[GRID_W + n_lat:GRID_W + n_lat + GRID_W, :] = zeros
    pad_ref[GRID_W:GRID_W + n_lat, :] = p_ref[0, n_ctx:n_ctx + n_lat, :].astype(F32)
    rows = min(CONV_ROWS, n_lat)
    col = lax.broadcasted_iota(jnp.int32, (rows, 1), 0) % GRID_W

    def body(i, carry):
        s = pl.multiple_of(i * rows, rows)
        up = pad_ref[pl.ds(s, rows), :]
        mid = pad_ref[pl.ds(s + GRID_W, rows), :]
        dn = pad_ref[pl.ds(s + 2 * GRID_W, rows), :]
        a0 = up * w[0:1] + mid * w[3:4] + dn * w[6:7]
        a1 = up * w[1:2] + mid * w[4:5] + dn * w[7:8]
        a2 = up * w[2:3] + mid * w[5:6] + dn * w[8:9]
        y = (jnp.where(col == 0, 0.0, pltpu.roll(a0, 1, 0)) + a1
             + jnp.where(col == GRID_W - 1, 0.0, pltpu.roll(a2, rows - 1, 0)))
        o_ref[0, pl.ds(n_ctx + s, rows), :] = post(y).astype(o_ref.dtype)
        return carry

    lax.fori_loop(0, n_lat // rows, body, 0)


def _conv_qkv(p, conv_w9, n_ctx, n_lat):
    b, tot, _ = p.shape
    c = conv_w9.shape[1]
    tc = 2 * GDN_DIM
    kern = functools.partial(_conv_kernel, n_ctx=n_ctx, n_lat=n_lat, tc=tc, qk_cols=2 * GDN_HEADS * GDN_DIM)
    return pl.pallas_call(
        kern,
        out_shape=jax.ShapeDtypeStruct((b, tot, c), BF16),
        grid=(b, c // tc),
        in_specs=[pl.BlockSpec((1, tot, tc), lambda i, j: (i, 0, j)),
                  pl.BlockSpec((9, tc), lambda i, j: (0, j))],
        out_specs=pl.BlockSpec((1, tot, tc), lambda i, j: (i, 0, j)),
        scratch_shapes=[pltpu.VMEM((n_lat + 2 * GRID_W, tc), F32)],
        compiler_params=pltpu.CompilerParams(dimension_semantics=("arbitrary", "arbitrary"),
                                             vmem_limit_bytes=VMEM_LIMIT),
        name="conv",
    )(p, conv_w9)


def _bwd_chunk(n, n_ctx_chunks, n_chunks):
    return jnp.where(n < n_ctx_chunks, n_ctx_chunks - 1 - n, n_chunks + n_ctx_chunks - 1 - n)


def _cumsum_rows(x, reverse):
    n = x.shape[0]
    row = lax.broadcasted_iota(jnp.int32, (n, 1), 0)
    s = 1
    while s < n:
        if reverse:
            x = x + jnp.where(row < n - s, pltpu.roll(x, n - s, 0), 0.0)
        else:
            x = x + jnp.where(row >= s, pltpu.roll(x, s, 0), 0.0)
        s *= 2
    return x


def _tri_masks():
    r = lax.broadcasted_iota(jnp.int32, (CHUNK, CHUNK), 0)
    c = lax.broadcasted_iota(jnp.int32, (CHUNK, CHUNK), 1)
    return r >= c, r <= c, r == c


PREP_BATCH = 4
PREP_CHUNKS = TOK_BLOCK // CHUNK
SLAB = CHUNK + SUBLANE


def _substitute(n_scr, t_scr, reverse):
    n_seg = CHUNK // SUBLANE
    sub = lax.broadcasted_iota(jnp.int32, (SUBLANE, LANE), 0)
    for seg in (reversed(range(n_seg)) if reverse else range(n_seg)):
        js = range(SUBLANE * seg, CHUNK) if reverse else range(0, SUBLANE * (seg + 1))
        acc_groups = range(seg, n_seg) if reverse else range(0, seg + 1)
        t_scr[SUBLANE * seg * CHUNK:SUBLANE * (seg + 1) * CHUNK, :] = jnp.zeros((SUBLANE * CHUNK, LANE), F32)

        def body(step, carry, seg=seg, js=js, acc_groups=acc_groups):
            i = SUBLANE * seg + (SUBLANE - 1 - step if reverse else step)
            base = pl.multiple_of(i * CHUNK, CHUNK)
            acc = {g: jnp.zeros((SUBLANE, LANE), F32) for g in acc_groups}
            for j in js:
                nij = jnp.broadcast_to(n_scr[pl.ds(base + j, 1), :], (SUBLANE, LANE))
                for g in (range(j // SUBLANE, n_seg) if reverse else range(0, j // SUBLANE + 1)):
                    acc[g] = acc[g] + nij * t_scr[j * CHUNK + g * SUBLANE:j * CHUNK + (g + 1) * SUBLANE, :]
            acc[seg] = acc[seg] + jnp.where(sub + SUBLANE * seg == i, 1.0, 0.0)
            for g in acc_groups:
                t_scr[pl.ds(base + g * SUBLANE, SUBLANE), :] = acc[g]
            return carry

        lax.fori_loop(0, SUBLANE, body, 0)


def _gdn_prep_kernel(q_ref, k_ref, gc_ref, gr_ref, ta_ref, g_scr, n_scr, t_scr, r_scr):
    scale = GDN_DIM ** -0.5
    r = lax.broadcasted_iota(jnp.int32, (CHUNK, LANE), 0)
    c = lax.broadcasted_iota(jnp.int32, (CHUNK, LANE), 1)
    incl = (jnp.logical_and(r >= c, c < CHUNK), jnp.logical_and(r <= c, c < CHUNK))
    strict = (jnp.logical_and(r > c, c < CHUNK), jnp.logical_and(r < c, c < CHUNK))
    rl = lax.broadcasted_iota(jnp.int32, (LANE, LANE), 0)
    cl = lax.broadcasted_iota(jnp.int32, (LANE, LANE), 1)
    live = jnp.logical_and(rl < CHUNK, cl < CHUNK)
    tri_r = (jnp.logical_and(rl <= cl, live).astype(F32), jnp.logical_and(rl >= cl, live).astype(F32))
    k_zero = jnp.zeros((CHUNK, GDN_DIM), BF16)

    def per_chunk(fn):
        def outer(bb, carry):
            def inner(ch, carry2):
                fn(bb, ch, pl.multiple_of(ch * CHUNK, CHUNK), (bb * PREP_CHUNKS + ch) * GDN_HEADS)
                return carry2
            return lax.fori_loop(0, PREP_CHUNKS, inner, carry)
        lax.fori_loop(0, PREP_BATCH, outer, 0)

    def build(bb, ch, row0, p0):
        gcol = gc_ref[bb, pl.ds(row0, CHUNK), :]
        grow = gr_ref[bb, ch]
        cum_c = [_cumsum_rows(gcol, reverse=(d == 1)) for d in (0, 1)]
        cum_r = [_dot(grow, tri_r[d], precision=HIGHEST) for d in (0, 1)]
        for h in range(GDN_HEADS):
            sl = slice(h * GDN_DIM, (h + 1) * GDN_DIM)
            kb = k_ref[bb, pl.ds(row0, CHUNK), sl]
            qk = jnp.concatenate([q_ref[bb, pl.ds(row0, CHUNK), sl], kb], axis=0)
            gram = _dot_nt(qk, jnp.concatenate([kb, k_zero], axis=0))
            for d in (0, 1):
                bl = GDN_HEADS * d + h
                gl = 2 * GDN_HEADS + bl
                dec = jnp.exp(jnp.where(incl[d], cum_c[d][:, gl:gl + 1] - cum_r[d][gl:gl + 1, :], -jnp.inf))
                nm = jnp.where(strict[d], -(gram[CHUNK:] * gcol[:, bl:bl + 1] * dec), 0.0)
                g_scr[d, pl.ds(pl.multiple_of((p0 + h) * SLAB, SUBLANE), CHUNK), :] = nm
                ta_ref[bb, ch, d, h, CHUNK:2 * CHUNK, :] = (gram[:CHUNK] * scale * dec)[:, :CHUNK].astype(BF16)

    per_chunk(build)

    for d in (0, 1):
        def to_lanes(i, carry, d=d):
            blk = g_scr[d, pl.ds(i, LANE, stride=SLAB), :]
            n_scr[pl.ds(pl.multiple_of(i * CHUNK, CHUNK), CHUNK), :] = blk.T[:CHUNK]
            return carry

        lax.fori_loop(0, CHUNK, to_lanes, 0, unroll=8)
        _substitute(n_scr, t_scr, reverse=(d == 1))

        def from_lanes(i, carry):
            blk = t_scr[pl.ds(pl.multiple_of(i * CHUNK, CHUNK), CHUNK), :]
            r_scr[pl.ds(i, LANE, stride=SLAB), :] = blk.T
            return carry

        lax.fori_loop(0, CHUNK, from_lanes, 0, unroll=8)

        def finish(bb, ch, row0, p0, d=d):
            grow = gr_ref[bb, ch]
            for h in range(GDN_HEADS):
                bl = GDN_HEADS * d + h
                t = r_scr[pl.ds(pl.multiple_of((p0 + h) * SLAB, SUBLANE), CHUNK), :]
                ta_ref[bb, ch, d, h, 0:CHUNK, :] = (t * grow[bl:bl + 1, :CHUNK]).astype(BF16)

        per_chunk(finish)


def _gdn_prep(qkv, gd, gd_rows):
    b, tot, _ = qkv.shape
    n_chunks = tot // CHUNK
    w = GDN_HEADS * GDN_DIM
    assert PREP_BATCH * PREP_CHUNKS * GDN_HEADS == LANE and b % PREP_BATCH == 0
    return pl.pallas_call(
        _gdn_prep_kernel,
        out_shape=jax.ShapeDtypeStruct((b, n_chunks, 2, GDN_HEADS, 2 * CHUNK, CHUNK), BF16),
        grid=(b // PREP_BATCH, tot // TOK_BLOCK),
        in_specs=[pl.BlockSpec((PREP_BATCH, TOK_BLOCK, w), lambda i, s: (i, s, 0)),
                  pl.BlockSpec((PREP_BATCH, TOK_BLOCK, w), lambda i, s: (i, s, 1)),
                  pl.BlockSpec((PREP_BATCH, TOK_BLOCK, LANE), lambda i, s: (i, s, 0)),
                  pl.BlockSpec((PREP_BATCH, PREP_CHUNKS, 4 * GDN_HEADS, LANE), lambda i, s: (i, s, 0, 0))],
        out_specs=pl.BlockSpec((PREP_BATCH, PREP_CHUNKS, 2, GDN_HEADS, 2 * CHUNK, CHUNK),
                               lambda i, s: (i, s, 0, 0, 0, 0)),
        scratch_shapes=[pltpu.VMEM((2, LANE * SLAB, LANE), F32),
                        pltpu.VMEM((CHUNK * CHUNK, LANE), F32),
                        pltpu.VMEM((CHUNK * CHUNK, LANE), F32),
                        pltpu.VMEM((LANE * SLAB, CHUNK), F32)],
        compiler_params=pltpu.CompilerParams(dimension_semantics=("arbitrary", "arbitrary"),
                                             vmem_limit_bytes=VMEM_LIMIT),
        name="gdn_prep",
    )(qkv, qkv, gd, gd_rows)


def _gdn_kernel(qf, kf, vf, qb, kb, vb, gcf, gcb, taf, tab, of, ob, s_ref):
    @pl.when(pl.program_id(1) == 0)
    def _():
        s_ref[...] = jnp.zeros(s_ref.shape, F32)

    scale = GDN_DIM ** -0.5
    dirs = ((qf, kf, vf, gcf, taf, of, CHUNK - 1), (qb, kb, vb, gcb, tab, ob, 0))
    probs = []
    for d, (q_ref, k_ref, v_ref, gc_ref, ta_ref, o_ref, last) in enumerate(dirs):
        cum_c = _cumsum_rows(gc_ref[0], reverse=(d == 1))
        for h in range(GDN_HEADS):
            bl = GDN_HEADS * d + h
            gl = 2 * GDN_HEADS + bl
            sl = slice(h * GDN_DIM, (h + 1) * GDN_DIM)
            gc = cum_c[:, gl:gl + 1]
            probs.append(dict(bl=bl, sl=sl, gc=gc, g_tot=gc[last:last + 1, :], h=h,
                              q_ref=q_ref, k_ref=k_ref, v_ref=v_ref, ta_ref=ta_ref, o_ref=o_ref))
    for p in probs:
        p["k"] = p["k_ref"][0, :, p["sl"]]
        qk = jnp.concatenate([p["q_ref"][0, :, p["sl"]], p["k"]], axis=0)
        p["s_old"] = s_ref[p["bl"]]
        p["qks"] = _dot(qk, p["s_old"].astype(BF16))
    for p in probs:
        p["eg"] = jnp.exp(p["gc"])
        x = p["v_ref"][0, :, p["sl"]].astype(F32) - p["eg"] * p["qks"][CHUNK:]
        p["ta"] = p["ta_ref"][0, 0, 0, p["h"]]
        p["v_new"] = _dot(p["ta"][:CHUNK], x.astype(BF16)).astype(BF16)
    for p in probs:
        kd = p["k"].astype(F32) * jnp.exp(p["g_tot"] - p["gc"])
        s_ref[p["bl"]] = p["s_old"] * jnp.exp(p["g_tot"]) + _dot_tn(kd.astype(BF16), p["v_new"])
    for p in probs:
        o = (scale * p["eg"]) * p["qks"][:CHUNK] + _dot(p["ta"][CHUNK:], p["v_new"])
        p["o_ref"][0, :, p["sl"]] = o.astype(BF16)


def _gdn_scan(qkv, gd, ta, n_ctx_chunks):
    b, tot, _ = qkv.shape
    n_chunks = tot // CHUNK
    w = GDN_HEADS * GDN_DIM
    bwd = functools.partial(_bwd_chunk, n_ctx_chunks=n_ctx_chunks, n_chunks=n_chunks)

    def tok(col, rev):
        if rev:
            return pl.BlockSpec((1, CHUNK, w), lambda i, n: (i, bwd(n), col))
        return pl.BlockSpec((1, CHUNK, w), lambda i, n: (i, n, col))

    gcol_f = pl.BlockSpec((1, CHUNK, LANE), lambda i, n: (i, n, 0))
    gcol_b = pl.BlockSpec((1, CHUNK, LANE), lambda i, n: (i, bwd(n), 0))
    ta_blk = (1, 1, 1, GDN_HEADS, 2 * CHUNK, CHUNK)
    ta_f = pl.BlockSpec(ta_blk, lambda i, n: (i, n, 0, 0, 0, 0))
    ta_b = pl.BlockSpec(ta_blk, lambda i, n: (i, bwd(n), 1, 0, 0, 0))
    out = jax.ShapeDtypeStruct((b, tot, w), BF16)
    return pl.pallas_call(
        _gdn_kernel,
        out_shape=(out, out),
        grid=(b, n_chunks),
        in_specs=[tok(0, False), tok(1, False), tok(2, False), tok(0, True), tok(1, True), tok(2, True),
                  gcol_f, gcol_b, ta_f, ta_b],
        out_specs=(tok(0, False), tok(0, True)),
        scratch_shapes=[pltpu.VMEM((2 * GDN_HEADS, GDN_DIM, GDN_DIM), F32)],
        compiler_params=pltpu.CompilerParams(dimension_semantics=("arbitrary", "arbitrary"),
                                             vmem_limit_bytes=VMEM_LIMIT),
        name="gdn",
    )(qkv, qkv, qkv, qkv, qkv, qkv, gd, gd, ta, ta)


def _gla_kernel(qkf, vf, qkb, vb, lrf, lrb, wf, wb, bf, bb, of, ob, s_ref):
    @pl.when(pl.program_id(1) == 0)
    def _():
        s_ref[...] = jnp.zeros(s_ref.shape, F32)

    scale = GLA_DK ** -0.5
    kw = GLA_HEADS * GLA_DK
    lower, upper, _ = _tri_masks()
    dirs = ((qkf, vf, lrf, wf, bf, of, lower, CHUNK // 2, CHUNK - 1),
            (qkb, vb, lrb, wb, bb, ob, upper, CHUNK - 1 - CHUNK // 2, 0))
    probs = []
    for d, (qk_ref, v_ref, lr_ref, w_ref, b_ref, o_ref, incl, centre, last) in enumerate(dirs):
        lo = 4 * GDN_HEADS + d * GLA_RANK
        z = _dot(lr_ref[0, :, lo:lo + GLA_RANK], w_ref[...], precision=HIGHEST) + b_ref[...]
        log_a = -_softplus(-z) * (1.0 / GLA_NORMALIZER)
        cum = _cumsum_rows(log_a, reverse=(d == 1))
        for h in range(GLA_HEADS):
            gc = cum[:, h * GLA_DK:(h + 1) * GLA_DK]
            probs.append(dict(idx=GLA_HEADS * d + h, gc=gc, incl=incl, o_ref=o_ref,
                              g_mid=gc[centre:centre + 1, :], g_tot=gc[last:last + 1, :],
                              q=qk_ref[0, :, h * GLA_DK:(h + 1) * GLA_DK].astype(F32) * scale,
                              k=qk_ref[0, :, kw + h * GLA_DK:kw + (h + 1) * GLA_DK].astype(F32),
                              v=v_ref[0, :, h * GLA_DV:(h + 1) * GLA_DV],
                              vsl=slice(h * GLA_DV, (h + 1) * GLA_DV)))
    for p in probs:
        attn = _dot_nt((p["q"] * jnp.exp(p["gc"] - p["g_mid"])).astype(BF16),
                       (p["k"] * jnp.exp(p["g_mid"] - p["gc"])).astype(BF16))
        p["attn"] = jnp.where(p["incl"], attn, 0.0).astype(BF16)
        p["s_old"] = s_ref[p["idx"]]
    for p in probs:
        o = _dot(p["attn"], p["v"]) + _dot_nt((p["q"] * jnp.exp(p["gc"])).astype(BF16), p["s_old"].astype(BF16))
        p["o_ref"][0, :, p["vsl"]] = o.astype(BF16)
    for p in probs:
        kd = p["k"] * jnp.exp(p["g_tot"] - p["gc"])
        s_ref[p["idx"]] = p["s_old"] * jnp.exp(p["g_tot"]) + _dot_tn(p["v"], kd.astype(BF16))


def _gla_scan(p, lr, wf, wb, bf, bb, n_ctx_chunks, qk_col, v_col):
    b, tot, _ = p.shape
    n_chunks = tot // CHUNK
    w = GLA_HEADS * GLA_DV
    bwd = functools.partial(_bwd_chunk, n_ctx_chunks=n_ctx_chunks, n_chunks=n_chunks)

    def tok(width, col, rev):
        if rev:
            return pl.BlockSpec((1, CHUNK, width), lambda i, n: (i, bwd(n), col))
        return pl.BlockSpec((1, CHUNK, width), lambda i, n: (i, n, col))

    const2 = lambda i, n: (0, 0)
    out = jax.ShapeDtypeStruct((b, tot, w), BF16)
    return pl.pallas_call(
        _gla_kernel,
        out_shape=(out, out),
        grid=(b, n_chunks),
        in_specs=[tok(w, qk_col, False), tok(w, v_col, False), tok(w, qk_col, True), tok(w, v_col, True),
                  tok(LANE, 0, False), tok(LANE, 0, True),
                  pl.BlockSpec(wf.shape, const2), pl.BlockSpec(wb.shape, const2),
                  pl.BlockSpec(bf.shape, const2), pl.BlockSpec(bb.shape, const2)],
        out_specs=(tok(w, 0, False), tok(w, 0, True)),
        scratch_shapes=[pltpu.VMEM((2 * GLA_HEADS, GLA_DV, GLA_DK), F32)],
        compiler_params=pltpu.CompilerParams(dimension_semantics=("arbitrary", "arbitrary"),
                                             vmem_limit_bytes=VMEM_LIMIT),
        name="gla",
    )(p, p, p, p, lr, lr, wf, wb, bf, bb)


def _head_rms(o, g, width):
    outs = []
    for h in range(o.shape[-1] // width):
        oh = o[:, h * width:(h + 1) * width]
        ms = jnp.mean(oh * oh, axis=-1, keepdims=True)
        outs.append(oh * lax.rsqrt(ms + EPS))
    return jnp.concatenate(outs, axis=-1) * g


def _merge_kernel(oaf, oab, obf, obb, z_ref, r_ref, ga_ref, gb_ref, x_ref, g1_ref, sh2_ref, sc2_ref,
                  na_ref, nb_ref, nf_ref, woa_ref, wob_ref, wo_ref, wr_ref, br_ref,
                  x1_ref, h2_ref, lg_ref):
    f32 = lambda ref: ref[0].astype(F32)
    ya_in = _head_rms(f32(oaf) + f32(oab), na_ref[...], GDN_DIM) * _silu(f32(z_ref))
    ya = _dot(ya_in.astype(BF16), woa_ref[...])
    yb_in = _head_rms(f32(obf) + f32(obb), nb_ref[...], GLA_DV) * _silu(f32(r_ref))
    yb = _dot(yb_in.astype(BF16), wob_ref[...])
    y = jax.nn.sigmoid(f32(ga_ref)) * ya + jax.nn.sigmoid(f32(gb_ref)) * yb
    x1 = x_ref[0] + g1_ref[0] * _dot(y.astype(BF16), wo_ref[...])
    x1_ref[0] = x1
    ms = jnp.mean(x1 * x1, axis=-1, keepdims=True)
    h2 = (x1 * lax.rsqrt(ms + EPS) * nf_ref[...]) * (1.0 + sc2_ref[0]) + sh2_ref[0]
    _store_tile_rows(h2_ref, h2)
    lg_ref[0] = _dot(h2, wr_ref[...], precision=HIGHEST) + br_ref[...]


def _merge(oaf, oab, obf, obb, p, x, g1, sh2, sc2, na, nb, nf, woa, wob, wo, wr, br, cols):
    b, l, d = x.shape
    z_col, r_col, ga_col, gb_col = cols
    lat = lambda i, s: (i, s + 1, 0)
    row3 = lambda i, s: (i, 0, 0)
    const2 = lambda i, s: (0, 0)
    blk = lambda i, s: (i, s, 0)
    tb = (1, TOK_BLOCK, d)
    pcol = lambda c: pl.BlockSpec(tb, lambda i, s: (i, s + 1, c))
    vec = pl.BlockSpec((1, 1, d), row3)
    cvec = pl.BlockSpec((1, d), const2)
    wspec = pl.BlockSpec((d, d), const2)
    groups = d // LANE
    n_blk = l // TOK_BLOCK
    return pl.pallas_call(
        _merge_kernel,
        out_shape=(jax.ShapeDtypeStruct((b, l, d), F32),
                   jax.ShapeDtypeStruct((b * l * groups, LANE), F32),
                   jax.ShapeDtypeStruct((b, l, LANE), F32)),
        grid=(b, n_blk),
        in_specs=[pl.BlockSpec(tb, lat)] * 4 + [pcol(z_col), pcol(r_col), pcol(ga_col), pcol(gb_col),
                  pl.BlockSpec(tb, blk), vec, vec, vec, cvec, cvec, cvec, wspec, wspec, wspec,
                  pl.BlockSpec((d, LANE), const2), pl.BlockSpec((1, LANE), const2)],
        out_specs=(pl.BlockSpec(tb, blk),
                   pl.BlockSpec((TOK_BLOCK * groups, LANE), lambda i, s: (i * n_blk + s, 0)),
                   pl.BlockSpec((1, TOK_BLOCK, LANE), blk)),
        compiler_params=pltpu.CompilerParams(dimension_semantics=("arbitrary", "arbitrary"),
                                             vmem_limit_bytes=VMEM_LIMIT),
        name="merge",
    )(oaf, oab, obf, obb, p, p, p, p, x, g1, sh2, sc2, na, nb, nf, woa, wob, wo, wr, br)


def _row_copy(src_ref, dst_ref, sem, src_row, dst_row):
    src = pl.multiple_of(src_row * SUBLANE, SUBLANE)
    dst = pl.multiple_of(dst_row * SUBLANE, SUBLANE)
    return pltpu.make_async_copy(src_ref.at[pl.ds(src, SUBLANE)], dst_ref.at[pl.ds(dst, SUBLANE)], sem)


def _gather_step(step, n_steps, rows, phases, idx_hbm, src_hbm, idx_smem, buf, idx_sem, row_sem):
    part = rows // phases

    def idx_copy(s):
        return pltpu.make_async_copy(idx_hbm.at[s], idx_smem.at[s % 2], idx_sem.at[s % 2])

    def issue(s, lo, hi):
        for slot in (0, 1):
            @pl.when(s % 2 == slot)
            def _(slot=slot):
                def body(r2, carry):
                    for k in (0, 1):
                        r = 2 * r2 + k
                        _row_copy(src_hbm, buf.at[slot], row_sem.at[slot], idx_smem[slot, r], r).start(priority=k)
                    return carry

                lax.fori_loop(lo // 2, hi // 2, body, 0, unroll=4)

    @pl.when(step == 0)
    def _():
        idx_copy(0).start()
        idx_copy(0).wait()
        issue(0, 0, rows)

        @pl.when(n_steps > 1)
        def _():
            idx_copy(1).start()

    @pl.when(step + 1 < n_steps)
    def _():
        idx_copy(step + 1).wait()

    def prefetch(j):
        @pl.when(step + 1 < n_steps)
        def _():
            issue(step + 1, j * part, (j + 1) * part)

        if j == phases - 1:
            @pl.when(step + 2 < n_steps)
            def _():
                idx_copy(step + 2).start()

    slot = step % 2
    pltpu.make_async_copy(src_hbm.at[pl.ds(0, rows * SUBLANE)], buf.at[slot], row_sem.at[slot]).wait()
    return buf.at[slot], prefetch


def _gather_scratch(rows):
    return [pltpu.SMEM((2, rows), jnp.int32), pltpu.VMEM((2, rows * SUBLANE, LANE), F32),
            pltpu.SemaphoreType.DMA((2,)), pltpu.SemaphoreType.DMA((2,))]


def _expert_kernel(be_ref, tok_ref, h_ref, wgu_ref, bgu_ref, wd_ref, bd_ref, y_ref, *gather_scratch):
    del be_ref
    de = wd_ref.shape[1]
    groups = wgu_ref.shape[1] // LANE
    x_ref, prefetch = _gather_step(pl.program_id(0), pl.num_programs(0), EXPERT_BLOCK, EXPERT_PHASES,
                                   tok_ref, h_ref, *gather_scratch)
    x = _load_tile_rows(x_ref, EXPERT_BLOCK, groups).astype(BF16)
    y = None
    cw = de // EXPERT_PHASES
    for j in range(EXPERT_PHASES):
        prefetch(j)
        glu_sl = slice(j * cw, (j + 1) * cw)
        lin_sl = slice(de + j * cw, de + (j + 1) * cw)
        glu = jnp.minimum(_dot(x, wgu_ref[0, :, glu_sl]) + bgu_ref[0, :, glu_sl], SWIGLU_LIMIT)
        lin = jnp.clip(_dot(x, wgu_ref[0, :, lin_sl]) + bgu_ref[0, :, lin_sl], -SWIGLU_LIMIT, SWIGLU_LIMIT)
        act = glu * jax.nn.sigmoid(SWIGLU_ALPHA * glu) * (lin + 1.0)
        yj = _dot(act.astype(BF16), wd_ref[0, glu_sl, :])
        y = yj if y is None else y + yj
    _store_tile_rows(y_ref, y + bd_ref[0])


def _experts(h, slot_tok, block_expert, w_gu, b_gu, w_down, b_down):
    ne, d, de2 = w_gu.shape
    de = de2 // 2
    groups = d // LANE
    assert groups == SUBLANE
    n_slots = slot_tok.shape[0]
    blk_rows = EXPERT_BLOCK * groups
    grid_spec = pltpu.PrefetchScalarGridSpec(
        num_scalar_prefetch=1,
        grid=(n_slots // EXPERT_BLOCK,),
        in_specs=[pl.BlockSpec(memory_space=pl.ANY), pl.BlockSpec(memory_space=pl.ANY),
                  pl.BlockSpec((1, d, de2), lambda i, be: (be[i], 0, 0)),
                  pl.BlockSpec((1, 1, de2), lambda i, be: (be[i], 0, 0)),
                  pl.BlockSpec((1, de, d), lambda i, be: (be[i], 0, 0)),
                  pl.BlockSpec((1, 1, d), lambda i, be: (be[i], 0, 0))],
        out_specs=pl.BlockSpec((blk_rows, LANE), lambda i, be: (i, 0)),
        scratch_shapes=_gather_scratch(EXPERT_BLOCK),
    )
    return pl.pallas_call(
        _expert_kernel,
        out_shape=jax.ShapeDtypeStruct((n_slots * groups, LANE), F32),
        grid_spec=grid_spec,
        compiler_params=pltpu.CompilerParams(dimension_semantics=("arbitrary",), vmem_limit_bytes=VMEM_LIMIT),
        name="experts",
    )(block_expert, slot_tok.reshape(n_slots // EXPERT_BLOCK, EXPERT_BLOCK), h,
      w_gu, b_gu.reshape(ne, 1, de2), w_down, b_down.reshape(ne, 1, d))


def _combine_kernel(dest_ref, yb_ref, x1_ref, p_ref, g2_ref, fg_ref, o_ref, *gather_scratch):
    groups = x1_ref.shape[-1] // LANE
    step = pl.program_id(0) * pl.num_programs(1) + pl.program_id(1)
    n_steps = pl.num_programs(0) * pl.num_programs(1)
    y4_ref, prefetch = _gather_step(step, n_steps, TOK_BLOCK * TOP_K, TOP_K, dest_ref, yb_ref, *gather_scratch)
    p = p_ref[0]
    y = None
    for k in range(TOP_K):
        prefetch(k)
        yk = p[:, k:k + 1] * _load_tile_rows(y4_ref, TOK_BLOCK, groups, first=k * groups, stride=TOP_K * groups)
        y = yk if y is None else y + yk
    x2 = x1_ref[0] + g2_ref[0] * y
    ms = jnp.mean(x2 * x2, axis=-1, keepdims=True)
    o_ref[0] = x2 * lax.rsqrt(ms + EPS) * fg_ref[...]


def _combine(x1, yb, dest, top_p, g2, final_g):
    b, l, d = x1.shape
    assert d // LANE == SUBLANE
    blk = lambda i, s: (i, s, 0)
    n_blk = l // TOK_BLOCK
    rows = TOK_BLOCK * TOP_K
    return pl.pallas_call(
        _combine_kernel,
        out_shape=jax.ShapeDtypeStruct((b, l, d), F32),
        grid=(b, n_blk),
        in_specs=[pl.BlockSpec(memory_space=pl.ANY), pl.BlockSpec(memory_space=pl.ANY),
                  pl.BlockSpec((1, TOK_BLOCK, d), blk),
                  pl.BlockSpec((1, TOK_BLOCK, TOP_K), blk),
                  pl.BlockSpec((1, 1, d), lambda i, s: (i, 0, 0)),
                  pl.BlockSpec((1, d), lambda i, s: (0, 0))],
        out_specs=pl.BlockSpec((1, TOK_BLOCK, d), blk),
        scratch_shapes=_gather_scratch(rows),
        compiler_params=pltpu.CompilerParams(dimension_semantics=("arbitrary", "arbitrary"),
                                             vmem_limit_bytes=VMEM_LIMIT),
        name="combine",
    )(dest.reshape(b * n_blk, rows), yb, x1, top_p, g2, final_g)


def _routing(logits):
    t = logits.shape[0]
    n_assign = t * TOP_K
    n_slots = -(-(n_assign + N_EXPERTS * (EXPERT_BLOCK - 1)) // EXPERT_BLOCK) * EXPERT_BLOCK
    top_logit, top_idx = lax.top_k(logits, TOP_K)
    top_p = jax.nn.softmax(top_logit, axis=-1)
    flat_e = top_idx.reshape(-1).astype(jnp.int32)
    experts = jnp.arange(N_EXPERTS, dtype=jnp.int32)
    lookup = lambda table, e: jnp.sum(jnp.where(e[:, None] == experts[None, :], table[None, :], 0), axis=1)
    pos = jnp.arange(n_assign, dtype=jnp.int32)
    sorted_e, order = lax.sort((flat_e, pos), num_keys=1)
    counts = jnp.sum((flat_e[:, None] == experts[None, :]).astype(jnp.int32), axis=0)
    start = jnp.cumsum(counts) - counts
    padded = (counts + EXPERT_BLOCK - 1) // EXPERT_BLOCK * EXPERT_BLOCK
    pad_end = jnp.cumsum(padded)
    pad_start = pad_end - padded
    dest_sorted = pos + lookup(pad_start - start, sorted_e)
    _, dest = lax.sort((order, dest_sorted), num_keys=1)
    block_start = jnp.arange(n_slots // EXPERT_BLOCK, dtype=jnp.int32) * EXPERT_BLOCK
    block_expert = jnp.minimum(jnp.sum((pad_end[None, :] <= block_start[:, None]).astype(jnp.int32), axis=1),
                               N_EXPERTS - 1)
    slot_e = jnp.repeat(block_expert, EXPERT_BLOCK)
    within = jnp.arange(n_slots, dtype=jnp.int32) - lookup(pad_start, slot_e)
    src = jnp.clip(lookup(start, slot_e) + within, 0, n_assign - 1)
    slot_tok = jnp.where(within < lookup(counts, slot_e), order[src] // TOP_K, 0)
    return slot_tok, dest, block_expert, top_p


def kernel(x, c, ctx, c_ctx, w_mod, b_mod, norm_mix_g, norm_ffn_g, w_in, conv_w, a_log_f, a_log_b, dt_bias_f,
           dt_bias_b, gdn_norm_g, gla_gate_w_f, gla_gate_w_b, gla_gate_b_f, gla_gate_b_b, gla_norm_g, w_out_a,
           w_out_b, w_out, w_router, b_router, w_gu, b_gu, w_down, b_down, final_norm_g):
    assert w_mod.shape[0] == 1, "single-layer kernel"
    b, l, d = x.shape
    n_ctx = ctx.shape[1]
    assert n_ctx == TOK_BLOCK and l % CONV_ROWS == 0 and d == GDN_HEADS * GDN_DIM
    gw = GDN_HEADS * GDN_DIM
    kw = GLA_HEADS * GLA_DK
    vw = GLA_HEADS * GLA_DV

    pad_rows = -(b + 1) % 8
    cc = jnp.concatenate([c, c_ctx[None], jnp.zeros((pad_rows, d), F32)], axis=0)
    mod = _modulation(cc, w_mod[0], b_mod[0])
    sh1, sc1, g1, sh2, sc2, g2 = [mod[:b, i * d:(i + 1) * d].reshape(b, 1, d) for i in range(6)]
    sh1c, sc1c = mod[b:b + 1, 0:d], mod[b:b + 1, d:2 * d]

    offs = {}
    o = 0
    for name, width in (("qa", gw), ("ka", gw), ("va", gw), ("za", gw), ("beta", 2 * GDN_HEADS),
                        ("dec", 2 * GDN_HEADS), ("qb", kw), ("kb", kw), ("vb", vw), ("rb", vw),
                        ("lr", 2 * GLA_RANK), ("gates", 2 * d)):
        offs[name] = (o, o + width)
        o += width
    wi = w_in[0]
    cols = lambda n: wi[:, offs[n][0]:offs[n][1]]
    w_main = jnp.concatenate([cols(n) for n in ("qa", "ka", "va", "za", "qb", "kb", "vb", "rb", "gates")],
                             axis=1).astype(BF16)
    zpad = lambda n: jnp.zeros((d, n), F32)
    w_small = jnp.concatenate([cols("beta"), cols("dec"), cols("lr"),
                               zpad(LANE - 4 * GDN_HEADS - 2 * GLA_RANK)], axis=1)
    w_small_hi = w_small.astype(BF16)
    w_small = jnp.stack([w_small_hi, (w_small - w_small_hi.astype(F32)).astype(BF16)])
    lane_pad = lambda v, lo: jnp.zeros((1, LANE), F32).at[0, lo:lo + v.shape[0]].set(v)
    coef_a = lane_pad(-jnp.exp(jnp.concatenate([a_log_f[0], a_log_b[0]])), 2 * GDN_HEADS)
    coef_b = lane_pad(jnp.concatenate([dt_bias_f[0], dt_bias_b[0]]), 2 * GDN_HEADS)

    p, gd = _input_projection(x, ctx, sh1, sc1, sh1c, sc1c, norm_mix_g, w_main, w_small, coef_a, coef_b)
    tot = p.shape[1]
    n_chunks = tot // CHUNK
    n_ctx_chunks = n_ctx // CHUNK

    qkv = _conv_qkv(p, conv_w[0].reshape(9, 3 * gw), n_ctx, l)
    gd_rows = gd[:, :, :4 * GDN_HEADS].reshape(b, n_chunks, CHUNK, 4 * GDN_HEADS).transpose(0, 1, 3, 2)
    gd_rows = jnp.pad(gd_rows, ((0, 0), (0, 0), (0, 0), (0, LANE - CHUNK)))
    oaf, oab = _gdn_scan(qkv, gd, _gdn_prep(qkv, gd, gd_rows), n_ctx_chunks)

    obf, obb = _gla_scan(p, gd, gla_gate_w_f[0], gla_gate_w_b[0], gla_gate_b_f, gla_gate_b_b, n_ctx_chunks,
                         qk_col=4, v_col=5)

    wr = jnp.zeros((d, LANE), F32).at[:, :N_EXPERTS].set(w_router[0])
    br = jnp.zeros((1, LANE), F32).at[0, :N_EXPERTS].set(b_router[0])
    x1, h2, logits = _merge(oaf, oab, obf, obb, p, x, g1, sh2, sc2,
                            jnp.tile(gdn_norm_g, (1, GDN_HEADS)), jnp.tile(gla_norm_g, (1, GLA_HEADS)), norm_ffn_g,
                            w_out_a[0].astype(BF16), w_out_b[0].astype(BF16), w_out[0].astype(BF16), wr, br,
                            cols=(3, 6, 7, 8))

    return _channel_mixing(x1, h2, logits, g2, w_gu, b_gu, w_down, b_down, final_norm_g)


def _channel_mixing(x1, h2, logits, g2, w_gu, b_gu, w_down, b_down, final_norm_g):
    b, l, d = x1.shape
    t = b * l
    slot_tok, dest, block_expert, top_p = _routing(logits.reshape(t, LANE)[:, :N_EXPERTS])
    yb = _experts(h2, slot_tok, block_expert, w_gu[0].astype(BF16), b_gu[0], w_down[0].astype(BF16), b_down[0])
    return _combine(x1, yb, dest, top_p.reshape(b, l, TOP_K), g2, final_norm_g.reshape(1, d))
```

```python
import functools

import jax
import jax.numpy as jnp
from jax import lax
from jax.experimental import pallas as pl
from jax.experimental.pallas import tpu as pltpu

F32 = jnp.float32
BF16 = jnp.bfloat16
HIGHEST = lax.Precision.HIGHEST

EPS = 1e-6
GRID_W = 64
CHUNK = 64
TOK_BLOCK = 256
CONV_ROWS = 512
GDN_HEADS = 8
GDN_DIM = 128
GLA_HEADS = 4
GLA_DK = 128
GLA_DV = 256
GLA_RANK = 16
GLA_NORMALIZER = 16.0
N_EXPERTS = 32
TOP_K = 4
SWIGLU_LIMIT = 7.0
SWIGLU_ALPHA = 1.702
EXPERT_BLOCK = 512
EXPERT_HALVES = 2
LANE = 128
SUBLANE = 8
VMEM_LIMIT = 56 * 1024 * 1024


def _silu(t):
    return t * jax.nn.sigmoid(t)


def _softplus(t):
    return jnp.maximum(t, 0.0) + jnp.log1p(jnp.exp(-jnp.abs(t)))


def _dot(a, b, **kw):
    return jnp.dot(a, b, preferred_element_type=F32, **kw)


def _dot_nt(a, b, **kw):
    return lax.dot_general(a, b, (((1,), (1,)), ((), ())), preferred_element_type=F32, **kw)


def _dot_tn(a, b, **kw):
    return lax.dot_general(a, b, (((0,), (0,)), ((), ())), preferred_element_type=F32, **kw)


def _store_tile_rows(ref, val):
    rows, width = val.shape
    groups = width // LANE
    for s in range(groups):
        ref[pl.ds(s, rows, stride=groups), :] = val[:, s * LANE:(s + 1) * LANE]


def _load_tile_rows(ref, rows, groups, first=0, stride=None):
    stride = groups if stride is None else stride
    return jnp.concatenate([ref[pl.ds(first + s, rows, stride=stride), :] for s in range(groups)], axis=-1)


def _mod_kernel(c_ref, w_ref, b_ref, o_ref):
    o_ref[...] = _dot(_silu(c_ref[...]), w_ref[...], precision=HIGHEST) + b_ref[...]


def _modulation(cc, w_mod, b_mod):
    rows, d = cc.shape
    n = w_mod.shape[1]
    tn = n // 4
    return pl.pallas_call(
        _mod_kernel,
        out_shape=jax.ShapeDtypeStruct((rows, n), F32),
        grid=(n // tn,),
        in_specs=[pl.BlockSpec((rows, d), lambda j: (0, 0)),
                  pl.BlockSpec((d, tn), lambda j: (0, j)),
                  pl.BlockSpec((1, tn), lambda j: (0, j))],
        out_specs=pl.BlockSpec((rows, tn), lambda j: (0, j)),
        compiler_params=pltpu.CompilerParams(vmem_limit_bytes=VMEM_LIMIT),
        name="mod",
    )(cc, w_mod, b_mod.reshape(1, n))


def _inproj_kernel(x_ref, ctx_ref, shl_ref, scl_ref, shc_ref, scc_ref, g_ref, w_ref, ws_ref, ca_ref, cb_ref,
                   p_ref, gd_ref):
    is_ctx = pl.program_id(1) == 0
    xin = jnp.where(is_ctx, ctx_ref[0], x_ref[0])
    shift = jnp.where(is_ctx, shc_ref[...], shl_ref[0])
    scale = jnp.where(is_ctx, scc_ref[...], scl_ref[0])
    ms = jnp.mean(xin * xin, axis=-1, keepdims=True)
    h = (xin * lax.rsqrt(ms + EPS) * g_ref[...]) * (1.0 + scale) + shift
    hb = h.astype(BF16)
    n_main = w_ref.shape[1]
    for j in range(n_main // 1024):
        p_ref[0, :, j * 1024:(j + 1) * 1024] = _dot(hb, w_ref[:, j * 1024:(j + 1) * 1024]).astype(p_ref.dtype)
    h_lo = (h - hb.astype(F32)).astype(BF16)
    raw = _dot(hb, ws_ref[0]) + (_dot(h_lo, ws_ref[0]) + _dot(hb, ws_ref[1]))
    lane = lax.broadcasted_iota(jnp.int32, raw.shape, 1)
    beta = jax.nn.sigmoid(raw)
    g = ca_ref[...] * _softplus(raw + cb_ref[...])
    gd_ref[0] = jnp.where(lane < 2 * GDN_HEADS, beta, jnp.where(lane < 4 * GDN_HEADS, g, raw))


def _input_projection(x, ctx, sh_lat, sc_lat, sh_ctx, sc_ctx, norm_g, w_main, w_small, coef_a, coef_b):
    b, l, d = x.shape
    n_blk = l // TOK_BLOCK + 1
    n_main = w_main.shape[1]
    tot = n_blk * TOK_BLOCK
    row3 = lambda i, s: (i, 0, 0)
    const2 = lambda i, s: (0, 0)
    blk = lambda i, s: (i, s, 0)
    return pl.pallas_call(
        _inproj_kernel,
        out_shape=(jax.ShapeDtypeStruct((b, tot, n_main), BF16),
                   jax.ShapeDtypeStruct((b, tot, LANE), F32)),
        grid=(b, n_blk),
        in_specs=[pl.BlockSpec((1, TOK_BLOCK, d), lambda i, s: (i, jnp.maximum(s - 1, 0), 0)),
                  pl.BlockSpec((1, TOK_BLOCK, d), row3),
                  pl.BlockSpec((1, 1, d), row3), pl.BlockSpec((1, 1, d), row3),
                  pl.BlockSpec((1, d), const2), pl.BlockSpec((1, d), const2),
                  pl.BlockSpec((1, d), const2),
                  pl.BlockSpec((d, n_main), const2, pipeline_mode=pl.Buffered(1)),
                  pl.BlockSpec((2, d, LANE), lambda i, s: (0, 0, 0)),
                  pl.BlockSpec((1, LANE), const2), pl.BlockSpec((1, LANE), const2)],
        out_specs=(pl.BlockSpec((1, TOK_BLOCK, n_main), blk),
                   pl.BlockSpec((1, TOK_BLOCK, LANE), blk)),
        compiler_params=pltpu.CompilerParams(dimension_semantics=("arbitrary", "arbitrary"),
                                             vmem_limit_bytes=VMEM_LIMIT),
        name="inproj",
    )(x, ctx, sh_lat, sc_lat, sh_ctx, sc_ctx, norm_g, w_main, w_small, coef_a, coef_b)


def _conv_kernel(p_ref, w_ref, o_ref, pad_ref, *, n_ctx, n_lat, tc, qk_cols):
    is_qk = pl.program_id(1) * tc < qk_cols
    w = w_ref[...]

    def post(y):
        s = _silu(y)
        outs = []
        for hh in range(tc // GDN_DIM):
            sh = s[:, hh * GDN_DIM:(hh + 1) * GDN_DIM]
            fac = lax.rsqrt(jnp.sum(sh * sh, axis=-1, keepdims=True) + EPS)
            outs.append(sh * jnp.where(is_qk, fac, 1.0))
        return outs[0] if len(outs) == 1 else jnp.concatenate(outs, axis=-1)

    xc = p_ref[0, 0:n_ctx, :].astype(F32)
    ridx = lax.broadcasted_iota(jnp.int32, (n_ctx, 1), 0)
    yc = (jnp.where(ridx == 0, 0.0, pltpu.roll(xc * w[3:4], 1, 0)) + xc * w[4:5]
          + jnp.where(ridx == n_ctx - 1, 0.0, pltpu.roll(xc * w[5:6], n_ctx - 1, 0)))
    o_ref[0, 0:n_ctx, :] = post(yc).astype(o_ref.dtype)

    zeros = jnp.zeros((GRID_W, tc), F32)
    pad_ref[0:GRID_W, :] = zeros
    pad_ref[GRID_W + n_lat:GRID_W + n_lat + GRID_W, :] = zeros
    pad_ref[GRID_W:GRID_W + n_lat, :] = p_ref[0, n_ctx:n_ctx + n_lat, :].astype(F32)
    rows = min(CONV_ROWS, n_lat)
    col = lax.broadcasted_iota(jnp.int32, (rows, 1), 0) % GRID_W

    def body(i, carry):
        s = pl.multiple_of(i * rows, rows)
        up = pad_ref[pl.ds(s, rows), :]
        mid = pad_ref[pl.ds(s + GRID_W, rows), :]
        dn = pad_ref[pl.ds(s + 2 * GRID_W, rows), :]
        a0 = up * w[0:1] + mid * w[3:4] + dn * w[6:7]
        a1 = up * w[1:2] + mid * w[4:5] + dn * w[7:8]
        a2 = up * w[2:3] + mid * w[5:6] + dn * w[8:9]
        y = (jnp.where(col == 0, 0.0, pltpu.roll(a0, 1, 0)) + a1
             + jnp.where(col == GRID_W - 1, 0.0, pltpu.roll(a2, rows - 1, 0)))
        o_ref[0, pl.ds(n_ctx + s, rows), :] = post(y).astype(o_ref.dtype)
        return carry

    lax.fori_loop(0, n_lat // rows, body, 0)


def _conv_qkv(p, conv_w9, n_ctx, n_lat):
    b, tot, _ = p.shape
    c = conv_w9.shape[1]
    tc = 2 * GDN_DIM
    kern = functools.partial(_conv_kernel, n_ctx=n_ctx, n_lat=n_lat, tc=tc, qk_cols=2 * GDN_HEADS * GDN_DIM)
    return pl.pallas_call(
        kern,
        out_shape=jax.ShapeDtypeStruct((b, tot, c), BF16),
        grid=(b, c // tc),
        in_specs=[pl.BlockSpec((1, tot, tc), lambda i, j: (i, 0, j)),
                  pl.BlockSpec((9, tc), lambda i, j: (0, j))],
        out_specs=pl.BlockSpec((1, tot, tc), lambda i, j: (i, 0, j)),
        scratch_shapes=[pltpu.VMEM((n_lat + 2 * GRID_W, tc), F32)],
        compiler_params=pltpu.CompilerParams(dimension_semantics=("arbitrary", "arbitrary"),
                                             vmem_limit_bytes=VMEM_LIMIT),
        name="conv",
    )(p, conv_w9)


def _bwd_chunk(n, n_ctx_chunks, n_chunks):
    return jnp.where(n < n_ctx_chunks, n_ctx_chunks - 1 - n, n_chunks + n_ctx_chunks - 1 - n)


def _cumsum_rows(x, reverse):
    n = x.shape[0]
    row = lax.broadcasted_iota(jnp.int32, (n, 1), 0)
    s = 1
    while s < n:
        if reverse:
            x = x + jnp.where(row < n - s, pltpu.roll(x, n - s, 0), 0.0)
        else:
            x = x + jnp.where(row >= s, pltpu.roll(x, s, 0), 0.0)
        s *= 2
    return x


def _tri_masks():
    r = lax.broadcasted_iota(jnp.int32, (CHUNK, CHUNK), 0)
    c = lax.broadcasted_iota(jnp.int32, (CHUNK, CHUNK), 1)
    return r >= c, r <= c, r == c


PREP_BATCH = 4
PREP_CHUNKS = TOK_BLOCK // CHUNK
SLAB = CHUNK + SUBLANE


def _substitute(n_scr, t_scr, reverse):
    n_seg = CHUNK // SUBLANE
    sub = lax.broadcasted_iota(jnp.int32, (SUBLANE, LANE), 0)
    for seg in (reversed(range(n_seg)) if reverse else range(n_seg)):
        js = range(SUBLANE * seg, CHUNK) if reverse else range(0, SUBLANE * (seg + 1))
        acc_groups = range(seg, n_seg) if reverse else range(0, seg + 1)
        t_scr[SUBLANE * seg * CHUNK:SUBLANE * (seg + 1) * CHUNK, :] = jnp.zeros((SUBLANE * CHUNK, LANE), F32)

        def body(step, carry, seg=seg, js=js, acc_groups=acc_groups):
            i = SUBLANE * seg + (SUBLANE - 1 - step if reverse else step)
            base = pl.multiple_of(i * CHUNK, CHUNK)
            acc = {g: jnp.zeros((SUBLANE, LANE), F32) for g in acc_groups}
            for j in js:
                nij = jnp.broadcast_to(n_scr[pl.ds(base + j, 1), :], (SUBLANE, LANE))
                for g in (range(j // SUBLANE, n_seg) if reverse else range(0, j // SUBLANE + 1)):
                    acc[g] = acc[g] + nij * t_scr[j * CHUNK + g * SUBLANE:j * CHUNK + (g + 1) * SUBLANE, :]
            acc[seg] = acc[seg] + jnp.where(sub + SUBLANE * seg == i, 1.0, 0.0)
            for g in acc_groups:
                t_scr[pl.ds(base + g * SUBLANE, SUBLANE), :] = acc[g]
            return carry

        lax.fori_loop(0, SUBLANE, body, 0)


def _gdn_prep_kernel(q_ref, k_ref, gc_ref, gr_ref, ta_ref, g_scr, n_scr, t_scr, r_scr):
    scale = GDN_DIM ** -0.5
    r = lax.broadcasted_iota(jnp.int32, (CHUNK, LANE), 0)
    c = lax.broadcasted_iota(jnp.int32, (CHUNK, LANE), 1)
    incl = (jnp.logical_and(r >= c, c < CHUNK), jnp.logical_and(r <= c, c < CHUNK))
    strict = (jnp.logical_and(r > c, c < CHUNK), jnp.logical_and(r < c, c < CHUNK))
    rl = lax.broadcasted_iota(jnp.int32, (LANE, LANE), 0)
    cl = lax.broadcasted_iota(jnp.int32, (LANE, LANE), 1)
    live = jnp.logical_and(rl < CHUNK, cl < CHUNK)
    tri_r = (jnp.logical_and(rl <= cl, live).astype(F32), jnp.logical_and(rl >= cl, live).astype(F32))
    k_zero = jnp.zeros((CHUNK, GDN_DIM), BF16)

    def per_chunk(fn):
        def outer(bb, carry):
            def inner(ch, carry2):
                fn(bb, ch, pl.multiple_of(ch * CHUNK, CHUNK), (bb * PREP_CHUNKS + ch) * GDN_HEADS)
                return carry2
            return lax.fori_loop(0, PREP_CHUNKS, inner, carry)
        lax.fori_loop(0, PREP_BATCH, outer, 0)

    def build(bb, ch, row0, p0):
        gcol = gc_ref[bb, pl.ds(row0, CHUNK), :]
        grow = gr_ref[bb, ch]
        cum_c = [_cumsum_rows(gcol, reverse=(d == 1)) for d in (0, 1)]
        cum_r = [_dot(grow, tri_r[d], precision=HIGHEST) for d in (0, 1)]
        for h in range(GDN_HEADS):
            sl = slice(h * GDN_DIM, (h + 1) * GDN_DIM)
            kb = k_ref[bb, pl.ds(row0, CHUNK), sl]
            qk = jnp.concatenate([q_ref[bb, pl.ds(row0, CHUNK), sl], kb], axis=0)
            gram = _dot_nt(qk, jnp.concatenate([kb, k_zero], axis=0))
            for d in (0, 1):
                bl = GDN_HEADS * d + h
                gl = 2 * GDN_HEADS + bl
                dec = jnp.exp(jnp.where(incl[d], cum_c[d][:, gl:gl + 1] - cum_r[d][gl:gl + 1, :], -jnp.inf))
                nm = jnp.where(strict[d], -(gram[CHUNK:] * gcol[:, bl:bl + 1] * dec), 0.0)
                g_scr[d, pl.ds(pl.multiple_of((p0 + h) * SLAB, SUBLANE), CHUNK), :] = nm
                ta_ref[bb, ch, d, h, CHUNK:2 * CHUNK, :] = (gram[:CHUNK] * scale * dec)[:, :CHUNK].astype(BF16)

    per_chunk(build)

    for d in (0, 1):
        def to_lanes(i, carry, d=d):
            blk = g_scr[d, pl.ds(i, LANE, stride=SLAB), :]
            n_scr[pl.ds(pl.multiple_of(i * CHUNK, CHUNK), CHUNK), :] = blk.T[:CHUNK]
            return carry

        lax.fori_loop(0, CHUNK, to_lanes, 0, unroll=8)
        _substitute(n_scr, t_scr, reverse=(d == 1))

        def from_lanes(i, carry):
            blk = t_scr[pl.ds(pl.multiple_of(i * CHUNK, CHUNK), CHUNK), :]
            r_scr[pl.ds(i, LANE, stride=SLAB), :] = blk.T
            return carry

        lax.fori_loop(0, CHUNK, from_lanes, 0, unroll=8)

        def finish(bb, ch, row0, p0, d=d):
            grow = gr_ref[bb, ch]
            for h in range(GDN_HEADS):
                bl = GDN_HEADS * d + h
                t = r_scr[pl.ds(pl.multiple_of((p0 + h) * SLAB, SUBLANE), CHUNK), :]
                ta_ref[bb, ch, d, h, 0:CHUNK, :] = (t * grow[bl:bl + 1, :CHUNK]).astype(BF16)

        per_chunk(finish)


def _gdn_prep(qkv, gd, gd_rows):
    b, tot, _ = qkv.shape
    n_chunks = tot // CHUNK
    w = GDN_HEADS * GDN_DIM
    assert PREP_BATCH * PREP_CHUNKS * GDN_HEADS == LANE and b % PREP_BATCH == 0
    return pl.pallas_call(
        _gdn_prep_kernel,
        out_shape=jax.ShapeDtypeStruct((b, n_chunks, 2, GDN_HEADS, 2 * CHUNK, CHUNK), BF16),
        grid=(b // PREP_BATCH, tot // TOK_BLOCK),
        in_specs=[pl.BlockSpec((PREP_BATCH, TOK_BLOCK, w), lambda i, s: (i, s, 0)),
                  pl.BlockSpec((PREP_BATCH, TOK_BLOCK, w), lambda i, s: (i, s, 1)),
                  pl.BlockSpec((PREP_BATCH, TOK_BLOCK, LANE), lambda i, s: (i, s, 0)),
                  pl.BlockSpec((PREP_BATCH, PREP_CHUNKS, 4 * GDN_HEADS, LANE), lambda i, s: (i, s, 0, 0))],
        out_specs=pl.BlockSpec((PREP_BATCH, PREP_CHUNKS, 2, GDN_HEADS, 2 * CHUNK, CHUNK),
                               lambda i, s: (i, s, 0, 0, 0, 0)),
        scratch_shapes=[pltpu.VMEM((2, LANE * SLAB, LANE), F32),
                        pltpu.VMEM((CHUNK * CHUNK, LANE), F32),
                        pltpu.VMEM((CHUNK * CHUNK, LANE), F32),
                        pltpu.VMEM((LANE * SLAB, CHUNK), F32)],
        compiler_params=pltpu.CompilerParams(dimension_semantics=("arbitrary", "arbitrary"),
                                             vmem_limit_bytes=VMEM_LIMIT),
        name="gdn_prep",
    )(qkv, qkv, gd, gd_rows)


def _gdn_kernel(qf, kf, vf, qb, kb, vb, gcf, gcb, taf, tab, of, ob, s_ref):
    @pl.when(pl.program_id(1) == 0)
    def _():
        s_ref[...] = jnp.zeros(s_ref.shape, F32)

    scale = GDN_DIM ** -0.5
    dirs = ((qf, kf, vf, gcf, taf, of, CHUNK - 1), (qb, kb, vb, gcb, tab, ob, 0))
    probs = []
    for d, (q_ref, k_ref, v_ref, gc_ref, ta_ref, o_ref, last) in enumerate(dirs):
        cum_c = _cumsum_rows(gc_ref[0], reverse=(d == 1))
        for h in range(GDN_HEADS):
            bl = GDN_HEADS * d + h
            gl = 2 * GDN_HEADS + bl
            sl = slice(h * GDN_DIM, (h + 1) * GDN_DIM)
            gc = cum_c[:, gl:gl + 1]
            probs.append(dict(bl=bl, sl=sl, gc=gc, g_tot=gc[last:last + 1, :], h=h,
                              q_ref=q_ref, k_ref=k_ref, v_ref=v_ref, ta_ref=ta_ref, o_ref=o_ref))
    for p in probs:
        p["k"] = p["k_ref"][0, :, p["sl"]]
        qk = jnp.concatenate([p["q_ref"][0, :, p["sl"]], p["k"]], axis=0)
        p["s_old"] = s_ref[p["bl"]]
        p["qks"] = _dot(qk, p["s_old"].astype(BF16))
    for p in probs:
        p["eg"] = jnp.exp(p["gc"])
        x = p["v_ref"][0, :, p["sl"]].astype(F32) - p["eg"] * p["qks"][CHUNK:]
        p["ta"] = p["ta_ref"][0, 0, 0, p["h"]]
        p["v_new"] = _dot(p["ta"][:CHUNK], x.astype(BF16)).astype(BF16)
    for p in probs:
        kd = p["k"].astype(F32) * jnp.exp(p["g_tot"] - p["gc"])
        s_ref[p["bl"]] = p["s_old"] * jnp.exp(p["g_tot"]) + _dot_tn(kd.astype(BF16), p["v_new"])
    for p in probs:
        o = (scale * p["eg"]) * p["qks"][:CHUNK] + _dot(p["ta"][CHUNK:], p["v_new"])
        p["o_ref"][0, :, p["sl"]] = o.astype(BF16)


def _gdn_scan(qkv, gd, ta, n_ctx_chunks):
    b, tot, _ = qkv.shape
    n_chunks = tot // CHUNK
    w = GDN_HEADS * GDN_DIM
    bwd = functools.partial(_bwd_chunk, n_ctx_chunks=n_ctx_chunks, n_chunks=n_chunks)

    def tok(col, rev):
        if rev:
            return pl.BlockSpec((1, CHUNK, w), lambda i, n: (i, bwd(n), col))
        return pl.BlockSpec((1, CHUNK, w), lambda i, n: (i, n, col))

    gcol_f = pl.BlockSpec((1, CHUNK, LANE), lambda i, n: (i, n, 0))
    gcol_b = pl.BlockSpec((1, CHUNK, LANE), lambda i, n: (i, bwd(n), 0))
    ta_blk = (1, 1, 1, GDN_HEADS, 2 * CHUNK, CHUNK)
    ta_f = pl.BlockSpec(ta_blk, lambda i, n: (i, n, 0, 0, 0, 0))
    ta_b = pl.BlockSpec(ta_blk, lambda i, n: (i, bwd(n), 1, 0, 0, 0))
    out = jax.ShapeDtypeStruct((b, tot, w), BF16)
    return pl.pallas_call(
        _gdn_kernel,
        out_shape=(out, out),
        grid=(b, n_chunks),
        in_specs=[tok(0, False), tok(1, False), tok(2, False), tok(0, True), tok(1, True), tok(2, True),
                  gcol_f, gcol_b, ta_f, ta_b],
        out_specs=(tok(0, False), tok(0, True)),
        scratch_shapes=[pltpu.VMEM((2 * GDN_HEADS, GDN_DIM, GDN_DIM), F32)],
        compiler_params=pltpu.CompilerParams(dimension_semantics=("arbitrary", "arbitrary"),
                                             vmem_limit_bytes=VMEM_LIMIT),
        name="gdn",
    )(qkv, qkv, qkv, qkv, qkv, qkv, gd, gd, ta, ta)


def _gla_kernel(qkf, vf, qkb, vb, lrf, lrb, wf, wb, bf, bb, of, ob, s_ref):
    @pl.when(pl.program_id(1) == 0)
    def _():
        s_ref[...] = jnp.zeros(s_ref.shape, F32)

    scale = GLA_DK ** -0.5
    kw = GLA_HEADS * GLA_DK
    lower, upper, _ = _tri_masks()
    dirs = ((qkf, vf, lrf, wf, bf, of, lower, CHUNK // 2, CHUNK - 1),
            (qkb, vb, lrb, wb, bb, ob, upper, CHUNK - 1 - CHUNK // 2, 0))
    probs = []
    for d, (qk_ref, v_ref, lr_ref, w_ref, b_ref, o_ref, incl, centre, last) in enumerate(dirs):
        lo = 4 * GDN_HEADS + d * GLA_RANK
        z = _dot(lr_ref[0, :, lo:lo + GLA_RANK], w_ref[...], precision=HIGHEST) + b_ref[...]
        log_a = -_softplus(-z) * (1.0 / GLA_NORMALIZER)
        cum = _cumsum_rows(log_a, reverse=(d == 1))
        for h in range(GLA_HEADS):
            gc = cum[:, h * GLA_DK:(h + 1) * GLA_DK]
            probs.append(dict(idx=GLA_HEADS * d + h, gc=gc, incl=incl, o_ref=o_ref,
                              g_mid=gc[centre:centre + 1, :], g_tot=gc[last:last + 1, :],
                              q=qk_ref[0, :, h * GLA_DK:(h + 1) * GLA_DK].astype(F32) * scale,
                              k=qk_ref[0, :, kw + h * GLA_DK:kw + (h + 1) * GLA_DK].astype(F32),
                              v=v_ref[0, :, h * GLA_DV:(h + 1) * GLA_DV],
                              vsl=slice(h * GLA_DV, (h + 1) * GLA_DV)))
    for p in probs:
        attn = _dot_nt((p["q"] * jnp.exp(p["gc"] - p["g_mid"])).astype(BF16),
                       (p["k"] * jnp.exp(p["g_mid"] - p["gc"])).astype(BF16))
        p["attn"] = jnp.where(p["incl"], attn, 0.0).astype(BF16)
        p["s_old"] = s_ref[p["idx"]]
    for p in probs:
        o = _dot(p["attn"], p["v"]) + _dot_nt((p["q"] * jnp.exp(p["gc"])).astype(BF16), p["s_old"].astype(BF16))
        p["o_ref"][0, :, p["vsl"]] = o.astype(BF16)
    for p in probs:
        kd = p["k"] * jnp.exp(p["g_tot"] - p["gc"])
        s_ref[p["idx"]] = p["s_old"] * jnp.exp(p["g_tot"]) + _dot_tn(p["v"], kd.astype(BF16))


def _gla_scan(p, lr, wf, wb, bf, bb, n_ctx_chunks, qk_col, v_col):
    b, tot, _ = p.shape
    n_chunks = tot // CHUNK
    w = GLA_HEADS * GLA_DV
    bwd = functools.partial(_bwd_chunk, n_ctx_chunks=n_ctx_chunks, n_chunks=n_chunks)

    def tok(width, col, rev):
        if rev:
            return pl.BlockSpec((1, CHUNK, width), lambda i, n: (i, bwd(n), col))
        return pl.BlockSpec((1, CHUNK, width), lambda i, n: (i, n, col))

    const2 = lambda i, n: (0, 0)
    out = jax.ShapeDtypeStruct((b, tot, w), BF16)
    return pl.pallas_call(
        _gla_kernel,
        out_shape=(out, out),
        grid=(b, n_chunks),
        in_specs=[tok(w, qk_col, False), tok(w, v_col, False), tok(w, qk_col, True), tok(w, v_col, True),
                  tok(LANE, 0, False), tok(LANE, 0, True),
                  pl.BlockSpec(wf.shape, const2), pl.BlockSpec(wb.shape, const2),
                  pl.BlockSpec(bf.shape, const2), pl.BlockSpec(bb.shape, const2)],
        out_specs=(tok(w, 0, False), tok(w, 0, True)),
        scratch_shapes=[pltpu.VMEM((2 * GLA_HEADS, GLA_DV, GLA_DK), F32)],
        compiler_params=pltpu.CompilerParams(dimension_semantics=("arbitrary", "arbitrary"),
                                             vmem_limit_bytes=VMEM_LIMIT),
        name="gla",
    )(p, p, p, p, lr, lr, wf, wb, bf, bb)


def _head_rms(o, g, width):
    outs = []
    for h in range(o.shape[-1] // width):
        oh = o[:, h * width:(h + 1) * width]
        ms = jnp.mean(oh * oh, axis=-1, keepdims=True)
        outs.append(oh * lax.rsqrt(ms + EPS))
    return jnp.concatenate(outs, axis=-1) * g


def _merge_kernel(oaf, oab, obf, obb, z_ref, r_ref, ga_ref, gb_ref, x_ref, g1_ref, sh2_ref, sc2_ref,
                  na_ref, nb_ref, nf_ref, woa_ref, wob_ref, wo_ref, wr_ref, br_ref,
                  x1_ref, h2_ref, lg_ref):
    f32 = lambda ref: ref[0].astype(F32)
    ya_in = _head_rms(f32(oaf) + f32(oab), na_ref[...], GDN_DIM) * _silu(f32(z_ref))
    ya = _dot(ya_in.astype(BF16), woa_ref[...])
    yb_in = _head_rms(f32(obf) + f32(obb), nb_ref[...], GLA_DV) * _silu(f32(r_ref))
    yb = _dot(yb_in.astype(BF16), wob_ref[...])
    y = jax.nn.sigmoid(f32(ga_ref)) * ya + jax.nn.sigmoid(f32(gb_ref)) * yb
    x1 = x_ref[0] + g1_ref[0] * _dot(y.astype(BF16), wo_ref[...])
    x1_ref[0] = x1
    ms = jnp.mean(x1 * x1, axis=-1, keepdims=True)
    h2 = (x1 * lax.rsqrt(ms + EPS) * nf_ref[...]) * (1.0 + sc2_ref[0]) + sh2_ref[0]
    _store_tile_rows(h2_ref, h2)
    lg_ref[0] = _dot(h2, wr_ref[...], precision=HIGHEST) + br_ref[...]


def _merge(oaf, oab, obf, obb, p, x, g1, sh2, sc2, na, nb, nf, woa, wob, wo, wr, br, cols):
    b, l, d = x.shape
    z_col, r_col, ga_col, gb_col = cols
    lat = lambda i, s: (i, s + 1, 0)
    row3 = lambda i, s: (i, 0, 0)
    const2 = lambda i, s: (0, 0)
    blk = lambda i, s: (i, s, 0)
    tb = (1, TOK_BLOCK, d)
    pcol = lambda c: pl.BlockSpec(tb, lambda i, s: (i, s + 1, c))
    vec = pl.BlockSpec((1, 1, d), row3)
    cvec = pl.BlockSpec((1, d), const2)
    wspec = pl.BlockSpec((d, d), const2)
    groups = d // LANE
    n_blk = l // TOK_BLOCK
    return pl.pallas_call(
        _merge_kernel,
        out_shape=(jax.ShapeDtypeStruct((b, l, d), F32),
                   jax.ShapeDtypeStruct((b * l * groups, LANE), F32),
                   jax.ShapeDtypeStruct((b, l, LANE), F32)),
        grid=(b, n_blk),
        in_specs=[pl.BlockSpec(tb, lat)] * 4 + [pcol(z_col), pcol(r_col), pcol(ga_col), pcol(gb_col),
                  pl.BlockSpec(tb, blk), vec, vec, vec, cvec, cvec, cvec, wspec, wspec, wspec,
                  pl.BlockSpec((d, LANE), const2), pl.BlockSpec((1, LANE), const2)],
        out_specs=(pl.BlockSpec(tb, blk),
                   pl.BlockSpec((TOK_BLOCK * groups, LANE), lambda i, s: (i * n_blk + s, 0)),
                   pl.BlockSpec((1, TOK_BLOCK, LANE), blk)),
        compiler_params=pltpu.CompilerParams(dimension_semantics=("arbitrary", "arbitrary"),
                                             vmem_limit_bytes=VMEM_LIMIT),
        name="merge",
    )(oaf, oab, obf, obb, p, p, p, p, x, g1, sh2, sc2, na, nb, nf, woa, wob, wo, wr, br)


def _row_copy(src_ref, dst_ref, sem, src_row, dst_row):
    first = lambda row: row * SUBLANE if isinstance(row, int) else pl.multiple_of(row * SUBLANE, SUBLANE)
    src, dst = first(src_row), first(dst_row)
    return pltpu.make_async_copy(src_ref.at[pl.ds(src, SUBLANE)], dst_ref.at[pl.ds(dst, SUBLANE)], sem)


def _gather_step(step, n_steps, rows, phases, two_queues, idx_hbm, src_hbm, idx_smem, buf, idx_sem, row_sem):
    part = rows // phases

    def idx_copy(s):
        return pltpu.make_async_copy(idx_hbm.at[s], idx_smem.at[s % 2], idx_sem.at[s % 2])

    def issue(s, lo, hi):
        slot = s % 2
        for r in range(lo, hi):
            _row_copy(src_hbm, buf.at[slot], row_sem.at[slot], idx_smem[slot, r], r).start(
                priority=r % 2 if two_queues else 0)

    @pl.when(step == 0)
    def _():
        idx_copy(0).start()
        idx_copy(0).wait()

        def first_rows(r, carry):
            _row_copy(src_hbm, buf.at[0], row_sem.at[0], idx_smem[0, r], r).start()
            return carry

        lax.fori_loop(0, rows, first_rows, 0)

        @pl.when(n_steps > 1)
        def _():
            idx_copy(1).start()

    @pl.when(step + 1 < n_steps)
    def _():
        idx_copy(step + 1).wait()

    def prefetch(j):
        @pl.when(step + 1 < n_steps)
        def _():
            issue(step + 1, j * part, (j + 1) * part)

        if j == phases - 1:
            @pl.when(step + 2 < n_steps)
            def _():
                idx_copy(step + 2).start()

    slot = step % 2
    pltpu.make_async_copy(src_hbm.at[pl.ds(0, rows * SUBLANE)], buf.at[slot], row_sem.at[slot]).wait()
    return buf.at[slot], prefetch


def _gather_scratch(rows):
    return [pltpu.SMEM((2, rows), jnp.int32), pltpu.VMEM((2, rows * SUBLANE, LANE), F32),
            pltpu.SemaphoreType.DMA((2,)), pltpu.SemaphoreType.DMA((2,))]


def _expert_kernel(be_ref, tok_ref, h_ref, wgu_ref, bgu_ref, wd_ref, bd_ref, y_ref, *gather_scratch):
    del be_ref
    de = wd_ref.shape[1]
    groups = wgu_ref.shape[1] // LANE
    x_ref, prefetch = _gather_step(pl.program_id(0), pl.num_programs(0), EXPERT_BLOCK, 2 * EXPERT_HALVES, False,
                                   tok_ref, h_ref, *gather_scratch)
    x = _load_tile_rows(x_ref, EXPERT_BLOCK, groups).astype(BF16)
    acts = []
    cw = de // EXPERT_HALVES
    for j in range(EXPERT_HALVES):
        prefetch(j)
        glu_sl = slice(j * cw, (j + 1) * cw)
        lin_sl = slice(de + j * cw, de + (j + 1) * cw)
        glu = jnp.minimum(_dot(x, wgu_ref[0, :, glu_sl]) + bgu_ref[0, :, glu_sl], SWIGLU_LIMIT)
        lin = jnp.clip(_dot(x, wgu_ref[0, :, lin_sl]) + bgu_ref[0, :, lin_sl], -SWIGLU_LIMIT, SWIGLU_LIMIT)
        acts.append((glu * jax.nn.sigmoid(SWIGLU_ALPHA * glu) * (lin + 1.0)).astype(BF16))
    act = jnp.concatenate(acts, axis=-1)
    ow = wd_ref.shape[2] // EXPERT_HALVES
    gw = ow // LANE
    for j in range(EXPERT_HALVES):
        prefetch(EXPERT_HALVES + j)
        yj = _dot(act, wd_ref[0, :, j * ow:(j + 1) * ow]) + bd_ref[0, :, j * ow:(j + 1) * ow]
        for s in range(gw):
            y_ref[pl.ds(j * gw + s, EXPERT_BLOCK, stride=groups), :] = yj[:, s * LANE:(s + 1) * LANE]


def _experts(h, slot_tok, block_expert, w_gu, b_gu, w_down, b_down):
    ne, d, de2 = w_gu.shape
    de = de2 // 2
    groups = d // LANE
    assert groups == SUBLANE
    n_slots = slot_tok.shape[0]
    blk_rows = EXPERT_BLOCK * groups
    grid_spec = pltpu.PrefetchScalarGridSpec(
        num_scalar_prefetch=1,
        grid=(n_slots // EXPERT_BLOCK,),
        in_specs=[pl.BlockSpec(memory_space=pl.ANY), pl.BlockSpec(memory_space=pl.ANY),
                  pl.BlockSpec((1, d, de2), lambda i, be: (be[i], 0, 0)),
                  pl.BlockSpec((1, 1, de2), lambda i, be: (be[i], 0, 0)),
                  pl.BlockSpec((1, de, d), lambda i, be: (be[i], 0, 0)),
                  pl.BlockSpec((1, 1, d), lambda i, be: (be[i], 0, 0))],
        out_specs=pl.BlockSpec((blk_rows, LANE), lambda i, be: (i, 0)),
        scratch_shapes=_gather_scratch(EXPERT_BLOCK),
    )
    return pl.pallas_call(
        _expert_kernel,
        out_shape=jax.ShapeDtypeStruct((n_slots * groups, LANE), F32),
        grid_spec=grid_spec,
        compiler_params=pltpu.CompilerParams(dimension_semantics=("arbitrary",), vmem_limit_bytes=VMEM_LIMIT),
        name="experts",
    )(block_expert, slot_tok.reshape(n_slots // EXPERT_BLOCK, EXPERT_BLOCK), h,
      w_gu, b_gu.reshape(ne, 1, de2), w_down, b_down.reshape(ne, 1, d))


def _combine_kernel(dest_ref, yb_ref, x1_ref, p_ref, g2_ref, fg_ref, o_ref, *gather_scratch):
    groups = x1_ref.shape[-1] // LANE
    step = pl.program_id(0) * pl.num_programs(1) + pl.program_id(1)
    n_steps = pl.num_programs(0) * pl.num_programs(1)
    y4_ref, prefetch = _gather_step(step, n_steps, TOK_BLOCK * TOP_K, 1, True, dest_ref, yb_ref, *gather_scratch)
    prefetch(0)
    p = p_ref[0]
    y = None
    for k in range(TOP_K):
        yk = p[:, k:k + 1] * _load_tile_rows(y4_ref, TOK_BLOCK, groups, first=k * groups, stride=TOP_K * groups)
        y = yk if y is None else y + yk
    x2 = x1_ref[0] + g2_ref[0] * y
    ms = jnp.mean(x2 * x2, axis=-1, keepdims=True)
    o_ref[0] = x2 * lax.rsqrt(ms + EPS) * fg_ref[...]


def _combine(x1, yb, dest, top_p, g2, final_g):
    b, l, d = x1.shape
    assert d // LANE == SUBLANE
    blk = lambda i, s: (i, s, 0)
    n_blk = l // TOK_BLOCK
    rows = TOK_BLOCK * TOP_K
    return pl.pallas_call(
        _combine_kernel,
        out_shape=jax.ShapeDtypeStruct((b, l, d), F32),
        grid=(b, n_blk),
        in_specs=[pl.BlockSpec(memory_space=pl.ANY), pl.BlockSpec(memory_space=pl.ANY),
                  pl.BlockSpec((1, TOK_BLOCK, d), blk),
                  pl.BlockSpec((1, TOK_BLOCK, TOP_K), blk),
                  pl.BlockSpec((1, 1, d), lambda i, s: (i, 0, 0)),
                  pl.BlockSpec((1, d), lambda i, s: (0, 0))],
        out_specs=pl.BlockSpec((1, TOK_BLOCK, d), blk),
        scratch_shapes=_gather_scratch(rows),
        compiler_params=pltpu.CompilerParams(dimension_semantics=("arbitrary", "arbitrary"),
                                             vmem_limit_bytes=VMEM_LIMIT),
        name="combine",
    )(dest.reshape(b * n_blk, rows), yb, x1, top_p, g2, final_g)


def _routing(logits):
    t = logits.shape[0]
    n_assign = t * TOP_K
    n_slots = -(-(n_assign + N_EXPERTS * (EXPERT_BLOCK - 1)) // EXPERT_BLOCK) * EXPERT_BLOCK
    top_logit, top_idx = lax.top_k(logits, TOP_K)
    top_p = jax.nn.softmax(top_logit, axis=-1)
    flat_e = top_idx.reshape(-1).astype(jnp.int32)
    experts = jnp.arange(N_EXPERTS, dtype=jnp.int32)
    lookup = lambda table, e: jnp.sum(jnp.where(e[:, None] == experts[None, :], table[None, :], 0), axis=1)
    pos = jnp.arange(n_assign, dtype=jnp.int32)
    sorted_e, order = lax.sort((flat_e, pos), num_keys=1)
    counts = jnp.sum((flat_e[:, None] == experts[None, :]).astype(jnp.int32), axis=0)
    start = jnp.cumsum(counts) - counts
    padded = (counts + EXPERT_BLOCK - 1) // EXPERT_BLOCK * EXPERT_BLOCK
    pad_end = jnp.cumsum(padded)
    pad_start = pad_end - padded
    dest_sorted = pos + lookup(pad_start - start, sorted_e)
    _, dest = lax.sort((order, dest_sorted), num_keys=1)
    block_start = jnp.arange(n_slots // EXPERT_BLOCK, dtype=jnp.int32) * EXPERT_BLOCK
    block_expert = jnp.minimum(jnp.sum((pad_end[None, :] <= block_start[:, None]).astype(jnp.int32), axis=1),
                               N_EXPERTS - 1)
    slot_e = jnp.repeat(block_expert, EXPERT_BLOCK)
    within = jnp.arange(n_slots, dtype=jnp.int32) - lookup(pad_start, slot_e)
    src = jnp.clip(lookup(start, slot_e) + within, 0, n_assign - 1)
    slot_tok = jnp.where(within < lookup(counts, slot_e), order[src] // TOP_K, 0)
    return slot_tok, dest, block_expert, top_p


def kernel(x, c, ctx, c_ctx, w_mod, b_mod, norm_mix_g, norm_ffn_g, w_in, conv_w, a_log_f, a_log_b, dt_bias_f,
           dt_bias_b, gdn_norm_g, gla_gate_w_f, gla_gate_w_b, gla_gate_b_f, gla_gate_b_b, gla_norm_g, w_out_a,
           w_out_b, w_out, w_router, b_router, w_gu, b_gu, w_down, b_down, final_norm_g):
    assert w_mod.shape[0] == 1, "single-layer kernel"
    b, l, d = x.shape
    n_ctx = ctx.shape[1]
    assert n_ctx == TOK_BLOCK and l % CONV_ROWS == 0 and d == GDN_HEADS * GDN_DIM
    gw = GDN_HEADS * GDN_DIM
    kw = GLA_HEADS * GLA_DK
    vw = GLA_HEADS * GLA_DV

    pad_rows = -(b + 1) % 8
    cc = jnp.concatenate([c, c_ctx[None], jnp.zeros((pad_rows, d), F32)], axis=0)
    mod = _modulation(cc, w_mod[0], b_mod[0])
    sh1, sc1, g1, sh2, sc2, g2 = [mod[:b, i * d:(i + 1) * d].reshape(b, 1, d) for i in range(6)]
    sh1c, sc1c = mod[b:b + 1, 0:d], mod[b:b + 1, d:2 * d]

    offs = {}
    o = 0
    for name, width in (("qa", gw), ("ka", gw), ("va", gw), ("za", gw), ("beta", 2 * GDN_HEADS),
                        ("dec", 2 * GDN_HEADS), ("qb", kw), ("kb", kw), ("vb", vw), ("rb", vw),
                        ("lr", 2 * GLA_RANK), ("gates", 2 * d)):
        offs[name] = (o, o + width)
        o += width
    wi = w_in[0]
    cols = lambda n: wi[:, offs[n][0]:offs[n][1]]
    w_main = jnp.concatenate([cols(n) for n in ("qa", "ka", "va", "za", "qb", "kb", "vb", "rb", "gates")],
                             axis=1).astype(BF16)
    zpad = lambda n: jnp.zeros((d, n), F32)
    w_small = jnp.concatenate([cols("beta"), cols("dec"), cols("lr"),
                               zpad(LANE - 4 * GDN_HEADS - 2 * GLA_RANK)], axis=1)
    w_small_hi = w_small.astype(BF16)
    w_small = jnp.stack([w_small_hi, (w_small - w_small_hi.astype(F32)).astype(BF16)])
    lane_pad = lambda v, lo: jnp.zeros((1, LANE), F32).at[0, lo:lo + v.shape[0]].set(v)
    coef_a = lane_pad(-jnp.exp(jnp.concatenate([a_log_f[0], a_log_b[0]])), 2 * GDN_HEADS)
    coef_b = lane_pad(jnp.concatenate([dt_bias_f[0], dt_bias_b[0]]), 2 * GDN_HEADS)

    p, gd = _input_projection(x, ctx, sh1, sc1, sh1c, sc1c, norm_mix_g, w_main, w_small, coef_a, coef_b)
    tot = p.shape[1]
    n_chunks = tot // CHUNK
    n_ctx_chunks = n_ctx // CHUNK

    qkv = _conv_qkv(p, conv_w[0].reshape(9, 3 * gw), n_ctx, l)
    gd_rows = gd[:, :, :4 * GDN_HEADS].reshape(b, n_chunks, CHUNK, 4 * GDN_HEADS).transpose(0, 1, 3, 2)
    gd_rows = jnp.pad(gd_rows, ((0, 0), (0, 0), (0, 0), (0, LANE - CHUNK)))
    oaf, oab = _gdn_scan(qkv, gd, _gdn_prep(qkv, gd, gd_rows), n_ctx_chunks)

    obf, obb = _gla_scan(p, gd, gla_gate_w_f[0], gla_gate_w_b[0], gla_gate_b_f, gla_gate_b_b, n_ctx_chunks,
                         qk_col=4, v_col=5)

    wr = jnp.zeros((d, LANE), F32).at[:, :N_EXPERTS].set(w_router[0])
    br = jnp.zeros((1, LANE), F32).at[0, :N_EXPERTS].set(b_router[0])
    x1, h2, logits = _merge(oaf, oab, obf, obb, p, x, g1, sh2, sc2,
                            jnp.tile(gdn_norm_g, (1, GDN_HEADS)), jnp.tile(gla_norm_g, (1, GLA_HEADS)), norm_ffn_g,
                            w_out_a[0].astype(BF16), w_out_b[0].astype(BF16), w_out[0].astype(BF16), wr, br,
                            cols=(3, 6, 7, 8))

    return _channel_mixing(x1, h2, logits, g2, w_gu, b_gu, w_down, b_down, final_norm_g)


def _channel_mixing(x1, h2, logits, g2, w_gu, b_gu, w_down, b_down, final_norm_g):
    b, l, d = x1.shape
    t = b * l
    slot_tok, dest, block_expert, top_p = _routing(logits.reshape(t, LANE)[:, :N_EXPERTS])
    yb = _experts(h2, slot_tok, block_expert, w_gu[0].astype(BF16), b_gu[0], w_down[0].astype(BF16), b_down[0])
    return _combine(x1, yb, dest, top_p.reshape(b, l, TOP_K), g2, final_norm_g.reshape(1, d))
```

```python
import functools

import jax
import jax.numpy as jnp
from jax import lax
from jax.experimental import pallas as pl
from jax.experimental.pallas import tpu as pltpu

F32 = jnp.float32
BF16 = jnp.bfloat16
HIGHEST = lax.Precision.HIGHEST

EPS = 1e-6
GRID_W = 64
CHUNK = 64
TOK_BLOCK = 256
CONV_ROWS = 512
GDN_HEADS = 8
GDN_DIM = 128
GLA_HEADS = 4
GLA_DK = 128
GLA_DV = 256
GLA_RANK = 16
GLA_NORMALIZER = 16.0
N_EXPERTS = 32
TOP_K = 4
SWIGLU_LIMIT = 7.0
SWIGLU_ALPHA = 1.702
EXPERT_BLOCK = 512
LANE = 128
SUBLANE = 8
VMEM_LIMIT = 56 * 1024 * 1024


def _silu(t):
    return t * jax.nn.sigmoid(t)


def _softplus(t):
    return jnp.maximum(t, 0.0) + jnp.log1p(jnp.exp(-jnp.abs(t)))


def _dot(a, b, **kw):
    return jnp.dot(a, b, preferred_element_type=F32, **kw)


def _dot_nt(a, b, **kw):
    return lax.dot_general(a, b, (((1,), (1,)), ((), ())), preferred_element_type=F32, **kw)


def _dot_tn(a, b, **kw):
    return lax.dot_general(a, b, (((0,), (0,)), ((), ())), preferred_element_type=F32, **kw)


def _store_tile_rows(ref, val):
    rows, width = val.shape
    groups = width // LANE
    for s in range(groups):
        ref[pl.ds(s, rows, stride=groups), :] = val[:, s * LANE:(s + 1) * LANE]


def _load_tile_rows(ref, rows, groups, first=0, stride=None):
    stride = groups if stride is None else stride
    return jnp.concatenate([ref[pl.ds(first + s, rows, stride=stride), :] for s in range(groups)], axis=-1)


def _mod_kernel(c_ref, w_ref, b_ref, o_ref):
    o_ref[...] = _dot(_silu(c_ref[...]), w_ref[...], precision=HIGHEST) + b_ref[...]


def _modulation(cc, w_mod, b_mod):
    rows, d = cc.shape
    n = w_mod.shape[1]
    tn = n // 4
    return pl.pallas_call(
        _mod_kernel,
        out_shape=jax.ShapeDtypeStruct((rows, n), F32),
        grid=(n // tn,),
        in_specs=[pl.BlockSpec((rows, d), lambda j: (0, 0)),
                  pl.BlockSpec((d, tn), lambda j: (0, j)),
                  pl.BlockSpec((1, tn), lambda j: (0, j))],
        out_specs=pl.BlockSpec((rows, tn), lambda j: (0, j)),
        compiler_params=pltpu.CompilerParams(vmem_limit_bytes=VMEM_LIMIT),
        name="mod",
    )(cc, w_mod, b_mod.reshape(1, n))


def _inproj_kernel(x_ref, ctx_ref, shl_ref, scl_ref, shc_ref, scc_ref, g_ref, w_ref, ws_ref, ca_ref, cb_ref,
                   p_ref, gd_ref):
    is_ctx = pl.program_id(1) == 0
    xin = jnp.where(is_ctx, ctx_ref[0], x_ref[0])
    shift = jnp.where(is_ctx, shc_ref[...], shl_ref[0])
    scale = jnp.where(is_ctx, scc_ref[...], scl_ref[0])
    ms = jnp.mean(xin * xin, axis=-1, keepdims=True)
    h = (xin * lax.rsqrt(ms + EPS) * g_ref[...]) * (1.0 + scale) + shift
    hb = h.astype(BF16)
    n_main = w_ref.shape[1]
    for j in range(n_main // 1024):
        p_ref[0, :, j * 1024:(j + 1) * 1024] = _dot(hb, w_ref[:, j * 1024:(j + 1) * 1024]).astype(p_ref.dtype)
    h_lo = (h - hb.astype(F32)).astype(BF16)
    raw = _dot(hb, ws_ref[0]) + (_dot(h_lo, ws_ref[0]) + _dot(hb, ws_ref[1]))
    lane = lax.broadcasted_iota(jnp.int32, raw.shape, 1)
    beta = jax.nn.sigmoid(raw)
    g = ca_ref[...] * _softplus(raw + cb_ref[...])
    gd_ref[0] = jnp.where(lane < 2 * GDN_HEADS, beta, jnp.where(lane < 4 * GDN_HEADS, g, raw))


def _input_projection(x, ctx, sh_lat, sc_lat, sh_ctx, sc_ctx, norm_g, w_main, w_small, coef_a, coef_b):
    b, l, d = x.shape
    n_blk = l // TOK_BLOCK + 1
    n_main = w_main.shape[1]
    tot = n_blk * TOK_BLOCK
    row3 = lambda i, s: (i, 0, 0)
    const2 = lambda i, s: (0, 0)
    blk = lambda i, s: (i, s, 0)
    return pl.pallas_call(
        _inproj_kernel,
        out_shape=(jax.ShapeDtypeStruct((b, tot, n_main), BF16),
                   jax.ShapeDtypeStruct((b, tot, LANE), F32)),
        grid=(b, n_blk),
        in_specs=[pl.BlockSpec((1, TOK_BLOCK, d), lambda i, s: (i, jnp.maximum(s - 1, 0), 0)),
                  pl.BlockSpec((1, TOK_BLOCK, d), row3),
                  pl.BlockSpec((1, 1, d), row3), pl.BlockSpec((1, 1, d), row3),
                  pl.BlockSpec((1, d), const2), pl.BlockSpec((1, d), const2),
                  pl.BlockSpec((1, d), const2),
                  pl.BlockSpec((d, n_main), const2, pipeline_mode=pl.Buffered(1)),
                  pl.BlockSpec((2, d, LANE), lambda i, s: (0, 0, 0)),
                  pl.BlockSpec((1, LANE), const2), pl.BlockSpec((1, LANE), const2)],
        out_specs=(pl.BlockSpec((1, TOK_BLOCK, n_main), blk),
                   pl.BlockSpec((1, TOK_BLOCK, LANE), blk)),
        compiler_params=pltpu.CompilerParams(dimension_semantics=("arbitrary", "arbitrary"),
                                             vmem_limit_bytes=VMEM_LIMIT),
        name="inproj",
    )(x, ctx, sh_lat, sc_lat, sh_ctx, sc_ctx, norm_g, w_main, w_small, coef_a, coef_b)


def _conv_kernel(p_ref, w_ref, o_ref, pad_ref, *, n_ctx, n_lat, tc, qk_cols):
    is_qk = pl.program_id(1) * tc < qk_cols
    w = w_ref[...]

    def post(y):
        s = _silu(y)
        outs = []
        for hh in range(tc // GDN_DIM):
            sh = s[:, hh * GDN_DIM:(hh + 1) * GDN_DIM]
            fac = lax.rsqrt(jnp.sum(sh * sh, axis=-1, keepdims=True) + EPS)
            outs.append(sh * jnp.where(is_qk, fac, 1.0))
        return outs[0] if len(outs) == 1 else jnp.concatenate(outs, axis=-1)

    xc = p_ref[0, 0:n_ctx, :].astype(F32)
    ridx = lax.broadcasted_iota(jnp.int32, (n_ctx, 1), 0)
    yc = (jnp.where(ridx == 0, 0.0, pltpu.roll(xc * w[3:4], 1, 0)) + xc * w[4:5]
          + jnp.where(ridx == n_ctx - 1, 0.0, pltpu.roll(xc * w[5:6], n_ctx - 1, 0)))
    o_ref[0, 0:n_ctx, :] = post(yc).astype(o_ref.dtype)

    zeros = jnp.zeros((GRID_W, tc), F32)
    pad_ref[0:GRID_W, :] = zeros
    pad_ref[GRID_W + n_lat:GRID_W + n_lat + GRID_W, :] = zeros
    pad_ref[GRID_W:GRID_W + n_lat, :] = p_ref[0, n_ctx:n_ctx + n_lat, :].astype(F32)
    rows = min(CONV_ROWS, n_lat)
    col = lax.broadcasted_iota(jnp.int32, (rows, 1), 0) % GRID_W

    def body(i, carry):
        s = pl.multiple_of(i * rows, rows)
        up = pad_ref[pl.ds(s, rows), :]
        mid = pad_ref[pl.ds(s + GRID_W, rows), :]
        dn = pad_ref[pl.ds(s + 2 * GRID_W, rows), :]
        a0 = up * w[0:1] + mid * w[3:4] + dn * w[6:7]
        a1 = up * w[1:2] + mid * w[4:5] + dn * w[7:8]
        a2 = up * w[2:3] + mid * w[5:6] + dn * w[8:9]
        y = (jnp.where(col == 0, 0.0, pltpu.roll(a0, 1, 0)) + a1
             + jnp.where(col == GRID_W - 1, 0.0, pltpu.roll(a2, rows - 1, 0)))
        o_ref[0, pl.ds(n_ctx + s, rows), :] = post(y).astype(o_ref.dtype)
        return carry

    lax.fori_loop(0, n_lat // rows, body, 0)


def _conv_qkv(p, conv_w9, n_ctx, n_lat):
    b, tot, _ = p.shape
    c = conv_w9.shape[1]
    tc = 2 * GDN_DIM
    kern = functools.partial(_conv_kernel, n_ctx=n_ctx, n_lat=n_lat, tc=tc, qk_cols=2 * GDN_HEADS * GDN_DIM)
    return pl.pallas_call(
        kern,
        out_shape=jax.ShapeDtypeStruct((b, tot, c), BF16),
        grid=(b, c // tc),
        in_specs=[pl.BlockSpec((1, tot, tc), lambda i, j: (i, 0, j)),
                  pl.BlockSpec((9, tc), lambda i, j: (0, j))],
        out_specs=pl.BlockSpec((1, tot, tc), lambda i, j: (i, 0, j)),
        scratch_shapes=[pltpu.VMEM((n_lat + 2 * GRID_W, tc), F32)],
        compiler_params=pltpu.CompilerParams(dimension_semantics=("arbitrary", "arbitrary"),
                                             vmem_limit_bytes=VMEM_LIMIT),
        name="conv",
    )(p, conv_w9)


def _bwd_chunk(n, n_ctx_chunks, n_chunks):
    return jnp.where(n < n_ctx_chunks, n_ctx_chunks - 1 - n, n_chunks + n_ctx_chunks - 1 - n)


def _cumsum_rows(x, reverse):
    n = x.shape[0]
    row = lax.broadcasted_iota(jnp.int32, (n, 1), 0)
    s = 1
    while s < n:
        if reverse:
            x = x + jnp.where(row < n - s, pltpu.roll(x, n - s, 0), 0.0)
        else:
            x = x + jnp.where(row >= s, pltpu.roll(x, s, 0), 0.0)
        s *= 2
    return x


def _tri_masks():
    r = lax.broadcasted_iota(jnp.int32, (CHUNK, CHUNK), 0)
    c = lax.broadcasted_iota(jnp.int32, (CHUNK, CHUNK), 1)
    return r >= c, r <= c, r == c


PREP_BATCH = 4
PREP_CHUNKS = TOK_BLOCK // CHUNK
SLAB = CHUNK + SUBLANE


def _substitute(n_scr, t_scr, reverse):
    n_seg = CHUNK // SUBLANE
    sub = lax.broadcasted_iota(jnp.int32, (SUBLANE, LANE), 0)
    for seg in (reversed(range(n_seg)) if reverse else range(n_seg)):
        js = range(SUBLANE * seg, CHUNK) if reverse else range(0, SUBLANE * (seg + 1))
        acc_groups = range(seg, n_seg) if reverse else range(0, seg + 1)
        t_scr[SUBLANE * seg * CHUNK:SUBLANE * (seg + 1) * CHUNK, :] = jnp.zeros((SUBLANE * CHUNK, LANE), F32)

        def body(step, carry, seg=seg, js=js, acc_groups=acc_groups):
            i = SUBLANE * seg + (SUBLANE - 1 - step if reverse else step)
            base = pl.multiple_of(i * CHUNK, CHUNK)
            acc = {g: jnp.zeros((SUBLANE, LANE), F32) for g in acc_groups}
            for j in js:
                nij = jnp.broadcast_to(n_scr[pl.ds(base + j, 1), :], (SUBLANE, LANE))
                for g in (range(j // SUBLANE, n_seg) if reverse else range(0, j // SUBLANE + 1)):
                    acc[g] = acc[g] + nij * t_scr[j * CHUNK + g * SUBLANE:j * CHUNK + (g + 1) * SUBLANE, :]
            acc[seg] = acc[seg] + jnp.where(sub + SUBLANE * seg == i, 1.0, 0.0)
            for g in acc_groups:
                t_scr[pl.ds(base + g * SUBLANE, SUBLANE), :] = acc[g]
            return carry

        lax.fori_loop(0, SUBLANE, body, 0)


def _gdn_prep_kernel(q_ref, k_ref, gc_ref, gr_ref, ta_ref, g_scr, n_scr, t_scr, r_scr):
    scale = GDN_DIM ** -0.5
    r = lax.broadcasted_iota(jnp.int32, (CHUNK, LANE), 0)
    c = lax.broadcasted_iota(jnp.int32, (CHUNK, LANE), 1)
    incl = (jnp.logical_and(r >= c, c < CHUNK), jnp.logical_and(r <= c, c < CHUNK))
    strict = (jnp.logical_and(r > c, c < CHUNK), jnp.logical_and(r < c, c < CHUNK))
    rl = lax.broadcasted_iota(jnp.int32, (LANE, LANE), 0)
    cl = lax.broadcasted_iota(jnp.int32, (LANE, LANE), 1)
    live = jnp.logical_and(rl < CHUNK, cl < CHUNK)
    tri_r = (jnp.logical_and(rl <= cl, live).astype(F32), jnp.logical_and(rl >= cl, live).astype(F32))
    k_zero = jnp.zeros((CHUNK, GDN_DIM), BF16)

    def per_chunk(fn):
        def outer(bb, carry):
            def inner(ch, carry2):
                fn(bb, ch, pl.multiple_of(ch * CHUNK, CHUNK), (bb * PREP_CHUNKS + ch) * GDN_HEADS)
                return carry2
            return lax.fori_loop(0, PREP_CHUNKS, inner, carry)
        lax.fori_loop(0, PREP_BATCH, outer, 0)

    def build(bb, ch, row0, p0):
        gcol = gc_ref[bb, pl.ds(row0, CHUNK), :]
        grow = gr_ref[bb, ch]
        cum_c = [_cumsum_rows(gcol, reverse=(d == 1)) for d in (0, 1)]
        cum_r = [_dot(grow, tri_r[d], precision=HIGHEST) for d in (0, 1)]
        for h in range(GDN_HEADS):
            sl = slice(h * GDN_DIM, (h + 1) * GDN_DIM)
            kb = k_ref[bb, pl.ds(row0, CHUNK), sl]
            qk = jnp.concatenate([q_ref[bb, pl.ds(row0, CHUNK), sl], kb], axis=0)
            gram = _dot_nt(qk, jnp.concatenate([kb, k_zero], axis=0))
            for d in (0, 1):
                bl = GDN_HEADS * d + h
                gl = 2 * GDN_HEADS + bl
                dec = jnp.exp(jnp.where(incl[d], cum_c[d][:, gl:gl + 1] - cum_r[d][gl:gl + 1, :], -jnp.inf))
                nm = jnp.where(strict[d], -(gram[CHUNK:] * gcol[:, bl:bl + 1] * dec), 0.0)
                g_scr[d, pl.ds(pl.multiple_of((p0 + h) * SLAB, SUBLANE), CHUNK), :] = nm
                ta_ref[bb, ch, d, h, CHUNK:2 * CHUNK, :] = (gram[:CHUNK] * scale * dec)[:, :CHUNK].astype(BF16)

    per_chunk(build)

    for d in (0, 1):
        def to_lanes(i, carry, d=d):
            blk = g_scr[d, pl.ds(i, LANE, stride=SLAB), :]
            n_scr[pl.ds(pl.multiple_of(i * CHUNK, CHUNK), CHUNK), :] = blk.T[:CHUNK]
            return carry

        lax.fori_loop(0, CHUNK, to_lanes, 0, unroll=8)
        _substitute(n_scr, t_scr, reverse=(d == 1))

        def from_lanes(i, carry):
            blk = t_scr[pl.ds(pl.multiple_of(i * CHUNK, CHUNK), CHUNK), :]
            r_scr[pl.ds(i, LANE, stride=SLAB), :] = blk.T
            return carry

        lax.fori_loop(0, CHUNK, from_lanes, 0, unroll=8)

        def finish(bb, ch, row0, p0, d=d):
            grow = gr_ref[bb, ch]
            for h in range(GDN_HEADS):
                bl = GDN_HEADS * d + h
                t = r_scr[pl.ds(pl.multiple_of((p0 + h) * SLAB, SUBLANE), CHUNK), :]
                ta_ref[bb, ch, d, h, 0:CHUNK, :] = (t * grow[bl:bl + 1, :CHUNK]).astype(BF16)

        per_chunk(finish)


def _gdn_prep(qkv, gd, gd_rows):
    b, tot, _ = qkv.shape
    n_chunks = tot // CHUNK
    w = GDN_HEADS * GDN_DIM
    assert PREP_BATCH * PREP_CHUNKS * GDN_HEADS == LANE and b % PREP_BATCH == 0
    return pl.pallas_call(
        _gdn_prep_kernel,
        out_shape=jax.ShapeDtypeStruct((b, n_chunks, 2, GDN_HEADS, 2 * CHUNK, CHUNK), BF16),
        grid=(b // PREP_BATCH, tot // TOK_BLOCK),
        in_specs=[pl.BlockSpec((PREP_BATCH, TOK_BLOCK, w), lambda i, s: (i, s, 0)),
                  pl.BlockSpec((PREP_BATCH, TOK_BLOCK, w), lambda i, s: (i, s, 1)),
                  pl.BlockSpec((PREP_BATCH, TOK_BLOCK, LANE), lambda i, s: (i, s, 0)),
                  pl.BlockSpec((PREP_BATCH, PREP_CHUNKS, 4 * GDN_HEADS, LANE), lambda i, s: (i, s, 0, 0))],
        out_specs=pl.BlockSpec((PREP_BATCH, PREP_CHUNKS, 2, GDN_HEADS, 2 * CHUNK, CHUNK),
                               lambda i, s: (i, s, 0, 0, 0, 0)),
        scratch_shapes=[pltpu.VMEM((2, LANE * SLAB, LANE), F32),
                        pltpu.VMEM((CHUNK * CHUNK, LANE), F32),
                        pltpu.VMEM((CHUNK * CHUNK, LANE), F32),
                        pltpu.VMEM((LANE * SLAB, CHUNK), F32)],
        compiler_params=pltpu.CompilerParams(dimension_semantics=("arbitrary", "arbitrary"),
                                             vmem_limit_bytes=VMEM_LIMIT),
        name="gdn_prep",
    )(qkv, qkv, gd, gd_rows)


def _gdn_kernel(qf, kf, vf, qb, kb, vb, gcf, gcb, taf, tab, of, ob, s_ref):
    @pl.when(pl.program_id(1) == 0)
    def _():
        s_ref[...] = jnp.zeros(s_ref.shape, F32)

    scale = GDN_DIM ** -0.5
    dirs = ((qf, kf, vf, gcf, taf, of, CHUNK - 1), (qb, kb, vb, gcb, tab, ob, 0))
    probs = []
    for d, (q_ref, k_ref, v_ref, gc_ref, ta_ref, o_ref, last) in enumerate(dirs):
        cum_c = _cumsum_rows(gc_ref[0], reverse=(d == 1))
        for h in range(GDN_HEADS):
            bl = GDN_HEADS * d + h
            gl = 2 * GDN_HEADS + bl
            sl = slice(h * GDN_DIM, (h + 1) * GDN_DIM)
            gc = cum_c[:, gl:gl + 1]
            probs.append(dict(bl=bl, sl=sl, gc=gc, g_tot=gc[last:last + 1, :], h=h,
                              q_ref=q_ref, k_ref=k_ref, v_ref=v_ref, ta_ref=ta_ref, o_ref=o_ref))
    for p in probs:
        p["k"] = p["k_ref"][0, :, p["sl"]]
        qk = jnp.concatenate([p["q_ref"][0, :, p["sl"]], p["k"]], axis=0)
        p["s_old"] = s_ref[p["bl"]]
        p["qks"] = _dot(qk, p["s_old"].astype(BF16))
    for p in probs:
        p["eg"] = jnp.exp(p["gc"])
        x = p["v_ref"][0, :, p["sl"]].astype(F32) - p["eg"] * p["qks"][CHUNK:]
        p["ta"] = p["ta_ref"][0, 0, 0, p["h"]]
        p["v_new"] = _dot(p["ta"][:CHUNK], x.astype(BF16)).astype(BF16)
    for p in probs:
        kd = p["k"].astype(F32) * jnp.exp(p["g_tot"] - p["gc"])
        s_ref[p["bl"]] = p["s_old"] * jnp.exp(p["g_tot"]) + _dot_tn(kd.astype(BF16), p["v_new"])
    for p in probs:
        o = (scale * p["eg"]) * p["qks"][:CHUNK] + _dot(p["ta"][CHUNK:], p["v_new"])
        p["o_ref"][0, :, p["sl"]] = o.astype(BF16)


def _gdn_scan(qkv, gd, ta, n_ctx_chunks):
    b, tot, _ = qkv.shape
    n_chunks = tot // CHUNK
    w = GDN_HEADS * GDN_DIM
    bwd = functools.partial(_bwd_chunk, n_ctx_chunks=n_ctx_chunks, n_chunks=n_chunks)

    def tok(col, rev):
        if rev:
            return pl.BlockSpec((1, CHUNK, w), lambda i, n: (i, bwd(n), col))
        return pl.BlockSpec((1, CHUNK, w), lambda i, n: (i, n, col))

    gcol_f = pl.BlockSpec((1, CHUNK, LANE), lambda i, n: (i, n, 0))
    gcol_b = pl.BlockSpec((1, CHUNK, LANE), lambda i, n: (i, bwd(n), 0))
    ta_blk = (1, 1, 1, GDN_HEADS, 2 * CHUNK, CHUNK)
    ta_f = pl.BlockSpec(ta_blk, lambda i, n: (i, n, 0, 0, 0, 0))
    ta_b = pl.BlockSpec(ta_blk, lambda i, n: (i, bwd(n), 1, 0, 0, 0))
    out = jax.ShapeDtypeStruct((b, tot, w), BF16)
    return pl.pallas_call(
        _gdn_kernel,
        out_shape=(out, out),
        grid=(b, n_chunks),
        in_specs=[tok(0, False), tok(1, False), tok(2, False), tok(0, True), tok(1, True), tok(2, True),
                  gcol_f, gcol_b, ta_f, ta_b],
        out_specs=(tok(0, False), tok(0, True)),
        scratch_shapes=[pltpu.VMEM((2 * GDN_HEADS, GDN_DIM, GDN_DIM), F32)],
        compiler_params=pltpu.CompilerParams(dimension_semantics=("arbitrary", "arbitrary"),
                                             vmem_limit_bytes=VMEM_LIMIT),
        name="gdn",
    )(qkv, qkv, qkv, qkv, qkv, qkv, gd, gd, ta, ta)


def _gla_kernel(qkf, vf, qkb, vb, lrf, lrb, wf, wb, bf, bb, of, ob, s_ref):
    @pl.when(pl.program_id(1) == 0)
    def _():
        s_ref[...] = jnp.zeros(s_ref.shape, F32)

    scale = GLA_DK ** -0.5
    kw = GLA_HEADS * GLA_DK
    lower, upper, _ = _tri_masks()
    dirs = ((qkf, vf, lrf, wf, bf, of, lower, CHUNK // 2, CHUNK - 1),
            (qkb, vb, lrb, wb, bb, ob, upper, CHUNK - 1 - CHUNK // 2, 0))
    probs = []
    for d, (qk_ref, v_ref, lr_ref, w_ref, b_ref, o_ref, incl, centre, last) in enumerate(dirs):
        lo = 4 * GDN_HEADS + d * GLA_RANK
        z = _dot(lr_ref[0, :, lo:lo + GLA_RANK], w_ref[...], precision=HIGHEST) + b_ref[...]
        log_a = -_softplus(-z) * (1.0 / GLA_NORMALIZER)
        cum = _cumsum_rows(log_a, reverse=(d == 1))
        for h in range(GLA_HEADS):
            gc = cum[:, h * GLA_DK:(h + 1) * GLA_DK]
            probs.append(dict(idx=GLA_HEADS * d + h, gc=gc, incl=incl, o_ref=o_ref,
                              g_mid=gc[centre:centre + 1, :], g_tot=gc[last:last + 1, :],
                              q=qk_ref[0, :, h * GLA_DK:(h + 1) * GLA_DK].astype(F32) * scale,
                              k=qk_ref[0, :, kw + h * GLA_DK:kw + (h + 1) * GLA_DK].astype(F32),
                              v=v_ref[0, :, h * GLA_DV:(h + 1) * GLA_DV],
                              vsl=slice(h * GLA_DV, (h + 1) * GLA_DV)))
    for p in probs:
        attn = _dot_nt((p["q"] * jnp.exp(p["gc"] - p["g_mid"])).astype(BF16),
                       (p["k"] * jnp.exp(p["g_mid"] - p["gc"])).astype(BF16))
        p["attn"] = jnp.where(p["incl"], attn, 0.0).astype(BF16)
        p["s_old"] = s_ref[p["idx"]]
    for p in probs:
        o = _dot(p["attn"], p["v"]) + _dot_nt((p["q"] * jnp.exp(p["gc"])).astype(BF16), p["s_old"].astype(BF16))
        p["o_ref"][0, :, p["vsl"]] = o.astype(BF16)
    for p in probs:
        kd = p["k"] * jnp.exp(p["g_tot"] - p["gc"])
        s_ref[p["idx"]] = p["s_old"] * jnp.exp(p["g_tot"]) + _dot_tn(p["v"], kd.astype(BF16))


def _gla_scan(p, lr, wf, wb, bf, bb, n_ctx_chunks, qk_col, v_col):
    b, tot, _ = p.shape
    n_chunks = tot // CHUNK
    w = GLA_HEADS * GLA_DV
    bwd = functools.partial(_bwd_chunk, n_ctx_chunks=n_ctx_chunks, n_chunks=n_chunks)

    def tok(width, col, rev):
        if rev:
            return pl.BlockSpec((1, CHUNK, width), lambda i, n: (i, bwd(n), col))
        return pl.BlockSpec((1, CHUNK, width), lambda i, n: (i, n, col))

    const2 = lambda i, n: (0, 0)
    out = jax.ShapeDtypeStruct((b, tot, w), BF16)
    return pl.pallas_call(
        _gla_kernel,
        out_shape=(out, out),
        grid=(b, n_chunks),
        in_specs=[tok(w, qk_col, False), tok(w, v_col, False), tok(w, qk_col, True), tok(w, v_col, True),
                  tok(LANE, 0, False), tok(LANE, 0, True),
                  pl.BlockSpec(wf.shape, const2), pl.BlockSpec(wb.shape, const2),
                  pl.BlockSpec(bf.shape, const2), pl.BlockSpec(bb.shape, const2)],
        out_specs=(tok(w, 0, False), tok(w, 0, True)),
        scratch_shapes=[pltpu.VMEM((2 * GLA_HEADS, GLA_DV, GLA_DK), F32)],
        compiler_params=pltpu.CompilerParams(dimension_semantics=("arbitrary", "arbitrary"),
                                             vmem_limit_bytes=VMEM_LIMIT),
        name="gla",
    )(p, p, p, p, lr, lr, wf, wb, bf, bb)


def _head_rms(o, g, width):
    outs = []
    for h in range(o.shape[-1] // width):
        oh = o[:, h * width:(h + 1) * width]
        ms = jnp.mean(oh * oh, axis=-1, keepdims=True)
        outs.append(oh * lax.rsqrt(ms + EPS))
    return jnp.concatenate(outs, axis=-1) * g


def _merge_kernel(oaf, oab, obf, obb, z_ref, r_ref, ga_ref, gb_ref, x_ref, g1_ref, sh2_ref, sc2_ref,
                  na_ref, nb_ref, nf_ref, woa_ref, wob_ref, wo_ref, wr_ref, br_ref,
                  x1_ref, h2_ref, lg_ref):
    f32 = lambda ref: ref[0].astype(F32)
    ya_in = _head_rms(f32(oaf) + f32(oab), na_ref[...], GDN_DIM) * _silu(f32(z_ref))
    ya = _dot(ya_in.astype(BF16), woa_ref[...])
    yb_in = _head_rms(f32(obf) + f32(obb), nb_ref[...], GLA_DV) * _silu(f32(r_ref))
    yb = _dot(yb_in.astype(BF16), wob_ref[...])
    y = jax.nn.sigmoid(f32(ga_ref)) * ya + jax.nn.sigmoid(f32(gb_ref)) * yb
    x1 = x_ref[0] + g1_ref[0] * _dot(y.astype(BF16), wo_ref[...])
    x1_ref[0] = x1
    ms = jnp.mean(x1 * x1, axis=-1, keepdims=True)
    h2 = (x1 * lax.rsqrt(ms + EPS) * nf_ref[...]) * (1.0 + sc2_ref[0]) + sh2_ref[0]
    _store_tile_rows(h2_ref, h2)
    lg_ref[0] = _dot(h2, wr_ref[...], precision=HIGHEST) + br_ref[...]


def _merge(oaf, oab, obf, obb, p, x, g1, sh2, sc2, na, nb, nf, woa, wob, wo, wr, br, cols):
    b, l, d = x.shape
    z_col, r_col, ga_col, gb_col = cols
    lat = lambda i, s: (i, s + 1, 0)
    row3 = lambda i, s: (i, 0, 0)
    const2 = lambda i, s: (0, 0)
    blk = lambda i, s: (i, s, 0)
    tb = (1, TOK_BLOCK, d)
    pcol = lambda c: pl.BlockSpec(tb, lambda i, s: (i, s + 1, c))
    vec = pl.BlockSpec((1, 1, d), row3)
    cvec = pl.BlockSpec((1, d), const2)
    wspec = pl.BlockSpec((d, d), const2)
    groups = d // LANE
    n_blk = l // TOK_BLOCK
    return pl.pallas_call(
        _merge_kernel,
        out_shape=(jax.ShapeDtypeStruct((b, l, d), F32),
                   jax.ShapeDtypeStruct((b * l * groups, LANE), F32),
                   jax.ShapeDtypeStruct((b, l, LANE), F32)),
        grid=(b, n_blk),
        in_specs=[pl.BlockSpec(tb, lat)] * 4 + [pcol(z_col), pcol(r_col), pcol(ga_col), pcol(gb_col),
                  pl.BlockSpec(tb, blk), vec, vec, vec, cvec, cvec, cvec, wspec, wspec, wspec,
                  pl.BlockSpec((d, LANE), const2), pl.BlockSpec((1, LANE), const2)],
        out_specs=(pl.BlockSpec(tb, blk),
                   pl.BlockSpec((TOK_BLOCK * groups, LANE), lambda i, s: (i * n_blk + s, 0)),
                   pl.BlockSpec((1, TOK_BLOCK, LANE), blk)),
        compiler_params=pltpu.CompilerParams(dimension_semantics=("arbitrary", "arbitrary"),
                                             vmem_limit_bytes=VMEM_LIMIT),
        name="merge",
    )(oaf, oab, obf, obb, p, p, p, p, x, g1, sh2, sc2, na, nb, nf, woa, wob, wo, wr, br)


def _row_copy(src_ref, dst_ref, sem, src_row, dst_row):
    first = lambda row: row * SUBLANE if isinstance(row, int) else pl.multiple_of(row * SUBLANE, SUBLANE)
    src, dst = first(src_row), first(dst_row)
    return pltpu.make_async_copy(src_ref.at[pl.ds(src, SUBLANE)], dst_ref.at[pl.ds(dst, SUBLANE)], sem)


def _gather_step(step, n_steps, rows, phases, two_queues, idx_hbm, src_hbm, idx_smem, buf, idx_sem, row_sem):
    part = rows // phases

    def idx_copy(s):
        return pltpu.make_async_copy(idx_hbm.at[s], idx_smem.at[s % 2], idx_sem.at[s % 2])

    def issue(s, lo, hi):
        slot = s % 2
        for r in range(lo, hi):
            _row_copy(src_hbm, buf.at[slot], row_sem.at[slot], idx_smem[slot, r], r).start(
                priority=r % 2 if two_queues else 0)

    @pl.when(step == 0)
    def _():
        idx_copy(0).start()
        idx_copy(0).wait()

        def first_rows(r, carry):
            _row_copy(src_hbm, buf.at[0], row_sem.at[0], idx_smem[0, r], r).start()
            return carry

        lax.fori_loop(0, rows, first_rows, 0)

        @pl.when(n_steps > 1)
        def _():
            idx_copy(1).start()

    @pl.when(step + 1 < n_steps)
    def _():
        idx_copy(step + 1).wait()

    def prefetch(j):
        @pl.when(step + 1 < n_steps)
        def _():
            issue(step + 1, j * part, (j + 1) * part)

        if j == phases - 1:
            @pl.when(step + 2 < n_steps)
            def _():
                idx_copy(step + 2).start()

    slot = step % 2
    pltpu.make_async_copy(src_hbm.at[pl.ds(0, rows * SUBLANE)], buf.at[slot], row_sem.at[slot]).wait()
    return buf.at[slot], prefetch


def _gather_scratch(rows):
    return [pltpu.SMEM((2, rows), jnp.int32), pltpu.VMEM((2, rows * SUBLANE, LANE), F32),
            pltpu.SemaphoreType.DMA((2,)), pltpu.SemaphoreType.DMA((2,))]


def _expert_kernel(be_ref, tok_ref, h_ref, wgu_ref, bgu_ref, wd_ref, bd_ref, y_ref, *gather_scratch):
    del be_ref
    de = wd_ref.shape[1]
    groups = wgu_ref.shape[1] // LANE
    x_ref, prefetch = _gather_step(pl.program_id(0), pl.num_programs(0), EXPERT_BLOCK, 2, False,
                                   tok_ref, h_ref, *gather_scratch)
    prefetch(0)
    x = _load_tile_rows(x_ref, EXPERT_BLOCK, groups)
    gu = _dot(x.astype(BF16), wgu_ref[0]) + bgu_ref[0]
    glu = jnp.minimum(gu[:, :de], SWIGLU_LIMIT)
    lin = jnp.clip(gu[:, de:], -SWIGLU_LIMIT, SWIGLU_LIMIT)
    act = (glu * jax.nn.sigmoid(SWIGLU_ALPHA * glu) * (lin + 1.0)).astype(BF16)
    prefetch(1)
    _store_tile_rows(y_ref, _dot(act, wd_ref[0]) + bd_ref[0])


def _experts(h, slot_tok, block_expert, w_gu, b_gu, w_down, b_down):
    ne, d, de2 = w_gu.shape
    de = de2 // 2
    groups = d // LANE
    assert groups == SUBLANE
    n_slots = slot_tok.shape[0]
    blk_rows = EXPERT_BLOCK * groups
    grid_spec = pltpu.PrefetchScalarGridSpec(
        num_scalar_prefetch=1,
        grid=(n_slots // EXPERT_BLOCK,),
        in_specs=[pl.BlockSpec(memory_space=pl.ANY), pl.BlockSpec(memory_space=pl.ANY),
                  pl.BlockSpec((1, d, de2), lambda i, be: (be[i], 0, 0)),
                  pl.BlockSpec((1, 1, de2), lambda i, be: (be[i], 0, 0)),
                  pl.BlockSpec((1, de, d), lambda i, be: (be[i], 0, 0)),
                  pl.BlockSpec((1, 1, d), lambda i, be: (be[i], 0, 0))],
        out_specs=pl.BlockSpec((blk_rows, LANE), lambda i, be: (i, 0)),
        scratch_shapes=_gather_scratch(EXPERT_BLOCK),
    )
    return pl.pallas_call(
        _expert_kernel,
        out_shape=jax.ShapeDtypeStruct((n_slots * groups, LANE), F32),
        grid_spec=grid_spec,
        compiler_params=pltpu.CompilerParams(dimension_semantics=("arbitrary",), vmem_limit_bytes=VMEM_LIMIT),
        name="experts",
    )(block_expert, slot_tok.reshape(n_slots // EXPERT_BLOCK, EXPERT_BLOCK), h,
      w_gu, b_gu.reshape(ne, 1, de2), w_down, b_down.reshape(ne, 1, d))


def _combine_kernel(dest_ref, yb_ref, x1_ref, p_ref, g2_ref, fg_ref, o_ref, *gather_scratch):
    groups = x1_ref.shape[-1] // LANE
    step = pl.program_id(0) * pl.num_programs(1) + pl.program_id(1)
    n_steps = pl.num_programs(0) * pl.num_programs(1)
    y4_ref, prefetch = _gather_step(step, n_steps, TOK_BLOCK * TOP_K, 1, True, dest_ref, yb_ref, *gather_scratch)
    prefetch(0)
    p = p_ref[0]
    y = None
    for k in range(TOP_K):
        yk = p[:, k:k + 1] * _load_tile_rows(y4_ref, TOK_BLOCK, groups, first=k * groups, stride=TOP_K * groups)
        y = yk if y is None else y + yk
    x2 = x1_ref[0] + g2_ref[0] * y
    ms = jnp.mean(x2 * x2, axis=-1, keepdims=True)
    o_ref[0] = x2 * lax.rsqrt(ms + EPS) * fg_ref[...]


def _combine(x1, yb, dest, top_p, g2, final_g):
    b, l, d = x1.shape
    assert d // LANE == SUBLANE
    blk = lambda i, s: (i, s, 0)
    n_blk = l // TOK_BLOCK
    rows = TOK_BLOCK * TOP_K
    return pl.pallas_call(
        _combine_kernel,
        out_shape=jax.ShapeDtypeStruct((b, l, d), F32),
        grid=(b, n_blk),
        in_specs=[pl.BlockSpec(memory_space=pl.ANY), pl.BlockSpec(memory_space=pl.ANY),
                  pl.BlockSpec((1, TOK_BLOCK, d), blk),
                  pl.BlockSpec((1, TOK_BLOCK, TOP_K), blk),
                  pl.BlockSpec((1, 1, d), lambda i, s: (i, 0, 0)),
                  pl.BlockSpec((1, d), lambda i, s: (0, 0))],
        out_specs=pl.BlockSpec((1, TOK_BLOCK, d), blk),
        scratch_shapes=_gather_scratch(rows),
        compiler_params=pltpu.CompilerParams(dimension_semantics=("arbitrary", "arbitrary"),
                                             vmem_limit_bytes=VMEM_LIMIT),
        name="combine",
    )(dest.reshape(b * n_blk, rows), yb, x1, top_p, g2, final_g)


def _routing(logits):
    t = logits.shape[0]
    n_assign = t * TOP_K
    n_slots = -(-(n_assign + N_EXPERTS * (EXPERT_BLOCK - 1)) // EXPERT_BLOCK) * EXPERT_BLOCK
    top_logit, top_idx = lax.top_k(logits, TOP_K)
    top_p = jax.nn.softmax(top_logit, axis=-1)
    flat_e = top_idx.reshape(-1).astype(jnp.int32)
    experts = jnp.arange(N_EXPERTS, dtype=jnp.int32)
    lookup = lambda table, e: jnp.sum(jnp.where(e[:, None] == experts[None, :], table[None, :], 0), axis=1)
    pos = jnp.arange(n_assign, dtype=jnp.int32)
    sorted_e, order = lax.sort((flat_e, pos), num_keys=1)
    counts = jnp.sum((flat_e[:, None] == experts[None, :]).astype(jnp.int32), axis=0)
    start = jnp.cumsum(counts) - counts
    padded = (counts + EXPERT_BLOCK - 1) // EXPERT_BLOCK * EXPERT_BLOCK
    pad_end = jnp.cumsum(padded)
    pad_start = pad_end - padded
    dest_sorted = pos + lookup(pad_start - start, sorted_e)
    _, dest = lax.sort((order, dest_sorted), num_keys=1)
    block_start = jnp.arange(n_slots // EXPERT_BLOCK, dtype=jnp.int32) * EXPERT_BLOCK
    block_expert = jnp.minimum(jnp.sum((pad_end[None, :] <= block_start[:, None]).astype(jnp.int32), axis=1),
                               N_EXPERTS - 1)
    slot_e = jnp.repeat(block_expert, EXPERT_BLOCK)
    within = jnp.arange(n_slots, dtype=jnp.int32) - lookup(pad_start, slot_e)
    src = jnp.clip(lookup(start, slot_e) + within, 0, n_assign - 1)
    slot_tok = jnp.where(within < lookup(counts, slot_e), order[src] // TOP_K, 0)
    return slot_tok, dest, block_expert, top_p


def kernel(x, c, ctx, c_ctx, w_mod, b_mod, norm_mix_g, norm_ffn_g, w_in, conv_w, a_log_f, a_log_b, dt_bias_f,
           dt_bias_b, gdn_norm_g, gla_gate_w_f, gla_gate_w_b, gla_gate_b_f, gla_gate_b_b, gla_norm_g, w_out_a,
           w_out_b, w_out, w_router, b_router, w_gu, b_gu, w_down, b_down, final_norm_g):
    assert w_mod.shape[0] == 1, "single-layer kernel"
    b, l, d = x.shape
    n_ctx = ctx.shape[1]
    assert n_ctx == TOK_BLOCK and l % CONV_ROWS == 0 and d == GDN_HEADS * GDN_DIM
    gw = GDN_HEADS * GDN_DIM
    kw = GLA_HEADS * GLA_DK
    vw = GLA_HEADS * GLA_DV

    pad_rows = -(b + 1) % 8
    cc = jnp.concatenate([c, c_ctx[None], jnp.zeros((pad_rows, d), F32)], axis=0)
    mod = _modulation(cc, w_mod[0], b_mod[0])
    sh1, sc1, g1, sh2, sc2, g2 = [mod[:b, i * d:(i + 1) * d].reshape(b, 1, d) for i in range(6)]
    sh1c, sc1c = mod[b:b + 1, 0:d], mod[b:b + 1, d:2 * d]

    offs = {}
    o = 0
    for name, width in (("qa", gw), ("ka", gw), ("va", gw), ("za", gw), ("beta", 2 * GDN_HEADS),
                        ("dec", 2 * GDN_HEADS), ("qb", kw), ("kb", kw), ("vb", vw), ("rb", vw),
                        ("lr", 2 * GLA_RANK), ("gates", 2 * d)):
        offs[name] = (o, o + width)
        o += width
    wi = w_in[0]
    cols = lambda n: wi[:, offs[n][0]:offs[n][1]]
    w_main = jnp.concatenate([cols(n) for n in ("qa", "ka", "va", "za", "qb", "kb", "vb", "rb", "gates")],
                             axis=1).astype(BF16)
    zpad = lambda n: jnp.zeros((d, n), F32)
    w_small = jnp.concatenate([cols("beta"), cols("dec"), cols("lr"),
                               zpad(LANE - 4 * GDN_HEADS - 2 * GLA_RANK)], axis=1)
    w_small_hi = w_small.astype(BF16)
    w_small = jnp.stack([w_small_hi, (w_small - w_small_hi.astype(F32)).astype(BF16)])
    lane_pad = lambda v, lo: jnp.zeros((1, LANE), F32).at[0, lo:lo + v.shape[0]].set(v)
    coef_a = lane_pad(-jnp.exp(jnp.concatenate([a_log_f[0], a_log_b[0]])), 2 * GDN_HEADS)
    coef_b = lane_pad(jnp.concatenate([dt_bias_f[0], dt_bias_b[0]]), 2 * GDN_HEADS)

    p, gd = _input_projection(x, ctx, sh1, sc1, sh1c, sc1c, norm_mix_g, w_main, w_small, coef_a, coef_b)
    tot = p.shape[1]
    n_chunks = tot // CHUNK
    n_ctx_chunks = n_ctx // CHUNK

    qkv = _conv_qkv(p, conv_w[0].reshape(9, 3 * gw), n_ctx, l)
    gd_rows = gd[:, :, :4 * GDN_HEADS].reshape(b, n_chunks, CHUNK, 4 * GDN_HEADS).transpose(0, 1, 3, 2)
    gd_rows = jnp.pad(gd_rows, ((0, 0), (0, 0), (0, 0), (0, LANE - CHUNK)))
    oaf, oab = _gdn_scan(qkv, gd, _gdn_prep(qkv, gd, gd_rows), n_ctx_chunks)

    obf, obb = _gla_scan(p, gd, gla_gate_w_f[0], gla_gate_w_b[0], gla_gate_b_f, gla_gate_b_b, n_ctx_chunks,
                         qk_col=4, v_col=5)

    wr = jnp.zeros((d, LANE), F32).at[:, :N_EXPERTS].set(w_router[0])
    br = jnp.zeros((1, LANE), F32).at[0, :N_EXPERTS].set(b_router[0])
    x1, h2, logits = _merge(oaf, oab, obf, obb, p, x, g1, sh2, sc2,
                            jnp.tile(gdn_norm_g, (1, GDN_HEADS)), jnp.tile(gla_norm_g, (1, GLA_HEADS)), norm_ffn_g,
                            w_out_a[0].astype(BF16), w_out_b[0].astype(BF16), w_out[0].astype(BF16), wr, br,
                            cols=(3, 6, 7, 8))

    return _channel_mixing(x1, h2, logits, g2, w_gu, b_gu, w_down, b_down, final_norm_g)


def _channel_mixing(x1, h2, logits, g2, w_gu, b_gu, w_down, b_down, final_norm_g):
    b, l, d = x1.shape
    t = b * l
    slot_tok, dest, block_expert, top_p = _routing(logits.reshape(t, LANE)[:, :N_EXPERTS])
    yb = _experts(h2, slot_tok, block_expert, w_gu[0].astype(BF16), b_gu[0], w_down[0].astype(BF16), b_down[0])
    return _combine(x1, yb, dest, top_p.reshape(b, l, TOP_K), g2, final_norm_g.reshape(1, d))
```

```python
import functools

import jax
import jax.numpy as jnp
from jax import lax
from jax.experimental import pallas as pl
from jax.experimental.pallas import tpu as pltpu

F32 = jnp.float32
BF16 = jnp.bfloat16
HIGHEST = lax.Precision.HIGHEST

EPS = 1e-6
GRID_W = 64
CHUNK = 64
SCAN_CHUNKS = 4
TOK_BLOCK = 256
CONV_ROWS = 512
GDN_HEADS = 8
GDN_DIM = 128
GLA_HEADS = 4
GLA_DK = 128
GLA_DV = 256
GLA_RANK = 16
GLA_NORMALIZER = 16.0
N_EXPERTS = 32
TOP_K = 4
SWIGLU_LIMIT = 7.0
SWIGLU_ALPHA = 1.702
EXPERT_BLOCK = 512
LANE = 128
SUBLANE = 8
VMEM_LIMIT = 56 * 1024 * 1024


def _silu(t):
    return t * jax.nn.sigmoid(t)


def _softplus(t):
    return jnp.maximum(t, 0.0) + jnp.log1p(jnp.exp(-jnp.abs(t)))


def _dot(a, b, **kw):
    return jnp.dot(a, b, preferred_element_type=F32, **kw)


def _dot_nt(a, b, **kw):
    return lax.dot_general(a, b, (((1,), (1,)), ((), ())), preferred_element_type=F32, **kw)


def _dot_tn(a, b, **kw):
    return lax.dot_general(a, b, (((0,), (0,)), ((), ())), preferred_element_type=F32, **kw)


def _store_tile_rows(ref, val):
    rows, width = val.shape
    groups = width // LANE
    for s in range(groups):
        ref[pl.ds(s, rows, stride=groups), :] = val[:, s * LANE:(s + 1) * LANE]


def _load_tile_rows(ref, rows, groups, first=0, stride=None):
    stride = groups if stride is None else stride
    return jnp.concatenate([ref[pl.ds(first + s, rows, stride=stride), :] for s in range(groups)], axis=-1)


def _mod_kernel(c_ref, w_ref, b_ref, o_ref):
    o_ref[...] = _dot(_silu(c_ref[...]), w_ref[...], precision=HIGHEST) + b_ref[...]


def _modulation(cc, w_mod, b_mod):
    rows, d = cc.shape
    n = w_mod.shape[1]
    tn = n // 4
    return pl.pallas_call(
        _mod_kernel,
        out_shape=jax.ShapeDtypeStruct((rows, n), F32),
        grid=(n // tn,),
        in_specs=[pl.BlockSpec((rows, d), lambda j: (0, 0)),
                  pl.BlockSpec((d, tn), lambda j: (0, j)),
                  pl.BlockSpec((1, tn), lambda j: (0, j))],
        out_specs=pl.BlockSpec((rows, tn), lambda j: (0, j)),
        compiler_params=pltpu.CompilerParams(vmem_limit_bytes=VMEM_LIMIT),
        name="mod",
    )(cc, w_mod, b_mod.reshape(1, n))


def _inproj_kernel(x_ref, ctx_ref, shl_ref, scl_ref, shc_ref, scc_ref, g_ref, w_ref, ws_ref, ca_ref, cb_ref,
                   p_ref, gd_ref):
    is_ctx = pl.program_id(1) == 0
    xin = jnp.where(is_ctx, ctx_ref[0], x_ref[0])
    shift = jnp.where(is_ctx, shc_ref[...], shl_ref[0])
    scale = jnp.where(is_ctx, scc_ref[...], scl_ref[0])
    ms = jnp.mean(xin * xin, axis=-1, keepdims=True)
    h = (xin * lax.rsqrt(ms + EPS) * g_ref[...]) * (1.0 + scale) + shift
    hb = h.astype(BF16)
    n_main = w_ref.shape[1]
    for j in range(n_main // 1024):
        p_ref[0, :, j * 1024:(j + 1) * 1024] = _dot(hb, w_ref[:, j * 1024:(j + 1) * 1024]).astype(p_ref.dtype)
    h_lo = (h - hb.astype(F32)).astype(BF16)
    raw = _dot(hb, ws_ref[0]) + (_dot(h_lo, ws_ref[0]) + _dot(hb, ws_ref[1]))
    lane = lax.broadcasted_iota(jnp.int32, raw.shape, 1)
    beta = jax.nn.sigmoid(raw)
    g = ca_ref[...] * _softplus(raw + cb_ref[...])
    gd_ref[0] = jnp.where(lane < 2 * GDN_HEADS, beta, jnp.where(lane < 4 * GDN_HEADS, g, raw))


def _input_projection(x, ctx, sh_lat, sc_lat, sh_ctx, sc_ctx, norm_g, w_main, w_small, coef_a, coef_b):
    b, l, d = x.shape
    n_blk = l // TOK_BLOCK + 1
    n_main = w_main.shape[1]
    tot = n_blk * TOK_BLOCK
    row3 = lambda i, s: (i, 0, 0)
    const2 = lambda i, s: (0, 0)
    blk = lambda i, s: (i, s, 0)
    return pl.pallas_call(
        _inproj_kernel,
        out_shape=(jax.ShapeDtypeStruct((b, tot, n_main), BF16),
                   jax.ShapeDtypeStruct((b, tot, LANE), F32)),
        grid=(b, n_blk),
        in_specs=[pl.BlockSpec((1, TOK_BLOCK, d), lambda i, s: (i, jnp.maximum(s - 1, 0), 0)),
                  pl.BlockSpec((1, TOK_BLOCK, d), row3),
                  pl.BlockSpec((1, 1, d), row3), pl.BlockSpec((1, 1, d), row3),
                  pl.BlockSpec((1, d), const2), pl.BlockSpec((1, d), const2),
                  pl.BlockSpec((1, d), const2),
                  pl.BlockSpec((d, n_main), const2, pipeline_mode=pl.Buffered(1)),
                  pl.BlockSpec((2, d, LANE), lambda i, s: (0, 0, 0)),
                  pl.BlockSpec((1, LANE), const2), pl.BlockSpec((1, LANE), const2)],
        out_specs=(pl.BlockSpec((1, TOK_BLOCK, n_main), blk),
                   pl.BlockSpec((1, TOK_BLOCK, LANE), blk)),
        compiler_params=pltpu.CompilerParams(dimension_semantics=("arbitrary", "arbitrary"),
                                             vmem_limit_bytes=VMEM_LIMIT),
        name="inproj",
    )(x, ctx, sh_lat, sc_lat, sh_ctx, sc_ctx, norm_g, w_main, w_small, coef_a, coef_b)


def _conv_kernel(p_ref, w_ref, o_ref, pad_ref, *, n_ctx, n_lat, tc, qk_cols):
    is_qk = pl.program_id(1) * tc < qk_cols
    w = w_ref[...]

    def post(y):
        s = _silu(y)
        outs = []
        for hh in range(tc // GDN_DIM):
            sh = s[:, hh * GDN_DIM:(hh + 1) * GDN_DIM]
            fac = lax.rsqrt(jnp.sum(sh * sh, axis=-1, keepdims=True) + EPS)
            outs.append(sh * jnp.where(is_qk, fac, 1.0))
        return outs[0] if len(outs) == 1 else jnp.concatenate(outs, axis=-1)

    xc = p_ref[0, 0:n_ctx, :].astype(F32)
    ridx = lax.broadcasted_iota(jnp.int32, (n_ctx, 1), 0)
    yc = (jnp.where(ridx == 0, 0.0, pltpu.roll(xc * w[3:4], 1, 0)) + xc * w[4:5]
          + jnp.where(ridx == n_ctx - 1, 0.0, pltpu.roll(xc * w[5:6], n_ctx - 1, 0)))
    o_ref[0, 0:n_ctx, :] = post(yc).astype(o_ref.dtype)

    zeros = jnp.zeros((GRID_W, tc), F32)
    pad_ref[0:GRID_W, :] = zeros
    pad_ref[GRID_W + n_lat:GRID_W + n_lat + GRID_W, :] = zeros
    pad_ref[GRID_W:GRID_W + n_lat, :] = p_ref[0, n_ctx:n_ctx + n_lat, :].astype(F32)
    rows = min(CONV_ROWS, n_lat)
    col = lax.broadcasted_iota(jnp.int32, (rows, 1), 0) % GRID_W

    def body(i, carry):
        s = pl.multiple_of(i * rows, rows)
        up = pad_ref[pl.ds(s, rows), :]
        mid = pad_ref[pl.ds(s + GRID_W, rows), :]
        dn = pad_ref[pl.ds(s + 2 * GRID_W, rows), :]
        a0 = up * w[0:1] + mid * w[3:4] + dn * w[6:7]
        a1 = up * w[1:2] + mid * w[4:5] + dn * w[7:8]
        a2 = up * w[2:3] + mid * w[5:6] + dn * w[8:9]
        y = (jnp.where(col == 0, 0.0, pltpu.roll(a0, 1, 0)) + a1
             + jnp.where(col == GRID_W - 1, 0.0, pltpu.roll(a2, rows - 1, 0)))
        o_ref[0, pl.ds(n_ctx + s, rows), :] = post(y).astype(o_ref.dtype)
        return carry

    lax.fori_loop(0, n_lat // rows, body, 0)


def _conv_qkv(p, conv_w9, n_ctx, n_lat):
    b, tot, _ = p.shape
    c = conv_w9.shape[1]
    tc = 2 * GDN_DIM
    kern = functools.partial(_conv_kernel, n_ctx=n_ctx, n_lat=n_lat, tc=tc, qk_cols=2 * GDN_HEADS * GDN_DIM)
    return pl.pallas_call(
        kern,
        out_shape=jax.ShapeDtypeStruct((b, tot, c), BF16),
        grid=(b, c // tc),
        in_specs=[pl.BlockSpec((1, tot, tc), lambda i, j: (i, 0, j)),
                  pl.BlockSpec((9, tc), lambda i, j: (0, j))],
        out_specs=pl.BlockSpec((1, tot, tc), lambda i, j: (i, 0, j)),
        scratch_shapes=[pltpu.VMEM((n_lat + 2 * GRID_W, tc), F32)],
        compiler_params=pltpu.CompilerParams(dimension_semantics=("arbitrary", "arbitrary"),
                                             vmem_limit_bytes=VMEM_LIMIT),
        name="conv",
    )(p, conv_w9)


def _bwd_chunk(n, n_ctx_chunks, n_chunks):
    return jnp.where(n < n_ctx_chunks, n_ctx_chunks - 1 - n, n_chunks + n_ctx_chunks - 1 - n)


def _cumsum_rows(x, reverse):
    n = x.shape[0]
    row = lax.broadcasted_iota(jnp.int32, (n, 1), 0)
    s = 1
    while s < n:
        if reverse:
            x = x + jnp.where(row < n - s, pltpu.roll(x, n - s, 0), 0.0)
        else:
            x = x + jnp.where(row >= s, pltpu.roll(x, s, 0), 0.0)
        s *= 2
    return x


def _tri_masks():
    r = lax.broadcasted_iota(jnp.int32, (CHUNK, CHUNK), 0)
    c = lax.broadcasted_iota(jnp.int32, (CHUNK, CHUNK), 1)
    return r >= c, r <= c, r == c


PREP_BATCH = 4
PREP_CHUNKS = TOK_BLOCK // CHUNK
SLAB = CHUNK + SUBLANE


def _substitute(n_scr, t_scr, reverse):
    n_seg = CHUNK // SUBLANE
    sub = lax.broadcasted_iota(jnp.int32, (SUBLANE, LANE), 0)
    for seg in (reversed(range(n_seg)) if reverse else range(n_seg)):
        js = range(SUBLANE * seg, CHUNK) if reverse else range(0, SUBLANE * (seg + 1))
        acc_groups = range(seg, n_seg) if reverse else range(0, seg + 1)
        t_scr[SUBLANE * seg * CHUNK:SUBLANE * (seg + 1) * CHUNK, :] = jnp.zeros((SUBLANE * CHUNK, LANE), F32)

        def body(step, carry, seg=seg, js=js, acc_groups=acc_groups):
            i = SUBLANE * seg + (SUBLANE - 1 - step if reverse else step)
            base = pl.multiple_of(i * CHUNK, CHUNK)
            acc = {g: jnp.zeros((SUBLANE, LANE), F32) for g in acc_groups}
            for j in js:
                nij = jnp.broadcast_to(n_scr[pl.ds(j * CHUNK + i, 1), :], (SUBLANE, LANE))
                for g in (range(j // SUBLANE, n_seg) if reverse else range(0, j // SUBLANE + 1)):
                    acc[g] = acc[g] + nij * t_scr[j * CHUNK + g * SUBLANE:j * CHUNK + (g + 1) * SUBLANE, :]
            acc[seg] = acc[seg] + jnp.where(sub + SUBLANE * seg == i, 1.0, 0.0)
            for g in acc_groups:
                t_scr[pl.ds(base + g * SUBLANE, SUBLANE), :] = acc[g]
            return carry

        lax.fori_loop(0, SUBLANE, body, 0)


def _gdn_prep_kernel(q_ref, k_ref, gc_ref, gr_ref, ta_ref, g_scr, n_scr, t_scr, r_scr):
    scale = GDN_DIM ** -0.5
    r = lax.broadcasted_iota(jnp.int32, (CHUNK, LANE), 0)
    c = lax.broadcasted_iota(jnp.int32, (CHUNK, LANE), 1)
    incl = (jnp.logical_and(r >= c, c < CHUNK), jnp.logical_and(r <= c, c < CHUNK))
    strict = (jnp.logical_and(r > c, c < CHUNK), jnp.logical_and(r < c, c < CHUNK))
    rl = lax.broadcasted_iota(jnp.int32, (LANE, LANE), 0)
    cl = lax.broadcasted_iota(jnp.int32, (LANE, LANE), 1)
    live = jnp.logical_and(rl < CHUNK, cl < CHUNK)
    tri_r = (jnp.logical_and(rl <= cl, live).astype(F32), jnp.logical_and(rl >= cl, live).astype(F32))
    k_zero = jnp.zeros((CHUNK, GDN_DIM), BF16)

    def per_chunk(fn):
        def outer(bb, carry):
            def inner(ch, carry2):
                fn(bb, ch, pl.multiple_of(ch * CHUNK, CHUNK), (bb * PREP_CHUNKS + ch) * GDN_HEADS)
                return carry2
            return lax.fori_loop(0, PREP_CHUNKS, inner, carry)
        lax.fori_loop(0, PREP_BATCH, outer, 0)

    def build(bb, ch, row0, p0):
        gcol = gc_ref[bb, pl.ds(row0, CHUNK), :]
        grow = gr_ref[bb, ch]
        cum_c = [_cumsum_rows(gcol, reverse=(d == 1)) for d in (0, 1)]
        cum_r = [_dot(grow, tri_r[d], precision=HIGHEST) for d in (0, 1)]
        for h in range(GDN_HEADS):
            sl = slice(h * GDN_DIM, (h + 1) * GDN_DIM)
            kb = k_ref[bb, pl.ds(row0, CHUNK), sl]
            qk = jnp.concatenate([q_ref[bb, pl.ds(row0, CHUNK), sl], kb], axis=0)
            gram = _dot_nt(qk, jnp.concatenate([kb, k_zero], axis=0))
            for d in (0, 1):
                bl = GDN_HEADS * d + h
                gl = 2 * GDN_HEADS + bl
                diff = cum_c[d][:, gl:gl + 1] - cum_r[d][gl:gl + 1, :]
                dec = jnp.exp(jnp.where(incl[d], diff, -jnp.inf))
                dec_t = jnp.exp(jnp.where(strict[1 - d], -diff, -jnp.inf))
                nm_t = -(gram[CHUNK:] * grow[bl:bl + 1, :] * dec_t)
                g_scr[d, pl.ds(pl.multiple_of((p0 + h) * SLAB, SUBLANE), CHUNK), :] = nm_t
                ta_ref[bb, ch, d, h, CHUNK:2 * CHUNK, :] = (gram[:CHUNK] * scale * dec)[:, :CHUNK].astype(BF16)

    per_chunk(build)

    for d in (0, 1):
        def to_lanes(i, carry, d=d):
            blk = g_scr[d, pl.ds(i, LANE, stride=SLAB), :]
            n_scr[pl.ds(pl.multiple_of(i * CHUNK, CHUNK), CHUNK), :] = blk.T[:CHUNK]
            return carry

        lax.fori_loop(0, CHUNK, to_lanes, 0, unroll=8)
        _substitute(n_scr, t_scr, reverse=(d == 1))

        def from_lanes(i, carry):
            blk = t_scr[pl.ds(pl.multiple_of(i * CHUNK, CHUNK), CHUNK), :]
            r_scr[pl.ds(i, LANE, stride=SLAB), :] = blk.T
            return carry

        lax.fori_loop(0, CHUNK, from_lanes, 0, unroll=8)

        def finish(bb, ch, row0, p0, d=d):
            grow = gr_ref[bb, ch]
            for h in range(GDN_HEADS):
                bl = GDN_HEADS * d + h
                t = r_scr[pl.ds(pl.multiple_of((p0 + h) * SLAB, SUBLANE), CHUNK), :]
                ta_ref[bb, ch, d, h, 0:CHUNK, :] = (t * grow[bl:bl + 1, :CHUNK]).astype(BF16)

        per_chunk(finish)


def _gdn_prep(qkv, gd, gd_rows):
    b, tot, _ = qkv.shape
    n_chunks = tot // CHUNK
    w = GDN_HEADS * GDN_DIM
    assert PREP_BATCH * PREP_CHUNKS * GDN_HEADS == LANE and b % PREP_BATCH == 0
    return pl.pallas_call(
        _gdn_prep_kernel,
        out_shape=jax.ShapeDtypeStruct((b, n_chunks, 2, GDN_HEADS, 2 * CHUNK, CHUNK), BF16),
        grid=(b // PREP_BATCH, tot // TOK_BLOCK),
        in_specs=[pl.BlockSpec((PREP_BATCH, TOK_BLOCK, w), lambda i, s: (i, s, 0)),
                  pl.BlockSpec((PREP_BATCH, TOK_BLOCK, w), lambda i, s: (i, s, 1)),
                  pl.BlockSpec((PREP_BATCH, TOK_BLOCK, LANE), lambda i, s: (i, s, 0)),
                  pl.BlockSpec((PREP_BATCH, PREP_CHUNKS, 4 * GDN_HEADS, LANE), lambda i, s: (i, s, 0, 0))],
        out_specs=pl.BlockSpec((PREP_BATCH, PREP_CHUNKS, 2, GDN_HEADS, 2 * CHUNK, CHUNK),
                               lambda i, s: (i, s, 0, 0, 0, 0)),
        scratch_shapes=[pltpu.VMEM((2, LANE * SLAB, LANE), F32),
                        pltpu.VMEM((CHUNK * CHUNK, LANE), F32),
                        pltpu.VMEM((CHUNK * CHUNK, LANE), F32),
                        pltpu.VMEM((LANE * SLAB, CHUNK), F32)],
        compiler_params=pltpu.CompilerParams(dimension_semantics=("arbitrary", "arbitrary"),
                                             vmem_limit_bytes=VMEM_LIMIT),
        name="gdn_prep",
    )(qkv, qkv, gd, gd_rows)


def _gdn_kernel(qf, kf, vf, qb, kb, vb, gcf, gcb, taf, tab, of, ob, s_ref):
    @pl.when(pl.program_id(1) == 0)
    def _():
        s_ref[...] = jnp.zeros(s_ref.shape, F32)

    scale = GDN_DIM ** -0.5
    dirs = ((qf, kf, vf, gcf, taf, of, CHUNK - 1), (qb, kb, vb, gcb, tab, ob, 0))
    for c in range(SCAN_CHUNKS):
        probs = []
        for d, (q_ref, k_ref, v_ref, gc_ref, ta_ref, o_ref, last) in enumerate(dirs):
            tc = c if d == 0 else SCAN_CHUNKS - 1 - c
            rows = slice(tc * CHUNK, (tc + 1) * CHUNK)
            cum_c = _cumsum_rows(gc_ref[0, rows, :], reverse=(d == 1))
            for h in range(GDN_HEADS):
                bl = GDN_HEADS * d + h
                gl = 2 * GDN_HEADS + bl
                sl = slice(h * GDN_DIM, (h + 1) * GDN_DIM)
                gc = cum_c[:, gl:gl + 1]
                probs.append(dict(bl=bl, sl=sl, gc=gc, g_tot=gc[last:last + 1, :], h=h, rows=rows, tc=tc,
                                  q_ref=q_ref, k_ref=k_ref, v_ref=v_ref, ta_ref=ta_ref, o_ref=o_ref))
        for p in probs:
            p["k"] = p["k_ref"][0, p["rows"], p["sl"]]
            qk = jnp.concatenate([p["q_ref"][0, p["rows"], p["sl"]], p["k"]], axis=0)
            p["s_old"] = s_ref[p["bl"]]
            p["qks"] = _dot(qk, p["s_old"].astype(BF16))
        for p in probs:
            p["eg"] = jnp.exp(p["gc"])
            x = p["v_ref"][0, p["rows"], p["sl"]].astype(F32) - p["eg"] * p["qks"][CHUNK:]
            p["ta"] = p["ta_ref"][0, p["tc"], 0, p["h"]]
            p["v_new"] = _dot(p["ta"][:CHUNK], x.astype(BF16)).astype(BF16)
        for p in probs:
            kd = p["k"].astype(F32) * jnp.exp(p["g_tot"] - p["gc"])
            s_ref[p["bl"]] = p["s_old"] * jnp.exp(p["g_tot"]) + _dot_tn(kd.astype(BF16), p["v_new"])
        for p in probs:
            o = (scale * p["eg"]) * p["qks"][:CHUNK] + _dot(p["ta"][CHUNK:], p["v_new"])
            p["o_ref"][0, p["rows"], p["sl"]] = o.astype(BF16)


def _gdn_scan(qkv, gd, ta, n_ctx_chunks):
    b, tot, _ = qkv.shape
    rows = SCAN_CHUNKS * CHUNK
    n_steps = tot // rows
    assert n_ctx_chunks % SCAN_CHUNKS == 0
    w = GDN_HEADS * GDN_DIM
    bwd = functools.partial(_bwd_chunk, n_ctx_chunks=n_ctx_chunks // SCAN_CHUNKS, n_chunks=n_steps)

    def tok(col, rev):
        if rev:
            return pl.BlockSpec((1, rows, w), lambda i, n: (i, bwd(n), col))
        return pl.BlockSpec((1, rows, w), lambda i, n: (i, n, col))

    gcol_f = pl.BlockSpec((1, rows, LANE), lambda i, n: (i, n, 0))
    gcol_b = pl.BlockSpec((1, rows, LANE), lambda i, n: (i, bwd(n), 0))
    ta_blk = (1, SCAN_CHUNKS, 1, GDN_HEADS, 2 * CHUNK, CHUNK)
    ta_f = pl.BlockSpec(ta_blk, lambda i, n: (i, n, 0, 0, 0, 0))
    ta_b = pl.BlockSpec(ta_blk, lambda i, n: (i, bwd(n), 1, 0, 0, 0))
    out = jax.ShapeDtypeStruct((b, tot, w), BF16)
    return pl.pallas_call(
        _gdn_kernel,
        out_shape=(out, out),
        grid=(b, n_steps),
        in_specs=[tok(0, False), tok(1, False), tok(2, False), tok(0, True), tok(1, True), tok(2, True),
                  gcol_f, gcol_b, ta_f, ta_b],
        out_specs=(tok(0, False), tok(0, True)),
        scratch_shapes=[pltpu.VMEM((2 * GDN_HEADS, GDN_DIM, GDN_DIM), F32)],
        compiler_params=pltpu.CompilerParams(dimension_semantics=("arbitrary", "arbitrary"),
                                             vmem_limit_bytes=VMEM_LIMIT),
        name="gdn",
    )(qkv, qkv, qkv, qkv, qkv, qkv, gd, gd, ta, ta)


def _gla_kernel(qkf, vf, qkb, vb, lrf, lrb, wf, wb, bf, bb, of, ob, s_ref):
    @pl.when(pl.program_id(1) == 0)
    def _():
        s_ref[...] = jnp.zeros(s_ref.shape, F32)

    scale = GLA_DK ** -0.5
    kw = GLA_HEADS * GLA_DK
    lower, upper, _ = _tri_masks()
    dirs = ((qkf, vf, lrf, wf, bf, of, lower, CHUNK // 2, CHUNK - 1),
            (qkb, vb, lrb, wb, bb, ob, upper, CHUNK - 1 - CHUNK // 2, 0))
    log_a = []
    for d, (_, _, lr_ref, w_ref, b_ref, _, _, _, _) in enumerate(dirs):
        lo = 4 * GDN_HEADS + d * GLA_RANK
        z = _dot(lr_ref[0, :, lo:lo + GLA_RANK], w_ref[...], precision=HIGHEST) + b_ref[...]
        log_a.append(-_softplus(-z) * (1.0 / GLA_NORMALIZER))
    for c in range(SCAN_CHUNKS):
        probs = []
        for d, (qk_ref, v_ref, _, _, _, o_ref, incl, centre, last) in enumerate(dirs):
            tc = c if d == 0 else SCAN_CHUNKS - 1 - c
            rows = slice(tc * CHUNK, (tc + 1) * CHUNK)
            cum = _cumsum_rows(log_a[d][rows], reverse=(d == 1))
            for h in range(GLA_HEADS):
                gc = cum[:, h * GLA_DK:(h + 1) * GLA_DK]
                probs.append(dict(idx=GLA_HEADS * d + h, gc=gc, incl=incl, o_ref=o_ref, rows=rows,
                                  g_mid=gc[centre:centre + 1, :], g_tot=gc[last:last + 1, :],
                                  q=qk_ref[0, rows, h * GLA_DK:(h + 1) * GLA_DK].astype(F32) * scale,
                                  k=qk_ref[0, rows, kw + h * GLA_DK:kw + (h + 1) * GLA_DK].astype(F32),
                                  v=v_ref[0, rows, h * GLA_DV:(h + 1) * GLA_DV],
                                  vsl=slice(h * GLA_DV, (h + 1) * GLA_DV)))
        for p in probs:
            attn = _dot_nt((p["q"] * jnp.exp(p["gc"] - p["g_mid"])).astype(BF16),
                           (p["k"] * jnp.exp(p["g_mid"] - p["gc"])).astype(BF16))
            p["attn"] = jnp.where(p["incl"], attn, 0.0).astype(BF16)
            p["s_old"] = s_ref[p["idx"]]
        for p in probs:
            o = (_dot(p["attn"], p["v"])
                 + _dot_nt((p["q"] * jnp.exp(p["gc"])).astype(BF16), p["s_old"].astype(BF16)))
            p["o_ref"][0, p["rows"], p["vsl"]] = o.astype(BF16)
        for p in probs:
            kd = p["k"] * jnp.exp(p["g_tot"] - p["gc"])
            s_ref[p["idx"]] = p["s_old"] * jnp.exp(p["g_tot"]) + _dot_tn(p["v"], kd.astype(BF16))


def _gla_scan(p, lr, wf, wb, bf, bb, n_ctx_chunks, qk_col, v_col):
    b, tot, _ = p.shape
    rows = SCAN_CHUNKS * CHUNK
    n_steps = tot // rows
    assert n_ctx_chunks % SCAN_CHUNKS == 0
    w = GLA_HEADS * GLA_DV
    bwd = functools.partial(_bwd_chunk, n_ctx_chunks=n_ctx_chunks // SCAN_CHUNKS, n_chunks=n_steps)

    def tok(width, col, rev):
        if rev:
            return pl.BlockSpec((1, rows, width), lambda i, n: (i, bwd(n), col))
        return pl.BlockSpec((1, rows, width), lambda i, n: (i, n, col))

    const2 = lambda i, n: (0, 0)
    out = jax.ShapeDtypeStruct((b, tot, w), BF16)
    return pl.pallas_call(
        _gla_kernel,
        out_shape=(out, out),
        grid=(b, n_steps),
        in_specs=[tok(w, qk_col, False), tok(w, v_col, False), tok(w, qk_col, True), tok(w, v_col, True),
                  tok(LANE, 0, False), tok(LANE, 0, True),
                  pl.BlockSpec(wf.shape, const2), pl.BlockSpec(wb.shape, const2),
                  pl.BlockSpec(bf.shape, const2), pl.BlockSpec(bb.shape, const2)],
        out_specs=(tok(w, 0, False), tok(w, 0, True)),
        scratch_shapes=[pltpu.VMEM((2 * GLA_HEADS, GLA_DV, GLA_DK), F32)],
        compiler_params=pltpu.CompilerParams(dimension_semantics=("arbitrary", "arbitrary"),
                                             vmem_limit_bytes=VMEM_LIMIT),
        name="gla",
    )(p, p, p, p, lr, lr, wf, wb, bf, bb)


def _head_rms(o, g, width):
    outs = []
    for h in range(o.shape[-1] // width):
        oh = o[:, h * width:(h + 1) * width]
        ms = jnp.mean(oh * oh, axis=-1, keepdims=True)
        outs.append(oh * lax.rsqrt(ms + EPS))
    return jnp.concatenate(outs, axis=-1) * g


def _merge_kernel(oaf, oab, obf, obb, z_ref, r_ref, ga_ref, gb_ref, x_ref, g1_ref, sh2_ref, sc2_ref,
                  na_ref, nb_ref, nf_ref, woa_ref, wob_ref, wo_ref, wr_ref, br_ref,
                  x1_ref, h2_ref, lg_ref):
    f32 = lambda ref: ref[0].astype(F32)
    ya_in = _head_rms(f32(oaf) + f32(oab), na_ref[...], GDN_DIM) * _silu(f32(z_ref))
    ya = _dot(ya_in.astype(BF16), woa_ref[...])
    yb_in = _head_rms(f32(obf) + f32(obb), nb_ref[...], GLA_DV) * _silu(f32(r_ref))
    yb = _dot(yb_in.astype(BF16), wob_ref[...])
    y = jax.nn.sigmoid(f32(ga_ref)) * ya + jax.nn.sigmoid(f32(gb_ref)) * yb
    x1 = x_ref[0] + g1_ref[0] * _dot(y.astype(BF16), wo_ref[...])
    x1_ref[0] = x1
    ms = jnp.mean(x1 * x1, axis=-1, keepdims=True)
    h2 = (x1 * lax.rsqrt(ms + EPS) * nf_ref[...]) * (1.0 + sc2_ref[0]) + sh2_ref[0]
    _store_tile_rows(h2_ref, h2)
    lg_ref[0] = _dot(h2, wr_ref[...], precision=HIGHEST) + br_ref[...]


def _merge(oaf, oab, obf, obb, p, x, g1, sh2, sc2, na, nb, nf, woa, wob, wo, wr, br, cols):
    b, l, d = x.shape
    z_col, r_col, ga_col, gb_col = cols
    lat = lambda i, s: (i, s + 1, 0)
    row3 = lambda i, s: (i, 0, 0)
    const2 = lambda i, s: (0, 0)
    blk = lambda i, s: (i, s, 0)
    tb = (1, TOK_BLOCK, d)
    pcol = lambda c: pl.BlockSpec(tb, lambda i, s: (i, s + 1, c))
    vec = pl.BlockSpec((1, 1, d), row3)
    cvec = pl.BlockSpec((1, d), const2)
    wspec = pl.BlockSpec((d, d), const2)
    groups = d // LANE
    n_blk = l // TOK_BLOCK
    return pl.pallas_call(
        _merge_kernel,
        out_shape=(jax.ShapeDtypeStruct((b, l, d), F32),
                   jax.ShapeDtypeStruct((b * l * groups, LANE), F32),
                   jax.ShapeDtypeStruct((b, l, LANE), F32)),
        grid=(b, n_blk),
        in_specs=[pl.BlockSpec(tb, lat)] * 4 + [pcol(z_col), pcol(r_col), pcol(ga_col), pcol(gb_col),
                  pl.BlockSpec(tb, blk), vec, vec, vec, cvec, cvec, cvec, wspec, wspec, wspec,
                  pl.BlockSpec((d, LANE), const2), pl.BlockSpec((1, LANE), const2)],
        out_specs=(pl.BlockSpec(tb, blk),
                   pl.BlockSpec((TOK_BLOCK * groups, LANE), lambda i, s: (i * n_blk + s, 0)),
                   pl.BlockSpec((1, TOK_BLOCK, LANE), blk)),
        compiler_params=pltpu.CompilerParams(dimension_semantics=("arbitrary", "arbitrary"),
                                             vmem_limit_bytes=VMEM_LIMIT),
        name="merge",
    )(oaf, oab, obf, obb, p, p, p, p, x, g1, sh2, sc2, na, nb, nf, woa, wob, wo, wr, br)


def _row_copy(src_ref, dst_ref, sem, src_row, dst_row):
    first = lambda row: row * SUBLANE if isinstance(row, int) else pl.multiple_of(row * SUBLANE, SUBLANE)
    src, dst = first(src_row), first(dst_row)
    return pltpu.make_async_copy(src_ref.at[pl.ds(src, SUBLANE)], dst_ref.at[pl.ds(dst, SUBLANE)], sem)


def _gather_step(step, n_steps, rows, phases, two_queues, idx_hbm, src_hbm, idx_smem, buf, idx_sem, row_sem):
    part = rows // phases

    def idx_copy(s):
        return pltpu.make_async_copy(idx_hbm.at[s], idx_smem.at[s % 2], idx_sem.at[s % 2])

    def issue(s, lo, hi):
        slot = s % 2
        for r in range(lo, hi):
            _row_copy(src_hbm, buf.at[slot], row_sem.at[slot], idx_smem[slot, r], r).start(
                priority=r % 2 if two_queues else 0)

    @pl.when(step == 0)
    def _():
        idx_copy(0).start()
        idx_copy(0).wait()

        def first_rows(r, carry):
            _row_copy(src_hbm, buf.at[0], row_sem.at[0], idx_smem[0, r], r).start()
            return carry

        lax.fori_loop(0, rows, first_rows, 0)

        @pl.when(n_steps > 1)
        def _():
            idx_copy(1).start()

    @pl.when(step + 1 < n_steps)
    def _():
        idx_copy(step + 1).wait()

    def prefetch(j):
        @pl.when(step + 1 < n_steps)
        def _():
            issue(step + 1, j * part, (j + 1) * part)

        if j == phases - 1:
            @pl.when(step + 2 < n_steps)
            def _():
                idx_copy(step + 2).start()

    slot = step % 2
    pltpu.make_async_copy(src_hbm.at[pl.ds(0, rows * SUBLANE)], buf.at[slot], row_sem.at[slot]).wait()
    return buf.at[slot], prefetch


def _gather_scratch(rows):
    return [pltpu.SMEM((2, rows), jnp.int32), pltpu.VMEM((2, rows * SUBLANE, LANE), F32),
            pltpu.SemaphoreType.DMA((2,)), pltpu.SemaphoreType.DMA((2,))]


def _expert_kernel(be_ref, tok_ref, h_ref, wgu_ref, bgu_ref, wd_ref, bd_ref, y_ref, *gather_scratch):
    del be_ref
    de = wd_ref.shape[1]
    groups = wgu_ref.shape[1] // LANE
    x_ref, prefetch = _gather_step(pl.program_id(0), pl.num_programs(0), EXPERT_BLOCK, 2, False,
                                   tok_ref, h_ref, *gather_scratch)
    prefetch(0)
    x = _load_tile_rows(x_ref, EXPERT_BLOCK, groups)
    gu = _dot(x.astype(BF16), wgu_ref[0]) + bgu_ref[0]
    glu = jnp.minimum(gu[:, :de], SWIGLU_LIMIT)
    lin = jnp.clip(gu[:, de:], -SWIGLU_LIMIT, SWIGLU_LIMIT)
    act = (glu * jax.nn.sigmoid(SWIGLU_ALPHA * glu) * (lin + 1.0)).astype(BF16)
    prefetch(1)
    _store_tile_rows(y_ref, _dot(act, wd_ref[0]) + bd_ref[0])


def _experts(h, slot_tok, block_expert, w_gu, b_gu, w_down, b_down):
    ne, d, de2 = w_gu.shape
    de = de2 // 2
    groups = d // LANE
    assert groups == SUBLANE
    n_slots = slot_tok.shape[0]
    blk_rows = EXPERT_BLOCK * groups
    grid_spec = pltpu.PrefetchScalarGridSpec(
        num_scalar_prefetch=1,
        grid=(n_slots // EXPERT_BLOCK,),
        in_specs=[pl.BlockSpec(memory_space=pl.ANY), pl.BlockSpec(memory_space=pl.ANY),
                  pl.BlockSpec((1, d, de2), lambda i, be: (be[i], 0, 0)),
                  pl.BlockSpec((1, 1, de2), lambda i, be: (be[i], 0, 0)),
                  pl.BlockSpec((1, de, d), lambda i, be: (be[i], 0, 0)),
                  pl.BlockSpec((1, 1, d), lambda i, be: (be[i], 0, 0))],
        out_specs=pl.BlockSpec((blk_rows, LANE), lambda i, be: (i, 0)),
        scratch_shapes=_gather_scratch(EXPERT_BLOCK),
    )
    return pl.pallas_call(
        _expert_kernel,
        out_shape=jax.ShapeDtypeStruct((n_slots * groups, LANE), F32),
        grid_spec=grid_spec,
        compiler_params=pltpu.CompilerParams(dimension_semantics=("arbitrary",), vmem_limit_bytes=VMEM_LIMIT),
        name="experts",
    )(block_expert, slot_tok.reshape(n_slots // EXPERT_BLOCK, EXPERT_BLOCK), h,
      w_gu, b_gu.reshape(ne, 1, de2), w_down, b_down.reshape(ne, 1, d))


def _combine_kernel(dest_ref, yb_ref, x1_ref, p_ref, g2_ref, fg_ref, o_ref, *gather_scratch):
    groups = x1_ref.shape[-1] // LANE
    step = pl.program_id(0) * pl.num_programs(1) + pl.program_id(1)
    n_steps = pl.num_programs(0) * pl.num_programs(1)
    y4_ref, prefetch = _gather_step(step, n_steps, TOK_BLOCK * TOP_K, 1, True, dest_ref, yb_ref, *gather_scratch)
    prefetch(0)
    p = p_ref[0]
    y = None
    for k in range(TOP_K):
        yk = p[:, k:k + 1] * _load_tile_rows(y4_ref, TOK_BLOCK, groups, first=k * groups, stride=TOP_K * groups)
        y = yk if y is None else y + yk
    x2 = x1_ref[0] + g2_ref[0] * y
    ms = jnp.mean(x2 * x2, axis=-1, keepdims=True)
    o_ref[0] = x2 * lax.rsqrt(ms + EPS) * fg_ref[...]


def _combine(x1, yb, dest, top_p, g2, final_g):
    b, l, d = x1.shape
    assert d // LANE == SUBLANE
    blk = lambda i, s: (i, s, 0)
    n_blk = l // TOK_BLOCK
    rows = TOK_BLOCK * TOP_K
    return pl.pallas_call(
        _combine_kernel,
        out_shape=jax.ShapeDtypeStruct((b, l, d), F32),
        grid=(b, n_blk),
        in_specs=[pl.BlockSpec(memory_space=pl.ANY), pl.BlockSpec(memory_space=pl.ANY),
                  pl.BlockSpec((1, TOK_BLOCK, d), blk),
                  pl.BlockSpec((1, TOK_BLOCK, TOP_K), blk),
                  pl.BlockSpec((1, 1, d), lambda i, s: (i, 0, 0)),
                  pl.BlockSpec((1, d), lambda i, s: (0, 0))],
        out_specs=pl.BlockSpec((1, TOK_BLOCK, d), blk),
        scratch_shapes=_gather_scratch(rows),
        compiler_params=pltpu.CompilerParams(dimension_semantics=("arbitrary", "arbitrary"),
                                             vmem_limit_bytes=VMEM_LIMIT),
        name="combine",
    )(dest.reshape(b * n_blk, rows), yb, x1, top_p, g2, final_g)


def _routing(logits):
    t = logits.shape[0]
    n_assign = t * TOP_K
    n_slots = -(-(n_assign + N_EXPERTS * (EXPERT_BLOCK - 1)) // EXPERT_BLOCK) * EXPERT_BLOCK
    top_logit, top_idx = lax.top_k(logits, TOP_K)
    top_p = jax.nn.softmax(top_logit, axis=-1)
    flat_e = top_idx.reshape(-1).astype(jnp.int32)
    experts = jnp.arange(N_EXPERTS, dtype=jnp.int32)
    lookup = lambda table, e: jnp.sum(jnp.where(e[:, None] == experts[None, :], table[None, :], 0), axis=1)
    pos = jnp.arange(n_assign, dtype=jnp.int32)
    assert N_EXPERTS * n_assign < 2 ** 31
    sorted_key = lax.sort(flat_e * n_assign + pos)
    sorted_e, order = sorted_key // n_assign, sorted_key % n_assign
    counts = jnp.sum((flat_e[:, None] == experts[None, :]).astype(jnp.int32), axis=0)
    start = jnp.cumsum(counts) - counts
    padded = (counts + EXPERT_BLOCK - 1) // EXPERT_BLOCK * EXPERT_BLOCK
    pad_end = jnp.cumsum(padded)
    pad_start = pad_end - padded
    dest_sorted = pos + lookup(pad_start - start, sorted_e)
    _, dest = lax.sort((order, dest_sorted), num_keys=1)
    block_start = jnp.arange(n_slots // EXPERT_BLOCK, dtype=jnp.int32) * EXPERT_BLOCK
    block_expert = jnp.minimum(jnp.sum((pad_end[None, :] <= block_start[:, None]).astype(jnp.int32), axis=1),
                               N_EXPERTS - 1)
    slot_e = jnp.repeat(block_expert, EXPERT_BLOCK)
    within = jnp.arange(n_slots, dtype=jnp.int32) - lookup(pad_start, slot_e)
    src = jnp.clip(lookup(start, slot_e) + within, 0, n_assign - 1)
    slot_tok = jnp.where(within < lookup(counts, slot_e), order[src] // TOP_K, 0)
    return slot_tok, dest, block_expert, top_p


def kernel(x, c, ctx, c_ctx, w_mod, b_mod, norm_mix_g, norm_ffn_g, w_in, conv_w, a_log_f, a_log_b, dt_bias_f,
           dt_bias_b, gdn_norm_g, gla_gate_w_f, gla_gate_w_b, gla_gate_b_f, gla_gate_b_b, gla_norm_g, w_out_a,
           w_out_b, w_out, w_router, b_router, w_gu, b_gu, w_down, b_down, final_norm_g):
    assert w_mod.shape[0] == 1, "single-layer kernel"
    b, l, d = x.shape
    n_ctx = ctx.shape[1]
    assert n_ctx == TOK_BLOCK and l % CONV_ROWS == 0 and d == GDN_HEADS * GDN_DIM
    gw = GDN_HEADS * GDN_DIM
    kw = GLA_HEADS * GLA_DK
    vw = GLA_HEADS * GLA_DV

    pad_rows = -(b + 1) % 8
    cc = jnp.concatenate([c, c_ctx[None], jnp.zeros((pad_rows, d), F32)], axis=0)
    mod = _modulation(cc, w_mod[0], b_mod[0])
    sh1, sc1, g1, sh2, sc2, g2 = [mod[:b, i * d:(i + 1) * d].reshape(b, 1, d) for i in range(6)]
    sh1c, sc1c = mod[b:b + 1, 0:d], mod[b:b + 1, d:2 * d]

    offs = {}
    o = 0
    for name, width in (("qa", gw), ("ka", gw), ("va", gw), ("za", gw), ("beta", 2 * GDN_HEADS),
                        ("dec", 2 * GDN_HEADS), ("qb", kw), ("kb", kw), ("vb", vw), ("rb", vw),
                        ("lr", 2 * GLA_RANK), ("gates", 2 * d)):
        offs[name] = (o, o + width)
        o += width
    wi = w_in[0]
    cols = lambda n: wi[:, offs[n][0]:offs[n][1]]
    w_main = jnp.concatenate([cols(n) for n in ("qa", "ka", "va", "za", "qb", "kb", "vb", "rb", "gates")],
                             axis=1).astype(BF16)
    zpad = lambda n: jnp.zeros((d, n), F32)
    w_small = jnp.concatenate([cols("beta"), cols("dec"), cols("lr"),
                               zpad(LANE - 4 * GDN_HEADS - 2 * GLA_RANK)], axis=1)
    w_small_hi = w_small.astype(BF16)
    w_small = jnp.stack([w_small_hi, (w_small - w_small_hi.astype(F32)).astype(BF16)])
    lane_pad = lambda v, lo: jnp.zeros((1, LANE), F32).at[0, lo:lo + v.shape[0]].set(v)
    coef_a = lane_pad(-jnp.exp(jnp.concatenate([a_log_f[0], a_log_b[0]])), 2 * GDN_HEADS)
    coef_b = lane_pad(jnp.concatenate([dt_bias_f[0], dt_bias_b[0]]), 2 * GDN_HEADS)

    p, gd = _input_projection(x, ctx, sh1, sc1, sh1c, sc1c, norm_mix_g, w_main, w_small, coef_a, coef_b)
    tot = p.shape[1]
    n_chunks = tot // CHUNK
    n_ctx_chunks = n_ctx // CHUNK

    qkv = _conv_qkv(p, conv_w[0].reshape(9, 3 * gw), n_ctx, l)
    gd_rows = gd[:, :, :4 * GDN_HEADS].reshape(b, n_chunks, CHUNK, 4 * GDN_HEADS).transpose(0, 1, 3, 2)
    gd_rows = jnp.pad(gd_rows, ((0, 0), (0, 0), (0, 0), (0, LANE - CHUNK)))
    oaf, oab = _gdn_scan(qkv, gd, _gdn_prep(qkv, gd, gd_rows), n_ctx_chunks)

    obf, obb = _gla_scan(p, gd, gla_gate_w_f[0], gla_gate_w_b[0], gla_gate_b_f, gla_gate_b_b, n_ctx_chunks,
                         qk_col=4, v_col=5)

    wr = jnp.zeros((d, LANE), F32).at[:, :N_EXPERTS].set(w_router[0])
    br = jnp.zeros((1, LANE), F32).at[0, :N_EXPERTS].set(b_router[0])
    x1, h2, logits = _merge(oaf, oab, obf, obb, p, x, g1, sh2, sc2,
                            jnp.tile(gdn_norm_g, (1, GDN_HEADS)), jnp.tile(gla_norm_g, (1, GLA_HEADS)), norm_ffn_g,
                            w_out_a[0].astype(BF16), w_out_b[0].astype(BF16), w_out[0].astype(BF16), wr, br,
                            cols=(3, 6, 7, 8))

    return _channel_mixing(x1, h2, logits, g2, w_gu, b_gu, w_down, b_down, final_norm_g)


def _channel_mixing(x1, h2, logits, g2, w_gu, b_gu, w_down, b_down, final_norm_g):
    b, l, d = x1.shape
    t = b * l
    slot_tok, dest, block_expert, top_p = _routing(logits.reshape(t, LANE)[:, :N_EXPERTS])
    yb = _experts(h2, slot_tok, block_expert, w_gu[0].astype(BF16), b_gu[0], w_down[0].astype(BF16), b_down[0])
    return _combine(x1, yb, dest, top_p.reshape(b, l, TOP_K), g2, final_norm_g.reshape(1, d))
```

```python
import functools

import jax
import jax.numpy as jnp
from jax import lax
from jax.experimental import pallas as pl
from jax.experimental.pallas import tpu as pltpu

F32 = jnp.float32
BF16 = jnp.bfloat16
HIGHEST = lax.Precision.HIGHEST

EPS = 1e-6
GRID_W = 64
CHUNK = 64
SCAN_CHUNKS = 4
TOK_BLOCK = 256
PROJ_SLICE = 1024
CONV_ROWS = 512
GDN_HEADS = 8
GDN_DIM = 128
GLA_HEADS = 4
GLA_DK = 128
GLA_DV = 256
GLA_RANK = 16
GLA_NORMALIZER = 16.0
N_EXPERTS = 32
TOP_K = 4
SWIGLU_LIMIT = 7.0
SWIGLU_ALPHA = 1.702
EXPERT_BLOCK = 512
LANE = 128
SUBLANE = 8
VMEM_LIMIT = 56 * 1024 * 1024


def _silu(t):
    return t * jax.nn.sigmoid(t)


def _softplus(t):
    return jnp.maximum(t, 0.0) + jnp.log1p(jnp.exp(-jnp.abs(t)))


def _dot(a, b, **kw):
    return jnp.dot(a, b, preferred_element_type=F32, **kw)


def _dot_nt(a, b, **kw):
    return lax.dot_general(a, b, (((1,), (1,)), ((), ())), preferred_element_type=F32, **kw)


def _dot_tn(a, b, **kw):
    return lax.dot_general(a, b, (((0,), (0,)), ((), ())), preferred_element_type=F32, **kw)


def _store_tile_rows(ref, val):
    rows, width = val.shape
    groups = width // LANE
    for s in range(groups):
        ref[pl.ds(s, rows, stride=groups), :] = val[:, s * LANE:(s + 1) * LANE]


def _load_tile_rows(ref, rows, groups, first=0, stride=None):
    stride = groups if stride is None else stride
    return jnp.concatenate([ref[pl.ds(first + s, rows, stride=stride), :] for s in range(groups)], axis=-1)


def _mod_kernel(c_ref, w_ref, b_ref, o_ref):
    o_ref[...] = _dot(_silu(c_ref[...]), w_ref[...], precision=HIGHEST) + b_ref[...]


def _modulation(cc, w_mod, b_mod):
    rows, d = cc.shape
    n = w_mod.shape[1]
    tn = n // 4
    return pl.pallas_call(
        _mod_kernel,
        out_shape=jax.ShapeDtypeStruct((rows, n), F32),
        grid=(n // tn,),
        in_specs=[pl.BlockSpec((rows, d), lambda j: (0, 0)),
                  pl.BlockSpec((d, tn), lambda j: (0, j)),
                  pl.BlockSpec((1, tn), lambda j: (0, j))],
        out_specs=pl.BlockSpec((rows, tn), lambda j: (0, j)),
        compiler_params=pltpu.CompilerParams(vmem_limit_bytes=VMEM_LIMIT),
        name="mod",
    )(cc, w_mod, b_mod.reshape(1, n))


def _inproj_kernel(x_ref, ctx_ref, shl_ref, scl_ref, shc_ref, scc_ref, g_ref, w_ref, ws_ref, ca_ref, cb_ref,
                   p_ref, gd_ref):
    is_ctx = pl.program_id(1) == 0
    xin = jnp.where(is_ctx, ctx_ref[0], x_ref[0])
    shift = jnp.where(is_ctx, shc_ref[...], shl_ref[0])
    scale = jnp.where(is_ctx, scc_ref[...], scl_ref[0])
    ms = jnp.mean(xin * xin, axis=-1, keepdims=True)
    h = (xin * lax.rsqrt(ms + EPS) * g_ref[...]) * (1.0 + scale) + shift
    hb = h.astype(BF16)
    n_main = w_ref.shape[1]
    for j in range(n_main // PROJ_SLICE):
        cols = slice(j * PROJ_SLICE, (j + 1) * PROJ_SLICE)
        p_ref[0, :, cols] = _dot(hb, w_ref[:, cols]).astype(p_ref.dtype)
    h_lo = (h - hb.astype(F32)).astype(BF16)
    raw = _dot(hb, ws_ref[0]) + (_dot(h_lo, ws_ref[0]) + _dot(hb, ws_ref[1]))
    lane = lax.broadcasted_iota(jnp.int32, raw.shape, 1)
    beta = jax.nn.sigmoid(raw)
    g = ca_ref[...] * _softplus(raw + cb_ref[...])
    gd_ref[0] = jnp.where(lane < 2 * GDN_HEADS, beta, jnp.where(lane < 4 * GDN_HEADS, g, raw))


def _input_projection(x, ctx, sh_lat, sc_lat, sh_ctx, sc_ctx, norm_g, w_main, w_small, coef_a, coef_b):
    b, l, d = x.shape
    n_blk = l // TOK_BLOCK + 1
    n_main = w_main.shape[1]
    tot = n_blk * TOK_BLOCK
    row3 = lambda i, s: (i, 0, 0)
    const2 = lambda i, s: (0, 0)
    blk = lambda i, s: (i, s, 0)
    return pl.pallas_call(
        _inproj_kernel,
        out_shape=(jax.ShapeDtypeStruct((b, tot, n_main), BF16),
                   jax.ShapeDtypeStruct((b, tot, LANE), F32)),
        grid=(b, n_blk),
        in_specs=[pl.BlockSpec((1, TOK_BLOCK, d), lambda i, s: (i, jnp.maximum(s - 1, 0), 0)),
                  pl.BlockSpec((1, TOK_BLOCK, d), row3),
                  pl.BlockSpec((1, 1, d), row3), pl.BlockSpec((1, 1, d), row3),
                  pl.BlockSpec((1, d), const2), pl.BlockSpec((1, d), const2),
                  pl.BlockSpec((1, d), const2),
                  pl.BlockSpec((d, n_main), const2, pipeline_mode=pl.Buffered(1)),
                  pl.BlockSpec((2, d, LANE), lambda i, s: (0, 0, 0)),
                  pl.BlockSpec((1, LANE), const2), pl.BlockSpec((1, LANE), const2)],
        out_specs=(pl.BlockSpec((1, TOK_BLOCK, n_main), blk),
                   pl.BlockSpec((1, TOK_BLOCK, LANE), blk)),
        compiler_params=pltpu.CompilerParams(dimension_semantics=("arbitrary", "arbitrary"),
                                             vmem_limit_bytes=VMEM_LIMIT),
        name="inproj",
    )(x, ctx, sh_lat, sc_lat, sh_ctx, sc_ctx, norm_g, w_main, w_small, coef_a, coef_b)


def _conv_kernel(p_ref, w_ref, o_ref, pad_ref, *, n_ctx, n_lat, tc, qk_cols):
    is_qk = pl.program_id(1) * tc < qk_cols
    w = w_ref[...]

    def post(y):
        s = _silu(y)
        outs = []
        for hh in range(tc // GDN_DIM):
            sh = s[:, hh * GDN_DIM:(hh + 1) * GDN_DIM]
            fac = lax.rsqrt(jnp.sum(sh * sh, axis=-1, keepdims=True) + EPS)
            outs.append(sh * jnp.where(is_qk, fac, 1.0))
        return outs[0] if len(outs) == 1 else jnp.concatenate(outs, axis=-1)

    xc = p_ref[0, 0:n_ctx, :].astype(F32)
    ridx = lax.broadcasted_iota(jnp.int32, (n_ctx, 1), 0)
    yc = (jnp.where(ridx == 0, 0.0, pltpu.roll(xc * w[3:4], 1, 0)) + xc * w[4:5]
          + jnp.where(ridx == n_ctx - 1, 0.0, pltpu.roll(xc * w[5:6], n_ctx - 1, 0)))
    o_ref[0, 0:n_ctx, :] = post(yc).astype(o_ref.dtype)

    zeros = jnp.zeros((GRID_W, tc), F32)
    pad_ref[0:GRID_W, :] = zeros
    pad_ref[GRID_W + n_lat:GRID_W + n_lat + GRID_W, :] = zeros
    pad_ref[GRID_W:GRID_W + n_lat, :] = p_ref[0, n_ctx:n_ctx + n_lat, :].astype(F32)
    rows = min(CONV_ROWS, n_lat)
    col = lax.broadcasted_iota(jnp.int32, (rows, 1), 0) % GRID_W

    def body(i, carry):
        s = pl.multiple_of(i * rows, rows)
        up = pad_ref[pl.ds(s, rows), :]
        mid = pad_ref[pl.ds(s + GRID_W, rows), :]
        dn = pad_ref[pl.ds(s + 2 * GRID_W, rows), :]
        a0 = up * w[0:1] + mid * w[3:4] + dn * w[6:7]
        a1 = up * w[1:2] + mid * w[4:5] + dn * w[7:8]
        a2 = up * w[2:3] + mid * w[5:6] + dn * w[8:9]
        y = (jnp.where(col == 0, 0.0, pltpu.roll(a0, 1, 0)) + a1
             + jnp.where(col == GRID_W - 1, 0.0, pltpu.roll(a2, rows - 1, 0)))
        o_ref[0, pl.ds(n_ctx + s, rows), :] = post(y).astype(o_ref.dtype)
        return carry

    lax.fori_loop(0, n_lat // rows, body, 0)


def _conv_qkv(p, conv_w9, n_ctx, n_lat):
    b, tot, _ = p.shape
    c = conv_w9.shape[1]
    tc = 2 * GDN_DIM
    kern = functools.partial(_conv_kernel, n_ctx=n_ctx, n_lat=n_lat, tc=tc, qk_cols=2 * GDN_HEADS * GDN_DIM)
    return pl.pallas_call(
        kern,
        out_shape=jax.ShapeDtypeStruct((b, tot, c), BF16),
        grid=(b, c // tc),
        in_specs=[pl.BlockSpec((1, tot, tc), lambda i, j: (i, 0, j)),
                  pl.BlockSpec((9, tc), lambda i, j: (0, j))],
        out_specs=pl.BlockSpec((1, tot, tc), lambda i, j: (i, 0, j)),
        scratch_shapes=[pltpu.VMEM((n_lat + 2 * GRID_W, tc), F32)],
        compiler_params=pltpu.CompilerParams(dimension_semantics=("arbitrary", "arbitrary"),
                                             vmem_limit_bytes=VMEM_LIMIT),
        name="conv",
    )(p, conv_w9)


def _bwd_chunk(n, n_ctx_chunks, n_chunks):
    return jnp.where(n < n_ctx_chunks, n_ctx_chunks - 1 - n, n_chunks + n_ctx_chunks - 1 - n)


def _cumsum_rows(x, reverse):
    n = x.shape[0]
    row = lax.broadcasted_iota(jnp.int32, (n, 1), 0)
    s = 1
    while s < n:
        if reverse:
            x = x + jnp.where(row < n - s, pltpu.roll(x, n - s, 0), 0.0)
        else:
            x = x + jnp.where(row >= s, pltpu.roll(x, s, 0), 0.0)
        s *= 2
    return x


def _tri_masks():
    r = lax.broadcasted_iota(jnp.int32, (CHUNK, CHUNK), 0)
    c = lax.broadcasted_iota(jnp.int32, (CHUNK, CHUNK), 1)
    return r >= c, r <= c, r == c


PREP_BATCH = 4
PREP_CHUNKS = TOK_BLOCK // CHUNK
SLAB = CHUNK + SUBLANE


def _substitute(n_scr, t_scr, reverse):
    n_seg = CHUNK // SUBLANE
    sub = lax.broadcasted_iota(jnp.int32, (SUBLANE, LANE), 0)
    for seg in (reversed(range(n_seg)) if reverse else range(n_seg)):
        js = range(SUBLANE * seg, CHUNK) if reverse else range(0, SUBLANE * (seg + 1))
        acc_groups = range(seg, n_seg) if reverse else range(0, seg + 1)
        t_scr[SUBLANE * seg * CHUNK:SUBLANE * (seg + 1) * CHUNK, :] = jnp.zeros((SUBLANE * CHUNK, LANE), F32)

        def body(step, carry, seg=seg, js=js, acc_groups=acc_groups):
            i = SUBLANE * seg + (SUBLANE - 1 - step if reverse else step)
            base = pl.multiple_of(i * CHUNK, CHUNK)
            acc = {g: jnp.zeros((SUBLANE, LANE), F32) for g in acc_groups}
            for j in js:
                nij = jnp.broadcast_to(n_scr[pl.ds(j * CHUNK + i, 1), :], (SUBLANE, LANE))
                for g in (range(j // SUBLANE, n_seg) if reverse else range(0, j // SUBLANE + 1)):
                    acc[g] = acc[g] + nij * t_scr[j * CHUNK + g * SUBLANE:j * CHUNK + (g + 1) * SUBLANE, :]
            acc[seg] = acc[seg] + jnp.where(sub + SUBLANE * seg == i, 1.0, 0.0)
            for g in acc_groups:
                t_scr[pl.ds(base + g * SUBLANE, SUBLANE), :] = acc[g]
            return carry

        lax.fori_loop(0, SUBLANE, body, 0)


def _gdn_prep_kernel(q_ref, k_ref, gc_ref, gr_ref, ta_ref, g_scr, n_scr, t_scr, r_scr):
    scale = GDN_DIM ** -0.5
    r = lax.broadcasted_iota(jnp.int32, (CHUNK, LANE), 0)
    c = lax.broadcasted_iota(jnp.int32, (CHUNK, LANE), 1)
    incl = (jnp.logical_and(r >= c, c < CHUNK), jnp.logical_and(r <= c, c < CHUNK))
    strict = (jnp.logical_and(r > c, c < CHUNK), jnp.logical_and(r < c, c < CHUNK))
    rl = lax.broadcasted_iota(jnp.int32, (LANE, LANE), 0)
    cl = lax.broadcasted_iota(jnp.int32, (LANE, LANE), 1)
    live = jnp.logical_and(rl < CHUNK, cl < CHUNK)
    tri_r = (jnp.logical_and(rl <= cl, live).astype(F32), jnp.logical_and(rl >= cl, live).astype(F32))
    k_zero = jnp.zeros((CHUNK, GDN_DIM), BF16)

    def per_chunk(fn):
        def outer(bb, carry):
            def inner(ch, carry2):
                fn(bb, ch, pl.multiple_of(ch * CHUNK, CHUNK), (bb * PREP_CHUNKS + ch) * GDN_HEADS)
                return carry2
            return lax.fori_loop(0, PREP_CHUNKS, inner, carry)
        lax.fori_loop(0, PREP_BATCH, outer, 0)

    def build(bb, ch, row0, p0):
        gcol = gc_ref[bb, pl.ds(row0, CHUNK), :]
        grow = gr_ref[bb, ch]
        cum_c = [_cumsum_rows(gcol, reverse=(d == 1)) for d in (0, 1)]
        cum_r = [_dot(grow, tri_r[d], precision=HIGHEST) for d in (0, 1)]
        for h in range(GDN_HEADS):
            sl = slice(h * GDN_DIM, (h + 1) * GDN_DIM)
            kb = k_ref[bb, pl.ds(row0, CHUNK), sl]
            qk = jnp.concatenate([q_ref[bb, pl.ds(row0, CHUNK), sl], kb], axis=0)
            gram = _dot_nt(qk, jnp.concatenate([kb, k_zero], axis=0))
            for d in (0, 1):
                bl = GDN_HEADS * d + h
                gl = 2 * GDN_HEADS + bl
                diff = cum_c[d][:, gl:gl + 1] - cum_r[d][gl:gl + 1, :]
                dec = jnp.exp(jnp.where(incl[d], diff, -jnp.inf))
                dec_t = jnp.exp(jnp.where(strict[1 - d], -diff, -jnp.inf))
                nm_t = -(gram[CHUNK:] * grow[bl:bl + 1, :] * dec_t)
                g_scr[d, pl.ds(pl.multiple_of((p0 + h) * SLAB, SUBLANE), CHUNK), :] = nm_t
                ta_ref[bb, ch, d, h, CHUNK:2 * CHUNK, :] = (gram[:CHUNK] * scale * dec)[:, :CHUNK].astype(BF16)

    per_chunk(build)

    for d in (0, 1):
        def to_lanes(i, carry, d=d):
            blk = g_scr[d, pl.ds(i, LANE, stride=SLAB), :]
            n_scr[pl.ds(pl.multiple_of(i * CHUNK, CHUNK), CHUNK), :] = blk.T[:CHUNK]
            return carry

        lax.fori_loop(0, CHUNK, to_lanes, 0, unroll=8)
        _substitute(n_scr, t_scr, reverse=(d == 1))

        def from_lanes(i, carry):
            blk = t_scr[pl.ds(pl.multiple_of(i * CHUNK, CHUNK), CHUNK), :]
            r_scr[pl.ds(i, LANE, stride=SLAB), :] = blk.T
            return carry

        lax.fori_loop(0, CHUNK, from_lanes, 0, unroll=8)

        def finish(bb, ch, row0, p0, d=d):
            grow = gr_ref[bb, ch]
            for h in range(GDN_HEADS):
                bl = GDN_HEADS * d + h
                t = r_scr[pl.ds(pl.multiple_of((p0 + h) * SLAB, SUBLANE), CHUNK), :]
                ta_ref[bb, ch, d, h, 0:CHUNK, :] = (t * grow[bl:bl + 1, :CHUNK]).astype(BF16)

        per_chunk(finish)


def _gdn_prep(qkv, gd, gd_rows):
    b, tot, _ = qkv.shape
    n_chunks = tot // CHUNK
    w = GDN_HEADS * GDN_DIM
    assert PREP_BATCH * PREP_CHUNKS * GDN_HEADS == LANE and b % PREP_BATCH == 0
    return pl.pallas_call(
        _gdn_prep_kernel,
        out_shape=jax.ShapeDtypeStruct((b, n_chunks, 2, GDN_HEADS, 2 * CHUNK, CHUNK), BF16),
        grid=(b // PREP_BATCH, tot // TOK_BLOCK),
        in_specs=[pl.BlockSpec((PREP_BATCH, TOK_BLOCK, w), lambda i, s: (i, s, 0)),
                  pl.BlockSpec((PREP_BATCH, TOK_BLOCK, w), lambda i, s: (i, s, 1)),
                  pl.BlockSpec((PREP_BATCH, TOK_BLOCK, LANE), lambda i, s: (i, s, 0)),
                  pl.BlockSpec((PREP_BATCH, PREP_CHUNKS, 4 * GDN_HEADS, LANE), lambda i, s: (i, s, 0, 0))],
        out_specs=pl.BlockSpec((PREP_BATCH, PREP_CHUNKS, 2, GDN_HEADS, 2 * CHUNK, CHUNK),
                               lambda i, s: (i, s, 0, 0, 0, 0)),
        scratch_shapes=[pltpu.VMEM((2, LANE * SLAB, LANE), F32),
                        pltpu.VMEM((CHUNK * CHUNK, LANE), F32),
                        pltpu.VMEM((CHUNK * CHUNK, LANE), F32),
                        pltpu.VMEM((LANE * SLAB, CHUNK), F32)],
        compiler_params=pltpu.CompilerParams(dimension_semantics=("arbitrary", "arbitrary"),
                                             vmem_limit_bytes=VMEM_LIMIT),
        name="gdn_prep",
    )(qkv, qkv, gd, gd_rows)


def _gdn_kernel(qf, kf, vf, qb, kb, vb, gcf, gcb, taf, tab, of, ob, s_ref):
    @pl.when(pl.program_id(1) == 0)
    def _():
        s_ref[...] = jnp.zeros(s_ref.shape, F32)

    scale = GDN_DIM ** -0.5
    dirs = ((qf, kf, vf, gcf, taf, of, CHUNK - 1), (qb, kb, vb, gcb, tab, ob, 0))
    for c in range(SCAN_CHUNKS):
        probs = []
        for d, (q_ref, k_ref, v_ref, gc_ref, ta_ref, o_ref, last) in enumerate(dirs):
            tc = c if d == 0 else SCAN_CHUNKS - 1 - c
            rows = slice(tc * CHUNK, (tc + 1) * CHUNK)
            cum_c = _cumsum_rows(gc_ref[0, rows, :], reverse=(d == 1))
            for h in range(GDN_HEADS):
                bl = GDN_HEADS * d + h
                gl = 2 * GDN_HEADS + bl
                sl = slice(h * GDN_DIM, (h + 1) * GDN_DIM)
                gc = cum_c[:, gl:gl + 1]
                probs.append(dict(bl=bl, sl=sl, gc=gc, g_tot=gc[last:last + 1, :], h=h, rows=rows, tc=tc,
                                  q_ref=q_ref, k_ref=k_ref, v_ref=v_ref, ta_ref=ta_ref, o_ref=o_ref))
        for p in probs:
            p["k"] = p["k_ref"][0, p["rows"], p["sl"]]
            qk = jnp.concatenate([p["q_ref"][0, p["rows"], p["sl"]], p["k"]], axis=0)
            p["s_old"] = s_ref[p["bl"]]
            p["qks"] = _dot(qk, p["s_old"].astype(BF16))
        for p in probs:
            p["eg"] = jnp.exp(p["gc"])
            x = p["v_ref"][0, p["rows"], p["sl"]].astype(F32) - p["eg"] * p["qks"][CHUNK:]
            p["ta"] = p["ta_ref"][0, p["tc"], 0, p["h"]]
            p["v_new"] = _dot(p["ta"][:CHUNK], x.astype(BF16)).astype(BF16)
        for p in probs:
            kd = p["k"].astype(F32) * jnp.exp(p["g_tot"] - p["gc"])
            s_ref[p["bl"]] = p["s_old"] * jnp.exp(p["g_tot"]) + _dot_tn(kd.astype(BF16), p["v_new"])
        for p in probs:
            o = (scale * p["eg"]) * p["qks"][:CHUNK] + _dot(p["ta"][CHUNK:], p["v_new"])
            p["o_ref"][0, p["rows"], p["sl"]] = o.astype(BF16)


def _gdn_scan(qkv, gd, ta, n_ctx_chunks):
    b, tot, _ = qkv.shape
    rows = SCAN_CHUNKS * CHUNK
    n_steps = tot // rows
    assert n_ctx_chunks % SCAN_CHUNKS == 0
    w = GDN_HEADS * GDN_DIM
    bwd = functools.partial(_bwd_chunk, n_ctx_chunks=n_ctx_chunks // SCAN_CHUNKS, n_chunks=n_steps)

    def tok(col, rev):
        if rev:
            return pl.BlockSpec((1, rows, w), lambda i, n: (i, bwd(n), col))
        return pl.BlockSpec((1, rows, w), lambda i, n: (i, n, col))

    gcol_f = pl.BlockSpec((1, rows, LANE), lambda i, n: (i, n, 0))
    gcol_b = pl.BlockSpec((1, rows, LANE), lambda i, n: (i, bwd(n), 0))
    ta_blk = (1, SCAN_CHUNKS, 1, GDN_HEADS, 2 * CHUNK, CHUNK)
    ta_f = pl.BlockSpec(ta_blk, lambda i, n: (i, n, 0, 0, 0, 0))
    ta_b = pl.BlockSpec(ta_blk, lambda i, n: (i, bwd(n), 1, 0, 0, 0))
    out = jax.ShapeDtypeStruct((b, tot, w), BF16)
    return pl.pallas_call(
        _gdn_kernel,
        out_shape=(out, out),
        grid=(b, n_steps),
        in_specs=[tok(0, False), tok(1, False), tok(2, False), tok(0, True), tok(1, True), tok(2, True),
                  gcol_f, gcol_b, ta_f, ta_b],
        out_specs=(tok(0, False), tok(0, True)),
        scratch_shapes=[pltpu.VMEM((2 * GDN_HEADS, GDN_DIM, GDN_DIM), F32)],
        compiler_params=pltpu.CompilerParams(dimension_semantics=("arbitrary", "arbitrary"),
                                             vmem_limit_bytes=VMEM_LIMIT),
        name="gdn",
    )(qkv, qkv, qkv, qkv, qkv, qkv, gd, gd, ta, ta)


def _gla_kernel(qkf, vf, qkb, vb, lrf, lrb, wf, wb, bf, bb, of, ob, s_ref):
    @pl.when(pl.program_id(1) == 0)
    def _():
        s_ref[...] = jnp.zeros(s_ref.shape, F32)

    scale = GLA_DK ** -0.5
    kw = GLA_HEADS * GLA_DK
    lower, upper, _ = _tri_masks()
    dirs = ((qkf, vf, lrf, wf, bf, of, lower, CHUNK // 2, CHUNK - 1),
            (qkb, vb, lrb, wb, bb, ob, upper, CHUNK - 1 - CHUNK // 2, 0))
    log_a = []
    for d, (_, _, lr_ref, w_ref, b_ref, _, _, _, _) in enumerate(dirs):
        lo = 4 * GDN_HEADS + d * GLA_RANK
        z = _dot(lr_ref[0, :, lo:lo + GLA_RANK], w_ref[...], precision=HIGHEST) + b_ref[...]
        log_a.append(-_softplus(-z) * (1.0 / GLA_NORMALIZER))
    for c in range(SCAN_CHUNKS):
        probs = []
        for d, (qk_ref, v_ref, _, _, _, o_ref, incl, centre, last) in enumerate(dirs):
            tc = c if d == 0 else SCAN_CHUNKS - 1 - c
            rows = slice(tc * CHUNK, (tc + 1) * CHUNK)
            cum = _cumsum_rows(log_a[d][rows], reverse=(d == 1))
            for h in range(GLA_HEADS):
                gc = cum[:, h * GLA_DK:(h + 1) * GLA_DK]
                probs.append(dict(idx=GLA_HEADS * d + h, gc=gc, incl=incl, o_ref=o_ref, rows=rows,
                                  g_mid=gc[centre:centre + 1, :], g_tot=gc[last:last + 1, :],
                                  q=qk_ref[0, rows, h * GLA_DK:(h + 1) * GLA_DK].astype(F32) * scale,
                                  k=qk_ref[0, rows, kw + h * GLA_DK:kw + (h + 1) * GLA_DK].astype(F32),
                                  v=v_ref[0, rows, h * GLA_DV:(h + 1) * GLA_DV],
                                  vsl=slice(h * GLA_DV, (h + 1) * GLA_DV)))
        for p in probs:
            attn = _dot_nt((p["q"] * jnp.exp(p["gc"] - p["g_mid"])).astype(BF16),
                           (p["k"] * jnp.exp(p["g_mid"] - p["gc"])).astype(BF16))
            p["attn"] = jnp.where(p["incl"], attn, 0.0).astype(BF16)
            p["s_old"] = s_ref[p["idx"]]
        for p in probs:
            o = (_dot(p["attn"], p["v"])
                 + _dot_nt((p["q"] * jnp.exp(p["gc"])).astype(BF16), p["s_old"].astype(BF16)))
            p["o_ref"][0, p["rows"], p["vsl"]] = o.astype(BF16)
        for p in probs:
            kd = p["k"] * jnp.exp(p["g_tot"] - p["gc"])
            s_ref[p["idx"]] = p["s_old"] * jnp.exp(p["g_tot"]) + _dot_tn(p["v"], kd.astype(BF16))


def _gla_scan(p, lr, wf, wb, bf, bb, n_ctx_chunks, qk_col, v_col):
    b, tot, _ = p.shape
    rows = SCAN_CHUNKS * CHUNK
    n_steps = tot // rows
    assert n_ctx_chunks % SCAN_CHUNKS == 0
    w = GLA_HEADS * GLA_DV
    bwd = functools.partial(_bwd_chunk, n_ctx_chunks=n_ctx_chunks // SCAN_CHUNKS, n_chunks=n_steps)

    def tok(width, col, rev):
        if rev:
            return pl.BlockSpec((1, rows, width), lambda i, n: (i, bwd(n), col))
        return pl.BlockSpec((1, rows, width), lambda i, n: (i, n, col))

    const2 = lambda i, n: (0, 0)
    out = jax.ShapeDtypeStruct((b, tot, w), BF16)
    return pl.pallas_call(
        _gla_kernel,
        out_shape=(out, out),
        grid=(b, n_steps),
        in_specs=[tok(w, qk_col, False), tok(w, v_col, False), tok(w, qk_col, True), tok(w, v_col, True),
                  tok(LANE, 0, False), tok(LANE, 0, True),
                  pl.BlockSpec(wf.shape, const2), pl.BlockSpec(wb.shape, const2),
                  pl.BlockSpec(bf.shape, const2), pl.BlockSpec(bb.shape, const2)],
        out_specs=(tok(w, 0, False), tok(w, 0, True)),
        scratch_shapes=[pltpu.VMEM((2 * GLA_HEADS, GLA_DV, GLA_DK), F32)],
        compiler_params=pltpu.CompilerParams(dimension_semantics=("arbitrary", "arbitrary"),
                                             vmem_limit_bytes=VMEM_LIMIT),
        name="gla",
    )(p, p, p, p, lr, lr, wf, wb, bf, bb)


def _head_rms(o, g, width):
    outs = []
    for h in range(o.shape[-1] // width):
        oh = o[:, h * width:(h + 1) * width]
        ms = jnp.mean(oh * oh, axis=-1, keepdims=True)
        outs.append(oh * lax.rsqrt(ms + EPS))
    return jnp.concatenate(outs, axis=-1) * g


def _merge_kernel(oaf, oab, obf, obb, z_ref, r_ref, ga_ref, gb_ref, x_ref, g1_ref, sh2_ref, sc2_ref,
                  na_ref, nb_ref, nf_ref, woa_ref, wob_ref, wo_ref, wr_ref, br_ref,
                  x1_ref, h2_ref, lg_ref):
    f32 = lambda ref: ref[0].astype(F32)
    ya_in = _head_rms(f32(oaf) + f32(oab), na_ref[...], GDN_DIM) * _silu(f32(z_ref))
    ya = _dot(ya_in.astype(BF16), woa_ref[...])
    yb_in = _head_rms(f32(obf) + f32(obb), nb_ref[...], GLA_DV) * _silu(f32(r_ref))
    yb = _dot(yb_in.astype(BF16), wob_ref[...])
    y = jax.nn.sigmoid(f32(ga_ref)) * ya + jax.nn.sigmoid(f32(gb_ref)) * yb
    x1 = x_ref[0] + g1_ref[0] * _dot(y.astype(BF16), wo_ref[...])
    x1_ref[0] = x1
    ms = jnp.mean(x1 * x1, axis=-1, keepdims=True)
    h2 = (x1 * lax.rsqrt(ms + EPS) * nf_ref[...]) * (1.0 + sc2_ref[0]) + sh2_ref[0]
    _store_tile_rows(h2_ref, h2)
    lg_ref[0] = _dot(h2, wr_ref[...], precision=HIGHEST) + br_ref[...]


def _merge(oaf, oab, obf, obb, p, x, g1, sh2, sc2, na, nb, nf, woa, wob, wo, wr, br, cols):
    b, l, d = x.shape
    z_col, r_col, ga_col, gb_col = cols
    lat = lambda i, s: (i, s + 1, 0)
    row3 = lambda i, s: (i, 0, 0)
    const2 = lambda i, s: (0, 0)
    blk = lambda i, s: (i, s, 0)
    tb = (1, TOK_BLOCK, d)
    pcol = lambda c: pl.BlockSpec(tb, lambda i, s: (i, s + 1, c))
    vec = pl.BlockSpec((1, 1, d), row3)
    cvec = pl.BlockSpec((1, d), const2)
    wspec = pl.BlockSpec((d, d), const2)
    groups = d // LANE
    n_blk = l // TOK_BLOCK
    return pl.pallas_call(
        _merge_kernel,
        out_shape=(jax.ShapeDtypeStruct((b, l, d), F32),
                   jax.ShapeDtypeStruct((b * l * groups, LANE), F32),
                   jax.ShapeDtypeStruct((b, l, LANE), F32)),
        grid=(b, n_blk),
        in_specs=[pl.BlockSpec(tb, lat)] * 4 + [pcol(z_col), pcol(r_col), pcol(ga_col), pcol(gb_col),
                  pl.BlockSpec(tb, blk), vec, vec, vec, cvec, cvec, cvec, wspec, wspec, wspec,
                  pl.BlockSpec((d, LANE), const2), pl.BlockSpec((1, LANE), const2)],
        out_specs=(pl.BlockSpec(tb, blk),
                   pl.BlockSpec((TOK_BLOCK * groups, LANE), lambda i, s: (i * n_blk + s, 0)),
                   pl.BlockSpec((1, TOK_BLOCK, LANE), blk)),
        compiler_params=pltpu.CompilerParams(dimension_semantics=("arbitrary", "arbitrary"),
                                             vmem_limit_bytes=VMEM_LIMIT),
        name="merge",
    )(oaf, oab, obf, obb, p, p, p, p, x, g1, sh2, sc2, na, nb, nf, woa, wob, wo, wr, br)


def _row_copy(src_ref, dst_ref, sem, src_row, dst_row):
    first = lambda row: row * SUBLANE if isinstance(row, int) else pl.multiple_of(row * SUBLANE, SUBLANE)
    src, dst = first(src_row), first(dst_row)
    return pltpu.make_async_copy(src_ref.at[pl.ds(src, SUBLANE)], dst_ref.at[pl.ds(dst, SUBLANE)], sem)


def _gather_step(step, n_steps, rows, phases, two_queues, idx_hbm, src_hbm, idx_smem, buf, idx_sem, row_sem):
    part = rows // phases

    def idx_copy(s):
        return pltpu.make_async_copy(idx_hbm.at[s], idx_smem.at[s % 2], idx_sem.at[s % 2])

    def issue(s, lo, hi):
        slot = s % 2
        for r in range(lo, hi):
            _row_copy(src_hbm, buf.at[slot], row_sem.at[slot], idx_smem[slot, r], r).start(
                priority=r % 2 if two_queues else 0)

    @pl.when(step == 0)
    def _():
        idx_copy(0).start()
        idx_copy(0).wait()

        def first_rows(r, carry):
            _row_copy(src_hbm, buf.at[0], row_sem.at[0], idx_smem[0, r], r).start()
            return carry

        lax.fori_loop(0, rows, first_rows, 0)

        @pl.when(n_steps > 1)
        def _():
            idx_copy(1).start()

    @pl.when(step + 1 < n_steps)
    def _():
        idx_copy(step + 1).wait()

    def prefetch(j):
        @pl.when(step + 1 < n_steps)
        def _():
            issue(step + 1, j * part, (j + 1) * part)

        if j == phases - 1:
            @pl.when(step + 2 < n_steps)
            def _():
                idx_copy(step + 2).start()

    slot = step % 2
    pltpu.make_async_copy(src_hbm.at[pl.ds(0, rows * SUBLANE)], buf.at[slot], row_sem.at[slot]).wait()
    return buf.at[slot], prefetch


def _gather_scratch(rows):
    return [pltpu.SMEM((2, rows), jnp.int32), pltpu.VMEM((2, rows * SUBLANE, LANE), F32),
            pltpu.SemaphoreType.DMA((2,)), pltpu.SemaphoreType.DMA((2,))]


def _expert_kernel(be_ref, tok_ref, h_ref, wgu_ref, bgu_ref, wd_ref, bd_ref, y_ref, wgu_bf, wd_bf, *gather_scratch):
    step = pl.program_id(0)
    de = wd_ref.shape[1]
    groups = wgu_ref.shape[1] // LANE

    @pl.when(jnp.logical_or(step == 0, be_ref[step] != be_ref[jnp.maximum(step - 1, 0)]))
    def _():
        wgu_bf[...] = wgu_ref[0].astype(BF16)
        wd_bf[...] = wd_ref[0].astype(BF16)

    x_ref, prefetch = _gather_step(step, pl.num_programs(0), EXPERT_BLOCK, 2, False,
                                   tok_ref, h_ref, *gather_scratch)
    prefetch(0)
    x = _load_tile_rows(x_ref, EXPERT_BLOCK, groups)
    gu = _dot(x.astype(BF16), wgu_bf[...]) + bgu_ref[0]
    glu = jnp.minimum(gu[:, :de], SWIGLU_LIMIT)
    lin = jnp.clip(gu[:, de:], -SWIGLU_LIMIT, SWIGLU_LIMIT)
    act = (glu * jax.nn.sigmoid(SWIGLU_ALPHA * glu) * (lin + 1.0)).astype(BF16)
    prefetch(1)
    _store_tile_rows(y_ref, _dot(act, wd_bf[...]) + bd_ref[0])


def _experts(h, slot_tok, block_expert, w_gu, b_gu, w_down, b_down):
    ne, d, de2 = w_gu.shape
    de = de2 // 2
    groups = d // LANE
    assert groups == SUBLANE
    n_slots = slot_tok.shape[0]
    blk_rows = EXPERT_BLOCK * groups
    grid_spec = pltpu.PrefetchScalarGridSpec(
        num_scalar_prefetch=1,
        grid=(n_slots // EXPERT_BLOCK,),
        in_specs=[pl.BlockSpec(memory_space=pl.ANY), pl.BlockSpec(memory_space=pl.ANY),
                  pl.BlockSpec((1, d, de2), lambda i, be: (be[i], 0, 0)),
                  pl.BlockSpec((1, 1, de2), lambda i, be: (be[i], 0, 0)),
                  pl.BlockSpec((1, de, d), lambda i, be: (be[i], 0, 0)),
                  pl.BlockSpec((1, 1, d), lambda i, be: (be[i], 0, 0))],
        out_specs=pl.BlockSpec((blk_rows, LANE), lambda i, be: (i, 0)),
        scratch_shapes=[pltpu.VMEM((d, de2), BF16), pltpu.VMEM((de, d), BF16)] + _gather_scratch(EXPERT_BLOCK),
    )
    return pl.pallas_call(
        _expert_kernel,
        out_shape=jax.ShapeDtypeStruct((n_slots * groups, LANE), F32),
        grid_spec=grid_spec,
        compiler_params=pltpu.CompilerParams(dimension_semantics=("arbitrary",), vmem_limit_bytes=VMEM_LIMIT),
        name="experts",
    )(block_expert, slot_tok.reshape(n_slots // EXPERT_BLOCK, EXPERT_BLOCK), h,
      w_gu, b_gu.reshape(ne, 1, de2), w_down, b_down.reshape(ne, 1, d))


def _combine_kernel(dest_ref, yb_ref, x1_ref, p_ref, g2_ref, fg_ref, o_ref, *gather_scratch):
    groups = x1_ref.shape[-1] // LANE
    step = pl.program_id(0) * pl.num_programs(1) + pl.program_id(1)
    n_steps = pl.num_programs(0) * pl.num_programs(1)
    y4_ref, prefetch = _gather_step(step, n_steps, TOK_BLOCK * TOP_K, 1, True, dest_ref, yb_ref, *gather_scratch)
    prefetch(0)
    p = p_ref[0]
    y = None
    for k in range(TOP_K):
        yk = p[:, k:k + 1] * _load_tile_rows(y4_ref, TOK_BLOCK, groups, first=k * groups, stride=TOP_K * groups)
        y = yk if y is None else y + yk
    x2 = x1_ref[0] + g2_ref[0] * y
    ms = jnp.mean(x2 * x2, axis=-1, keepdims=True)
    o_ref[0] = x2 * lax.rsqrt(ms + EPS) * fg_ref[...]


def _combine(x1, yb, dest, top_p, g2, final_g):
    b, l, d = x1.shape
    assert d // LANE == SUBLANE
    blk = lambda i, s: (i, s, 0)
    n_blk = l // TOK_BLOCK
    rows = TOK_BLOCK * TOP_K
    return pl.pallas_call(
        _combine_kernel,
        out_shape=jax.ShapeDtypeStruct((b, l, d), F32),
        grid=(b, n_blk),
        in_specs=[pl.BlockSpec(memory_space=pl.ANY), pl.BlockSpec(memory_space=pl.ANY),
                  pl.BlockSpec((1, TOK_BLOCK, d), blk),
                  pl.BlockSpec((1, TOK_BLOCK, TOP_K), blk),
                  pl.BlockSpec((1, 1, d), lambda i, s: (i, 0, 0)),
                  pl.BlockSpec((1, d), lambda i, s: (0, 0))],
        out_specs=pl.BlockSpec((1, TOK_BLOCK, d), blk),
        scratch_shapes=_gather_scratch(rows),
        compiler_params=pltpu.CompilerParams(dimension_semantics=("arbitrary", "arbitrary"),
                                             vmem_limit_bytes=VMEM_LIMIT),
        name="combine",
    )(dest.reshape(b * n_blk, rows), yb, x1, top_p, g2, final_g)


def _routing(logits):
    t = logits.shape[0]
    n_assign = t * TOP_K
    n_slots = -(-(n_assign + N_EXPERTS * (EXPERT_BLOCK - 1)) // EXPERT_BLOCK) * EXPERT_BLOCK
    top_logit, top_idx = lax.top_k(logits, TOP_K)
    top_p = jax.nn.softmax(top_logit, axis=-1)
    flat_e = top_idx.reshape(-1).astype(jnp.int32)
    experts = jnp.arange(N_EXPERTS, dtype=jnp.int32)
    lookup = lambda table, e: jnp.sum(jnp.where(e[:, None] == experts[None, :], table[None, :], 0), axis=1)
    pos = jnp.arange(n_assign, dtype=jnp.int32)
    assert N_EXPERTS * n_assign < 2 ** 31
    sorted_key = lax.sort(flat_e * n_assign + pos)
    sorted_e, order = sorted_key // n_assign, sorted_key % n_assign
    counts = jnp.sum((flat_e[:, None] == experts[None, :]).astype(jnp.int32), axis=0)
    start = jnp.cumsum(counts) - counts
    padded = (counts + EXPERT_BLOCK - 1) // EXPERT_BLOCK * EXPERT_BLOCK
    pad_end = jnp.cumsum(padded)
    pad_start = pad_end - padded
    dest_sorted = pos + lookup(pad_start - start, sorted_e)
    _, dest = lax.sort((order, dest_sorted), num_keys=1)
    block_start = jnp.arange(n_slots // EXPERT_BLOCK, dtype=jnp.int32) * EXPERT_BLOCK
    block_expert = jnp.minimum(jnp.sum((pad_end[None, :] <= block_start[:, None]).astype(jnp.int32), axis=1),
                               N_EXPERTS - 1)
    slot_e = jnp.repeat(block_expert, EXPERT_BLOCK)
    within = jnp.arange(n_slots, dtype=jnp.int32) - lookup(pad_start, slot_e)
    src = jnp.clip(lookup(start, slot_e) + within, 0, n_assign - 1)
    slot_tok = jnp.where(within < lookup(counts, slot_e), order[src] // TOP_K, 0)
    return slot_tok, dest, block_expert, top_p


def kernel(x, c, ctx, c_ctx, w_mod, b_mod, norm_mix_g, norm_ffn_g, w_in, conv_w, a_log_f, a_log_b, dt_bias_f,
           dt_bias_b, gdn_norm_g, gla_gate_w_f, gla_gate_w_b, gla_gate_b_f, gla_gate_b_b, gla_norm_g, w_out_a,
           w_out_b, w_out, w_router, b_router, w_gu, b_gu, w_down, b_down, final_norm_g):
    assert w_mod.shape[0] == 1, "single-layer kernel"
    b, l, d = x.shape
    n_ctx = ctx.shape[1]
    assert n_ctx == TOK_BLOCK and l % CONV_ROWS == 0 and d == GDN_HEADS * GDN_DIM
    gw = GDN_HEADS * GDN_DIM
    kw = GLA_HEADS * GLA_DK
    vw = GLA_HEADS * GLA_DV

    pad_rows = -(b + 1) % 8
    cc = jnp.concatenate([c, c_ctx[None], jnp.zeros((pad_rows, d), F32)], axis=0)
    mod = _modulation(cc, w_mod[0], b_mod[0])
    sh1, sc1, g1, sh2, sc2, g2 = [mod[:b, i * d:(i + 1) * d].reshape(b, 1, d) for i in range(6)]
    sh1c, sc1c = mod[b:b + 1, 0:d], mod[b:b + 1, d:2 * d]

    offs = {}
    o = 0
    for name, width in (("qa", gw), ("ka", gw), ("va", gw), ("za", gw), ("beta", 2 * GDN_HEADS),
                        ("dec", 2 * GDN_HEADS), ("qb", kw), ("kb", kw), ("vb", vw), ("rb", vw),
                        ("lr", 2 * GLA_RANK), ("gates", 2 * d)):
        offs[name] = (o, o + width)
        o += width
    wi = w_in[0]
    cols = lambda n: wi[:, offs[n][0]:offs[n][1]]
    w_main = jnp.concatenate([cols(n) for n in ("qa", "ka", "va", "za", "qb", "kb", "vb", "rb", "gates")],
                             axis=1).astype(BF16)
    zpad = lambda n: jnp.zeros((d, n), F32)
    w_small = jnp.concatenate([cols("beta"), cols("dec"), cols("lr"),
                               zpad(LANE - 4 * GDN_HEADS - 2 * GLA_RANK)], axis=1)
    w_small_hi = w_small.astype(BF16)
    w_small = jnp.stack([w_small_hi, (w_small - w_small_hi.astype(F32)).astype(BF16)])
    lane_pad = lambda v, lo: jnp.zeros((1, LANE), F32).at[0, lo:lo + v.shape[0]].set(v)
    coef_a = lane_pad(-jnp.exp(jnp.concatenate([a_log_f[0], a_log_b[0]])), 2 * GDN_HEADS)
    coef_b = lane_pad(jnp.concatenate([dt_bias_f[0], dt_bias_b[0]]), 2 * GDN_HEADS)

    p, gd = _input_projection(x, ctx, sh1, sc1, sh1c, sc1c, norm_mix_g, w_main, w_small, coef_a, coef_b)
    tot = p.shape[1]
    n_chunks = tot // CHUNK
    n_ctx_chunks = n_ctx // CHUNK

    qkv = _conv_qkv(p, conv_w[0].reshape(9, 3 * gw), n_ctx, l)
    gd_rows = gd[:, :, :4 * GDN_HEADS].reshape(b, n_chunks, CHUNK, 4 * GDN_HEADS).transpose(0, 1, 3, 2)
    gd_rows = jnp.pad(gd_rows, ((0, 0), (0, 0), (0, 0), (0, LANE - CHUNK)))
    oaf, oab = _gdn_scan(qkv, gd, _gdn_prep(qkv, gd, gd_rows), n_ctx_chunks)

    obf, obb = _gla_scan(p, gd, gla_gate_w_f[0], gla_gate_w_b[0], gla_gate_b_f, gla_gate_b_b, n_ctx_chunks,
                         qk_col=4, v_col=5)

    wr = jnp.zeros((d, LANE), F32).at[:, :N_EXPERTS].set(w_router[0])
    br = jnp.zeros((1, LANE), F32).at[0, :N_EXPERTS].set(b_router[0])
    x1, h2, logits = _merge(oaf, oab, obf, obb, p, x, g1, sh2, sc2,
                            jnp.tile(gdn_norm_g, (1, GDN_HEADS)), jnp.tile(gla_norm_g, (1, GLA_HEADS)), norm_ffn_g,
                            w_out_a[0].astype(BF16), w_out_b[0].astype(BF16), w_out[0].astype(BF16), wr, br,
                            cols=(3, 6, 7, 8))

    return _channel_mixing(x1, h2, logits, g2, w_gu, b_gu, w_down, b_down, final_norm_g)


def _channel_mixing(x1, h2, logits, g2, w_gu, b_gu, w_down, b_down, final_norm_g):
    b, l, d = x1.shape
    t = b * l
    slot_tok, dest, block_expert, top_p = _routing(logits.reshape(t, LANE)[:, :N_EXPERTS])
    yb = _experts(h2, slot_tok, block_expert, w_gu[0], b_gu[0], w_down[0], b_down[0])
    return _combine(x1, yb, dest, top_p.reshape(b, l, TOP_K), g2, final_norm_g.reshape(1, d))
```

```python
import functools

import jax
import jax.numpy as jnp
from jax import lax
from jax.experimental import pallas as pl
from jax.experimental.pallas import tpu as pltpu

F32 = jnp.float32
BF16 = jnp.bfloat16
HIGHEST = lax.Precision.HIGHEST

EPS = 1e-6
GRID_W = 64
CHUNK = 64
SCAN_CHUNKS = 4
TOK_BLOCK = 256
PROJ_SLICE = 1024
CONV_ROWS = 512
GDN_HEADS = 8
GDN_DIM = 128
GLA_HEADS = 4
GLA_DK = 128
GLA_DV = 256
GLA_RANK = 16
GLA_NORMALIZER = 16.0
N_EXPERTS = 32
TOP_K = 4
SWIGLU_LIMIT = 7.0
SWIGLU_ALPHA = 1.702
EXPERT_BLOCK = 512
LANE = 128
SUBLANE = 8
VMEM_LIMIT = 56 * 1024 * 1024


def _silu(t):
    return t * jax.nn.sigmoid(t)


def _softplus(t):
    return jnp.maximum(t, 0.0) + jnp.log1p(jnp.exp(-jnp.abs(t)))


def _dot(a, b, **kw):
    return jnp.dot(a, b, preferred_element_type=F32, **kw)


def _dot_nt(a, b, **kw):
    return lax.dot_general(a, b, (((1,), (1,)), ((), ())), preferred_element_type=F32, **kw)


def _dot_tn(a, b, **kw):
    return lax.dot_general(a, b, (((0,), (0,)), ((), ())), preferred_element_type=F32, **kw)


def _store_tile_rows(ref, val):
    rows, width = val.shape
    groups = width // LANE
    for s in range(groups):
        ref[pl.ds(s, rows, stride=groups), :] = val[:, s * LANE:(s + 1) * LANE]


def _load_tile_rows(ref, rows, groups, first=0, stride=None):
    stride = groups if stride is None else stride
    return jnp.concatenate([ref[pl.ds(first + s, rows, stride=stride), :] for s in range(groups)], axis=-1)


def _mod_kernel(c_ref, w_ref, b_ref, o_ref):
    o_ref[...] = _dot(_silu(c_ref[...]), w_ref[...], precision=HIGHEST) + b_ref[...]


def _modulation(cc, w_mod, b_mod):
    rows, d = cc.shape
    n = w_mod.shape[1]
    tn = n // 4
    return pl.pallas_call(
        _mod_kernel,
        out_shape=jax.ShapeDtypeStruct((rows, n), F32),
        grid=(n // tn,),
        in_specs=[pl.BlockSpec((rows, d), lambda j: (0, 0)),
                  pl.BlockSpec((d, tn), lambda j: (0, j)),
                  pl.BlockSpec((1, tn), lambda j: (0, j))],
        out_specs=pl.BlockSpec((rows, tn), lambda j: (0, j)),
        compiler_params=pltpu.CompilerParams(vmem_limit_bytes=VMEM_LIMIT),
        name="mod",
    )(cc, w_mod, b_mod.reshape(1, n))


def _inproj_kernel(x_ref, ctx_ref, shl_ref, scl_ref, shc_ref, scc_ref, g_ref, w_ref, ws_ref, ca_ref, cb_ref,
                   p_ref, gd_ref):
    is_ctx = pl.program_id(1) == 0
    xin = jnp.where(is_ctx, ctx_ref[0], x_ref[0])
    shift = jnp.where(is_ctx, shc_ref[...], shl_ref[0])
    scale = jnp.where(is_ctx, scc_ref[...], scl_ref[0])
    ms = jnp.mean(xin * xin, axis=-1, keepdims=True)
    h = (xin * lax.rsqrt(ms + EPS) * g_ref[...]) * (1.0 + scale) + shift
    hb = h.astype(BF16)
    n_main = w_ref.shape[1]
    for j in range(n_main // PROJ_SLICE):
        cols = slice(j * PROJ_SLICE, (j + 1) * PROJ_SLICE)
        p_ref[0, :, cols] = _dot(hb, w_ref[:, cols]).astype(p_ref.dtype)
    h_lo = (h - hb.astype(F32)).astype(BF16)
    raw = _dot(hb, ws_ref[0]) + (_dot(h_lo, ws_ref[0]) + _dot(hb, ws_ref[1]))
    lane = lax.broadcasted_iota(jnp.int32, raw.shape, 1)
    beta = jax.nn.sigmoid(raw)
    g = ca_ref[...] * _softplus(raw + cb_ref[...])
    gd_ref[0] = jnp.where(lane < 2 * GDN_HEADS, beta, jnp.where(lane < 4 * GDN_HEADS, g, raw))


def _input_projection(x, ctx, sh_lat, sc_lat, sh_ctx, sc_ctx, norm_g, w_main, w_small, coef_a, coef_b):
    b, l, d = x.shape
    n_blk = l // TOK_BLOCK + 1
    n_main = w_main.shape[1]
    tot = n_blk * TOK_BLOCK
    row3 = lambda i, s: (i, 0, 0)
    const2 = lambda i, s: (0, 0)
    blk = lambda i, s: (i, s, 0)
    return pl.pallas_call(
        _inproj_kernel,
        out_shape=(jax.ShapeDtypeStruct((b, tot, n_main), BF16),
                   jax.ShapeDtypeStruct((b, tot, LANE), F32)),
        grid=(b, n_blk),
        in_specs=[pl.BlockSpec((1, TOK_BLOCK, d), lambda i, s: (i, jnp.maximum(s - 1, 0), 0)),
                  pl.BlockSpec((1, TOK_BLOCK, d), row3),
                  pl.BlockSpec((1, 1, d), row3), pl.BlockSpec((1, 1, d), row3),
                  pl.BlockSpec((1, d), const2), pl.BlockSpec((1, d), const2),
                  pl.BlockSpec((1, d), const2),
                  pl.BlockSpec((d, n_main), const2, pipeline_mode=pl.Buffered(1)),
                  pl.BlockSpec((2, d, LANE), lambda i, s: (0, 0, 0)),
                  pl.BlockSpec((1, LANE), const2), pl.BlockSpec((1, LANE), const2)],
        out_specs=(pl.BlockSpec((1, TOK_BLOCK, n_main), blk),
                   pl.BlockSpec((1, TOK_BLOCK, LANE), blk)),
        compiler_params=pltpu.CompilerParams(dimension_semantics=("arbitrary", "arbitrary"),
                                             vmem_limit_bytes=VMEM_LIMIT),
        name="inproj",
    )(x, ctx, sh_lat, sc_lat, sh_ctx, sc_ctx, norm_g, w_main, w_small, coef_a, coef_b)


def _conv_kernel(p_ref, w_ref, o_ref, pad_ref, *, n_ctx, n_lat, tc, qk_cols):
    is_qk = pl.program_id(1) * tc < qk_cols
    w = w_ref[...]

    def post(y):
        s = _silu(y)
        outs = []
        for hh in range(tc // GDN_DIM):
            sh = s[:, hh * GDN_DIM:(hh + 1) * GDN_DIM]
            fac = lax.rsqrt(jnp.sum(sh * sh, axis=-1, keepdims=True) + EPS)
            outs.append(sh * jnp.where(is_qk, fac, 1.0))
        return outs[0] if len(outs) == 1 else jnp.concatenate(outs, axis=-1)

    xc = p_ref[0, 0:n_ctx, :].astype(F32)
    ridx = lax.broadcasted_iota(jnp.int32, (n_ctx, 1), 0)
    yc = (jnp.where(ridx == 0, 0.0, pltpu.roll(xc * w[3:4], 1, 0)) + xc * w[4:5]
          + jnp.where(ridx == n_ctx - 1, 0.0, pltpu.roll(xc * w[5:6], n_ctx - 1, 0)))
    o_ref[0, 0:n_ctx, :] = post(yc).astype(o_ref.dtype)

    zeros = jnp.zeros((GRID_W, tc), F32)
    pad_ref[0:GRID_W, :] = zeros
    pad_ref[GRID_W + n_lat:GRID_W + n_lat + GRID_W, :] = zeros
    pad_ref[GRID_W:GRID_W + n_lat, :] = p_ref[0, n_ctx:n_ctx + n_lat, :].astype(F32)
    rows = min(CONV_ROWS, n_lat)
    col = lax.broadcasted_iota(jnp.int32, (rows, 1), 0) % GRID_W

    def body(i, carry):
        s = pl.multiple_of(i * rows, rows)
        up = pad_ref[pl.ds(s, rows), :]
        mid = pad_ref[pl.ds(s + GRID_W, rows), :]
        dn = pad_ref[pl.ds(s + 2 * GRID_W, rows), :]
        a0 = up * w[0:1] + mid * w[3:4] + dn * w[6:7]
        a1 = up * w[1:2] + mid * w[4:5] + dn * w[7:8]
        a2 = up * w[2:3] + mid * w[5:6] + dn * w[8:9]
        y = (jnp.where(col == 0, 0.0, pltpu.roll(a0, 1, 0)) + a1
             + jnp.where(col == GRID_W - 1, 0.0, pltpu.roll(a2, rows - 1, 0)))
        o_ref[0, pl.ds(n_ctx + s, rows), :] = post(y).astype(o_ref.dtype)
        return carry

    lax.fori_loop(0, n_lat // rows, body, 0)


def _conv_qkv(p, conv_w9, n_ctx, n_lat):
    b, tot, _ = p.shape
    c = conv_w9.shape[1]
    tc = 2 * GDN_DIM
    kern = functools.partial(_conv_kernel, n_ctx=n_ctx, n_lat=n_lat, tc=tc, qk_cols=2 * GDN_HEADS * GDN_DIM)
    return pl.pallas_call(
        kern,
        out_shape=jax.ShapeDtypeStruct((b, tot, c), BF16),
        grid=(b, c // tc),
        in_specs=[pl.BlockSpec((1, tot, tc), lambda i, j: (i, 0, j)),
                  pl.BlockSpec((9, tc), lambda i, j: (0, j))],
        out_specs=pl.BlockSpec((1, tot, tc), lambda i, j: (i, 0, j)),
        scratch_shapes=[pltpu.VMEM((n_lat + 2 * GRID_W, tc), F32)],
        compiler_params=pltpu.CompilerParams(dimension_semantics=("arbitrary", "arbitrary"),
                                             vmem_limit_bytes=VMEM_LIMIT),
        name="conv",
    )(p, conv_w9)


def _bwd_chunk(n, n_ctx_chunks, n_chunks):
    return jnp.where(n < n_ctx_chunks, n_ctx_chunks - 1 - n, n_chunks + n_ctx_chunks - 1 - n)


def _cumsum_rows(x, reverse):
    n = x.shape[0]
    row = lax.broadcasted_iota(jnp.int32, (n, 1), 0)
    s = 1
    while s < n:
        if reverse:
            x = x + jnp.where(row < n - s, pltpu.roll(x, n - s, 0), 0.0)
        else:
            x = x + jnp.where(row >= s, pltpu.roll(x, s, 0), 0.0)
        s *= 2
    return x


def _tri_masks():
    r = lax.broadcasted_iota(jnp.int32, (CHUNK, CHUNK), 0)
    c = lax.broadcasted_iota(jnp.int32, (CHUNK, CHUNK), 1)
    return r >= c, r <= c, r == c


PREP_BATCH = 4
PREP_CHUNKS = TOK_BLOCK // CHUNK
SLAB = CHUNK + SUBLANE


def _substitute(n_scr, t_scr, reverse):
    n_seg = CHUNK // SUBLANE
    sub = lax.broadcasted_iota(jnp.int32, (SUBLANE, LANE), 0)
    for seg in (reversed(range(n_seg)) if reverse else range(n_seg)):
        js = range(SUBLANE * seg, CHUNK) if reverse else range(0, SUBLANE * (seg + 1))
        acc_groups = range(seg, n_seg) if reverse else range(0, seg + 1)
        t_scr[SUBLANE * seg * CHUNK:SUBLANE * (seg + 1) * CHUNK, :] = jnp.zeros((SUBLANE * CHUNK, LANE), F32)

        def body(step, carry, seg=seg, js=js, acc_groups=acc_groups):
            i = SUBLANE * seg + (SUBLANE - 1 - step if reverse else step)
            base = pl.multiple_of(i * CHUNK, CHUNK)
            acc = {g: jnp.zeros((SUBLANE, LANE), F32) for g in acc_groups}
            for j in js:
                nij = jnp.broadcast_to(n_scr[pl.ds(j * CHUNK + i, 1), :], (SUBLANE, LANE))
                for g in (range(j // SUBLANE, n_seg) if reverse else range(0, j // SUBLANE + 1)):
                    acc[g] = acc[g] + nij * t_scr[j * CHUNK + g * SUBLANE:j * CHUNK + (g + 1) * SUBLANE, :]
            acc[seg] = acc[seg] + jnp.where(sub + SUBLANE * seg == i, 1.0, 0.0)
            for g in acc_groups:
                t_scr[pl.ds(base + g * SUBLANE, SUBLANE), :] = acc[g]
            return carry

        lax.fori_loop(0, SUBLANE, body, 0)


def _gdn_prep_kernel(q_ref, k_ref, gc_ref, gr_ref, ta_ref, g_scr, n_scr, t_scr, r_scr):
    scale = GDN_DIM ** -0.5
    r = lax.broadcasted_iota(jnp.int32, (CHUNK, LANE), 0)
    c = lax.broadcasted_iota(jnp.int32, (CHUNK, LANE), 1)
    incl = (jnp.logical_and(r >= c, c < CHUNK), jnp.logical_and(r <= c, c < CHUNK))
    strict = (jnp.logical_and(r > c, c < CHUNK), jnp.logical_and(r < c, c < CHUNK))
    rl = lax.broadcasted_iota(jnp.int32, (LANE, LANE), 0)
    cl = lax.broadcasted_iota(jnp.int32, (LANE, LANE), 1)
    live = jnp.logical_and(rl < CHUNK, cl < CHUNK)
    tri_r = (jnp.logical_and(rl <= cl, live).astype(F32), jnp.logical_and(rl >= cl, live).astype(F32))
    k_zero = jnp.zeros((CHUNK, GDN_DIM), BF16)

    def per_chunk(fn):
        def outer(bb, carry):
            def inner(ch, carry2):
                fn(bb, ch, pl.multiple_of(ch * CHUNK, CHUNK), (bb * PREP_CHUNKS + ch) * GDN_HEADS)
                return carry2
            return lax.fori_loop(0, PREP_CHUNKS, inner, carry)
        lax.fori_loop(0, PREP_BATCH, outer, 0)

    def build(bb, ch, row0, p0):
        gcol = gc_ref[bb, pl.ds(row0, CHUNK), :]
        grow = gr_ref[bb, ch]
        cum_c = [_cumsum_rows(gcol, reverse=(d == 1)) for d in (0, 1)]
        cum_r = [_dot(grow, tri_r[d], precision=HIGHEST) for d in (0, 1)]
        for h in range(GDN_HEADS):
            sl = slice(h * GDN_DIM, (h + 1) * GDN_DIM)
            kb = k_ref[bb, pl.ds(row0, CHUNK), sl]
            qk = jnp.concatenate([q_ref[bb, pl.ds(row0, CHUNK), sl], kb], axis=0)
            gram = _dot_nt(qk, jnp.concatenate([kb, k_zero], axis=0))
            for d in (0, 1):
                bl = GDN_HEADS * d + h
                gl = 2 * GDN_HEADS + bl
                diff = cum_c[d][:, gl:gl + 1] - cum_r[d][gl:gl + 1, :]
                dec = jnp.exp(jnp.where(incl[d], diff, -jnp.inf))
                dec_t = jnp.exp(jnp.where(strict[1 - d], -diff, -jnp.inf))
                nm_t = -(gram[CHUNK:] * grow[bl:bl + 1, :] * dec_t)
                g_scr[d, pl.ds(pl.multiple_of((p0 + h) * SLAB, SUBLANE), CHUNK), :] = nm_t
                ta_ref[bb, ch, d, h, CHUNK:2 * CHUNK, :] = (gram[:CHUNK] * scale * dec)[:, :CHUNK].astype(BF16)

    per_chunk(build)

    for d in (0, 1):
        def to_lanes(i, carry, d=d):
            blk = g_scr[d, pl.ds(i, LANE, stride=SLAB), :]
            n_scr[pl.ds(pl.multiple_of(i * CHUNK, CHUNK), CHUNK), :] = blk.T[:CHUNK]
            return carry

        lax.fori_loop(0, CHUNK, to_lanes, 0, unroll=8)
        _substitute(n_scr, t_scr, reverse=(d == 1))

        def from_lanes(i, carry):
            blk = t_scr[pl.ds(pl.multiple_of(i * CHUNK, CHUNK), CHUNK), :]
            r_scr[pl.ds(i, LANE, stride=SLAB), :] = blk.T
            return carry

        lax.fori_loop(0, CHUNK, from_lanes, 0, unroll=8)

        def finish(bb, ch, row0, p0, d=d):
            grow = gr_ref[bb, ch]
            for h in range(GDN_HEADS):
                bl = GDN_HEADS * d + h
                t = r_scr[pl.ds(pl.multiple_of((p0 + h) * SLAB, SUBLANE), CHUNK), :]
                ta_ref[bb, ch, d, h, 0:CHUNK, :] = (t * grow[bl:bl + 1, :CHUNK]).astype(BF16)

        per_chunk(finish)


def _gdn_prep(qkv, gd, gd_rows):
    b, tot, _ = qkv.shape
    n_chunks = tot // CHUNK
    w = GDN_HEADS * GDN_DIM
    assert PREP_BATCH * PREP_CHUNKS * GDN_HEADS == LANE and b % PREP_BATCH == 0
    return pl.pallas_call(
        _gdn_prep_kernel,
        out_shape=jax.ShapeDtypeStruct((b, n_chunks, 2, GDN_HEADS, 2 * CHUNK, CHUNK), BF16),
        grid=(b // PREP_BATCH, tot // TOK_BLOCK),
        in_specs=[pl.BlockSpec((PREP_BATCH, TOK_BLOCK, w), lambda i, s: (i, s, 0)),
                  pl.BlockSpec((PREP_BATCH, TOK_BLOCK, w), lambda i, s: (i, s, 1)),
                  pl.BlockSpec((PREP_BATCH, TOK_BLOCK, LANE), lambda i, s: (i, s, 0)),
                  pl.BlockSpec((PREP_BATCH, PREP_CHUNKS, 4 * GDN_HEADS, LANE), lambda i, s: (i, s, 0, 0))],
        out_specs=pl.BlockSpec((PREP_BATCH, PREP_CHUNKS, 2, GDN_HEADS, 2 * CHUNK, CHUNK),
                               lambda i, s: (i, s, 0, 0, 0, 0)),
        scratch_shapes=[pltpu.VMEM((2, LANE * SLAB, LANE), F32),
                        pltpu.VMEM((CHUNK * CHUNK, LANE), F32),
                        pltpu.VMEM((CHUNK * CHUNK, LANE), F32),
                        pltpu.VMEM((LANE * SLAB, CHUNK), F32)],
        compiler_params=pltpu.CompilerParams(dimension_semantics=("arbitrary", "arbitrary"),
                                             vmem_limit_bytes=VMEM_LIMIT),
        name="gdn_prep",
    )(qkv, qkv, gd, gd_rows)


def _gdn_kernel(qf, kf, vf, qb, kb, vb, gcf, gcb, taf, tab, of, ob, s_ref):
    @pl.when(pl.program_id(1) == 0)
    def _():
        s_ref[...] = jnp.zeros(s_ref.shape, F32)

    scale = GDN_DIM ** -0.5
    dirs = ((qf, kf, vf, gcf, taf, of, CHUNK - 1), (qb, kb, vb, gcb, tab, ob, 0))
    for c in range(SCAN_CHUNKS):
        probs = []
        for d, (q_ref, k_ref, v_ref, gc_ref, ta_ref, o_ref, last) in enumerate(dirs):
            tc = c if d == 0 else SCAN_CHUNKS - 1 - c
            rows = slice(tc * CHUNK, (tc + 1) * CHUNK)
            cum_c = _cumsum_rows(gc_ref[0, rows, :], reverse=(d == 1))
            for h in range(GDN_HEADS):
                bl = GDN_HEADS * d + h
                gl = 2 * GDN_HEADS + bl
                sl = slice(h * GDN_DIM, (h + 1) * GDN_DIM)
                gc = cum_c[:, gl:gl + 1]
                probs.append(dict(bl=bl, sl=sl, gc=gc, g_tot=gc[last:last + 1, :], h=h, rows=rows, tc=tc,
                                  q_ref=q_ref, k_ref=k_ref, v_ref=v_ref, ta_ref=ta_ref, o_ref=o_ref))
        for p in probs:
            p["k"] = p["k_ref"][0, p["rows"], p["sl"]]
            qk = jnp.concatenate([p["q_ref"][0, p["rows"], p["sl"]], p["k"]], axis=0)
            p["s_old"] = s_ref[p["bl"]]
            p["qks"] = _dot(qk, p["s_old"].astype(BF16))
        for p in probs:
            p["eg"] = jnp.exp(p["gc"])
            x = p["v_ref"][0, p["rows"], p["sl"]].astype(F32) - p["eg"] * p["qks"][CHUNK:]
            p["ta"] = p["ta_ref"][0, p["tc"], 0, p["h"]]
            p["v_new"] = _dot(p["ta"][:CHUNK], x.astype(BF16)).astype(BF16)
        for p in probs:
            kd = p["k"].astype(F32) * jnp.exp(p["g_tot"] - p["gc"])
            s_ref[p["bl"]] = p["s_old"] * jnp.exp(p["g_tot"]) + _dot_tn(kd.astype(BF16), p["v_new"])
        for p in probs:
            o = (scale * p["eg"]) * p["qks"][:CHUNK] + _dot(p["ta"][CHUNK:], p["v_new"])
            p["o_ref"][0, p["rows"], p["sl"]] = o.astype(BF16)


def _gdn_scan(qkv, gd, ta, n_ctx_chunks):
    b, tot, _ = qkv.shape
    rows = SCAN_CHUNKS * CHUNK
    n_steps = tot // rows
    assert n_ctx_chunks % SCAN_CHUNKS == 0
    w = GDN_HEADS * GDN_DIM
    bwd = functools.partial(_bwd_chunk, n_ctx_chunks=n_ctx_chunks // SCAN_CHUNKS, n_chunks=n_steps)

    def tok(col, rev):
        if rev:
            return pl.BlockSpec((1, rows, w), lambda i, n: (i, bwd(n), col))
        return pl.BlockSpec((1, rows, w), lambda i, n: (i, n, col))

    gcol_f = pl.BlockSpec((1, rows, LANE), lambda i, n: (i, n, 0))
    gcol_b = pl.BlockSpec((1, rows, LANE), lambda i, n: (i, bwd(n), 0))
    ta_blk = (1, SCAN_CHUNKS, 1, GDN_HEADS, 2 * CHUNK, CHUNK)
    ta_f = pl.BlockSpec(ta_blk, lambda i, n: (i, n, 0, 0, 0, 0))
    ta_b = pl.BlockSpec(ta_blk, lambda i, n: (i, bwd(n), 1, 0, 0, 0))
    out = jax.ShapeDtypeStruct((b, tot, w), BF16)
    return pl.pallas_call(
        _gdn_kernel,
        out_shape=(out, out),
        grid=(b, n_steps),
        in_specs=[tok(0, False), tok(1, False), tok(2, False), tok(0, True), tok(1, True), tok(2, True),
                  gcol_f, gcol_b, ta_f, ta_b],
        out_specs=(tok(0, False), tok(0, True)),
        scratch_shapes=[pltpu.VMEM((2 * GDN_HEADS, GDN_DIM, GDN_DIM), F32)],
        compiler_params=pltpu.CompilerParams(dimension_semantics=("arbitrary", "arbitrary"),
                                             vmem_limit_bytes=VMEM_LIMIT),
        name="gdn",
    )(qkv, qkv, qkv, qkv, qkv, qkv, gd, gd, ta, ta)


def _gla_kernel(qkf, vf, qkb, vb, lrf, lrb, wf, wb, bf, bb, of, ob, s_ref):
    @pl.when(pl.program_id(1) == 0)
    def _():
        s_ref[...] = jnp.zeros(s_ref.shape, F32)

    scale = GLA_DK ** -0.5
    kw = GLA_HEADS * GLA_DK
    lower, upper, _ = _tri_masks()
    dirs = ((qkf, vf, lrf, wf, bf, of, lower, CHUNK // 2, CHUNK - 1),
            (qkb, vb, lrb, wb, bb, ob, upper, CHUNK - 1 - CHUNK // 2, 0))
    log_a = []
    for d, (_, _, lr_ref, w_ref, b_ref, _, _, _, _) in enumerate(dirs):
        lo = 4 * GDN_HEADS + d * GLA_RANK
        z = _dot(lr_ref[0, :, lo:lo + GLA_RANK], w_ref[...], precision=HIGHEST) + b_ref[...]
        log_a.append(-_softplus(-z) * (1.0 / GLA_NORMALIZER))
    for c in range(SCAN_CHUNKS):
        probs = []
        for d, (qk_ref, v_ref, _, _, _, o_ref, incl, centre, last) in enumerate(dirs):
            tc = c if d == 0 else SCAN_CHUNKS - 1 - c
            rows = slice(tc * CHUNK, (tc + 1) * CHUNK)
            cum = _cumsum_rows(log_a[d][rows], reverse=(d == 1))
            for h in range(GLA_HEADS):
                gc = cum[:, h * GLA_DK:(h + 1) * GLA_DK]
                probs.append(dict(idx=GLA_HEADS * d + h, gc=gc, incl=incl, o_ref=o_ref, rows=rows,
                                  g_mid=gc[centre:centre + 1, :], g_tot=gc[last:last + 1, :],
                                  q=qk_ref[0, rows, h * GLA_DK:(h + 1) * GLA_DK].astype(F32) * scale,
                                  k=qk_ref[0, rows, kw + h * GLA_DK:kw + (h + 1) * GLA_DK].astype(F32),
                                  v=v_ref[0, rows, h * GLA_DV:(h + 1) * GLA_DV],
                                  vsl=slice(h * GLA_DV, (h + 1) * GLA_DV)))
        for p in probs:
            attn = _dot_nt((p["q"] * jnp.exp(p["gc"] - p["g_mid"])).astype(BF16),
                           (p["k"] * jnp.exp(p["g_mid"] - p["gc"])).astype(BF16))
            p["attn"] = jnp.where(p["incl"], attn, 0.0).astype(BF16)
            p["s_old"] = s_ref[p["idx"]]
        for p in probs:
            o = (_dot(p["attn"], p["v"])
                 + _dot_nt((p["q"] * jnp.exp(p["gc"])).astype(BF16), p["s_old"].astype(BF16)))
            p["o_ref"][0, p["rows"], p["vsl"]] = o.astype(BF16)
        for p in probs:
            kd = p["k"] * jnp.exp(p["g_tot"] - p["gc"])
            s_ref[p["idx"]] = p["s_old"] * jnp.exp(p["g_tot"]) + _dot_tn(p["v"], kd.astype(BF16))


def _gla_scan(p, lr, wf, wb, bf, bb, n_ctx_chunks, qk_col, v_col):
    b, tot, _ = p.shape
    rows = SCAN_CHUNKS * CHUNK
    n_steps = tot // rows
    assert n_ctx_chunks % SCAN_CHUNKS == 0
    w = GLA_HEADS * GLA_DV
    bwd = functools.partial(_bwd_chunk, n_ctx_chunks=n_ctx_chunks // SCAN_CHUNKS, n_chunks=n_steps)

    def tok(width, col, rev):
        if rev:
            return pl.BlockSpec((1, rows, width), lambda i, n: (i, bwd(n), col))
        return pl.BlockSpec((1, rows, width), lambda i, n: (i, n, col))

    const2 = lambda i, n: (0, 0)
    out = jax.ShapeDtypeStruct((b, tot, w), BF16)
    return pl.pallas_call(
        _gla_kernel,
        out_shape=(out, out),
        grid=(b, n_steps),
        in_specs=[tok(w, qk_col, False), tok(w, v_col, False), tok(w, qk_col, True), tok(w, v_col, True),
                  tok(LANE, 0, False), tok(LANE, 0, True),
                  pl.BlockSpec(wf.shape, const2), pl.BlockSpec(wb.shape, const2),
                  pl.BlockSpec(bf.shape, const2), pl.BlockSpec(bb.shape, const2)],
        out_specs=(tok(w, 0, False), tok(w, 0, True)),
        scratch_shapes=[pltpu.VMEM((2 * GLA_HEADS, GLA_DV, GLA_DK), F32)],
        compiler_params=pltpu.CompilerParams(dimension_semantics=("arbitrary", "arbitrary"),
                                             vmem_limit_bytes=VMEM_LIMIT),
        name="gla",
    )(p, p, p, p, lr, lr, wf, wb, bf, bb)


def _head_rms(o, g, width):
    outs = []
    for h in range(o.shape[-1] // width):
        oh = o[:, h * width:(h + 1) * width]
        ms = jnp.mean(oh * oh, axis=-1, keepdims=True)
        outs.append(oh * lax.rsqrt(ms + EPS))
    return jnp.concatenate(outs, axis=-1) * g


def _merge_kernel(oaf, oab, obf, obb, z_ref, r_ref, ga_ref, gb_ref, x_ref, g1_ref, sh2_ref, sc2_ref,
                  na_ref, nb_ref, nf_ref, woa_ref, wob_ref, wo_ref, wr_ref, br_ref,
                  x1_ref, h2_ref, lg_ref):
    f32 = lambda ref: ref[0].astype(F32)
    ya_in = _head_rms(f32(oaf) + f32(oab), na_ref[...], GDN_DIM) * _silu(f32(z_ref))
    ya = _dot(ya_in.astype(BF16), woa_ref[...])
    yb_in = _head_rms(f32(obf) + f32(obb), nb_ref[...], GLA_DV) * _silu(f32(r_ref))
    yb = _dot(yb_in.astype(BF16), wob_ref[...])
    y = jax.nn.sigmoid(f32(ga_ref)) * ya + jax.nn.sigmoid(f32(gb_ref)) * yb
    x1 = x_ref[0] + g1_ref[0] * _dot(y.astype(BF16), wo_ref[...])
    x1_ref[0] = x1
    ms = jnp.mean(x1 * x1, axis=-1, keepdims=True)
    h2 = (x1 * lax.rsqrt(ms + EPS) * nf_ref[...]) * (1.0 + sc2_ref[0]) + sh2_ref[0]
    _store_tile_rows(h2_ref, h2)
    lg_ref[0] = _dot(h2, wr_ref[...], precision=HIGHEST) + br_ref[...]


def _merge(oaf, oab, obf, obb, p, x, g1, sh2, sc2, na, nb, nf, woa, wob, wo, wr, br, cols):
    b, l, d = x.shape
    z_col, r_col, ga_col, gb_col = cols
    lat = lambda i, s: (i, s + 1, 0)
    row3 = lambda i, s: (i, 0, 0)
    const2 = lambda i, s: (0, 0)
    blk = lambda i, s: (i, s, 0)
    tb = (1, TOK_BLOCK, d)
    pcol = lambda c: pl.BlockSpec(tb, lambda i, s: (i, s + 1, c))
    vec = pl.BlockSpec((1, 1, d), row3)
    cvec = pl.BlockSpec((1, d), const2)
    wspec = pl.BlockSpec((d, d), const2)
    groups = d // LANE
    n_blk = l // TOK_BLOCK
    return pl.pallas_call(
        _merge_kernel,
        out_shape=(jax.ShapeDtypeStruct((b, l, d), F32),
                   jax.ShapeDtypeStruct((b * l * groups, LANE), F32),
                   jax.ShapeDtypeStruct((b, l, LANE), F32)),
        grid=(b, n_blk),
        in_specs=[pl.BlockSpec(tb, lat)] * 4 + [pcol(z_col), pcol(r_col), pcol(ga_col), pcol(gb_col),
                  pl.BlockSpec(tb, blk), vec, vec, vec, cvec, cvec, cvec, wspec, wspec, wspec,
                  pl.BlockSpec((d, LANE), const2), pl.BlockSpec((1, LANE), const2)],
        out_specs=(pl.BlockSpec(tb, blk),
                   pl.BlockSpec((TOK_BLOCK * groups, LANE), lambda i, s: (i * n_blk + s, 0)),
                   pl.BlockSpec((1, TOK_BLOCK, LANE), blk)),
        compiler_params=pltpu.CompilerParams(dimension_semantics=("arbitrary", "arbitrary"),
                                             vmem_limit_bytes=VMEM_LIMIT),
        name="merge",
    )(oaf, oab, obf, obb, p, p, p, p, x, g1, sh2, sc2, na, nb, nf, woa, wob, wo, wr, br)


def _row_copy(src_ref, dst_ref, sem, src_row, dst_row):
    first = lambda row: row * SUBLANE if isinstance(row, int) else pl.multiple_of(row * SUBLANE, SUBLANE)
    src, dst = first(src_row), first(dst_row)
    return pltpu.make_async_copy(src_ref.at[pl.ds(src, SUBLANE)], dst_ref.at[pl.ds(dst, SUBLANE)], sem)


def _gather_step(step, n_steps, rows, phases, two_queues, idx_hbm, src_hbm, idx_smem, buf, idx_sem, row_sem):
    part = rows // phases

    def idx_copy(s):
        return pltpu.make_async_copy(idx_hbm.at[s], idx_smem.at[s % 2], idx_sem.at[s % 2])

    def issue(s, lo, hi):
        slot = s % 2
        for r in range(lo, hi):
            _row_copy(src_hbm, buf.at[slot], row_sem.at[slot], idx_smem[slot, r], r).start(
                priority=r % 2 if two_queues else 0)

    @pl.when(step == 0)
    def _():
        idx_copy(0).start()
        idx_copy(0).wait()

        def first_rows(r, carry):
            _row_copy(src_hbm, buf.at[0], row_sem.at[0], idx_smem[0, r], r).start()
            return carry

        lax.fori_loop(0, rows, first_rows, 0)

        @pl.when(n_steps > 1)
        def _():
            idx_copy(1).start()

    @pl.when(step + 1 < n_steps)
    def _():
        idx_copy(step + 1).wait()

    def prefetch(j):
        @pl.when(step + 1 < n_steps)
        def _():
            issue(step + 1, j * part, (j + 1) * part)

        if j == phases - 1:
            @pl.when(step + 2 < n_steps)
            def _():
                idx_copy(step + 2).start()

    slot = step % 2
    pltpu.make_async_copy(src_hbm.at[pl.ds(0, rows * SUBLANE)], buf.at[slot], row_sem.at[slot]).wait()
    return buf.at[slot], prefetch


def _gather_scratch(rows):
    return [pltpu.SMEM((2, rows), jnp.int32), pltpu.VMEM((2, rows * SUBLANE, LANE), F32),
            pltpu.SemaphoreType.DMA((2,)), pltpu.SemaphoreType.DMA((2,))]


def _expert_kernel(be_ref, tok_ref, h_ref, wgu_ref, bgu_ref, wd_ref, bd_ref, y_ref, wgu_bf, wd_bf, *gather_scratch):
    step = pl.program_id(0)
    de = wd_ref.shape[1]
    groups = wgu_ref.shape[1] // LANE

    @pl.when(jnp.logical_or(step == 0, be_ref[step] != be_ref[jnp.maximum(step - 1, 0)]))
    def _():
        wgu_bf[...] = wgu_ref[0].astype(BF16)
        wd_bf[...] = wd_ref[0].astype(BF16)

    x_ref, prefetch = _gather_step(step, pl.num_programs(0), EXPERT_BLOCK, 2, True,
                                   tok_ref, h_ref, *gather_scratch)
    prefetch(0)
    x = _load_tile_rows(x_ref, EXPERT_BLOCK, groups)
    gu = _dot(x.astype(BF16), wgu_bf[...]) + bgu_ref[0]
    glu = jnp.minimum(gu[:, :de], SWIGLU_LIMIT)
    lin = jnp.clip(gu[:, de:], -SWIGLU_LIMIT, SWIGLU_LIMIT)
    act = (glu * jax.nn.sigmoid(SWIGLU_ALPHA * glu) * (lin + 1.0)).astype(BF16)
    prefetch(1)
    _store_tile_rows(y_ref, _dot(act, wd_bf[...]) + bd_ref[0])


def _experts(h, slot_tok, block_expert, w_gu, b_gu, w_down, b_down):
    ne, d, de2 = w_gu.shape
    de = de2 // 2
    groups = d // LANE
    assert groups == SUBLANE
    n_slots = slot_tok.shape[0]
    blk_rows = EXPERT_BLOCK * groups
    grid_spec = pltpu.PrefetchScalarGridSpec(
        num_scalar_prefetch=1,
        grid=(n_slots // EXPERT_BLOCK,),
        in_specs=[pl.BlockSpec(memory_space=pl.ANY), pl.BlockSpec(memory_space=pl.ANY),
                  pl.BlockSpec((1, d, de2), lambda i, be: (be[i], 0, 0)),
                  pl.BlockSpec((1, 1, de2), lambda i, be: (be[i], 0, 0)),
                  pl.BlockSpec((1, de, d), lambda i, be: (be[i], 0, 0)),
                  pl.BlockSpec((1, 1, d), lambda i, be: (be[i], 0, 0))],
        out_specs=pl.BlockSpec((blk_rows, LANE), lambda i, be: (i, 0)),
        scratch_shapes=[pltpu.VMEM((d, de2), BF16), pltpu.VMEM((de, d), BF16)] + _gather_scratch(EXPERT_BLOCK),
    )
    return pl.pallas_call(
        _expert_kernel,
        out_shape=jax.ShapeDtypeStruct((n_slots * groups, LANE), F32),
        grid_spec=grid_spec,
        compiler_params=pltpu.CompilerParams(dimension_semantics=("arbitrary",), vmem_limit_bytes=VMEM_LIMIT),
        name="experts",
    )(block_expert, slot_tok.reshape(n_slots // EXPERT_BLOCK, EXPERT_BLOCK), h,
      w_gu, b_gu.reshape(ne, 1, de2), w_down, b_down.reshape(ne, 1, d))


def _combine_kernel(dest_ref, yb_ref, x1_ref, p_ref, g2_ref, fg_ref, o_ref, *gather_scratch):
    groups = x1_ref.shape[-1] // LANE
    step = pl.program_id(0) * pl.num_programs(1) + pl.program_id(1)
    n_steps = pl.num_programs(0) * pl.num_programs(1)
    y4_ref, prefetch = _gather_step(step, n_steps, TOK_BLOCK * TOP_K, 1, True, dest_ref, yb_ref, *gather_scratch)
    prefetch(0)
    p = p_ref[0]
    y = None
    for k in range(TOP_K):
        yk = p[:, k:k + 1] * _load_tile_rows(y4_ref, TOK_BLOCK, groups, first=k * groups, stride=TOP_K * groups)
        y = yk if y is None else y + yk
    x2 = x1_ref[0] + g2_ref[0] * y
    ms = jnp.mean(x2 * x2, axis=-1, keepdims=True)
    o_ref[0] = x2 * lax.rsqrt(ms + EPS) * fg_ref[...]


def _combine(x1, yb, dest, top_p, g2, final_g):
    b, l, d = x1.shape
    assert d // LANE == SUBLANE
    blk = lambda i, s: (i, s, 0)
    n_blk = l // TOK_BLOCK
    rows = TOK_BLOCK * TOP_K
    return pl.pallas_call(
        _combine_kernel,
        out_shape=jax.ShapeDtypeStruct((b, l, d), F32),
        grid=(b, n_blk),
        in_specs=[pl.BlockSpec(memory_space=pl.ANY), pl.BlockSpec(memory_space=pl.ANY),
                  pl.BlockSpec((1, TOK_BLOCK, d), blk),
                  pl.BlockSpec((1, TOK_BLOCK, TOP_K), blk),
                  pl.BlockSpec((1, 1, d), lambda i, s: (i, 0, 0)),
                  pl.BlockSpec((1, d), lambda i, s: (0, 0))],
        out_specs=pl.BlockSpec((1, TOK_BLOCK, d), blk),
        scratch_shapes=_gather_scratch(rows),
        compiler_params=pltpu.CompilerParams(dimension_semantics=("arbitrary", "arbitrary"),
                                             vmem_limit_bytes=VMEM_LIMIT),
        name="combine",
    )(dest.reshape(b * n_blk, rows), yb, x1, top_p, g2, final_g)


def _routing(logits):
    t = logits.shape[0]
    n_assign = t * TOP_K
    n_slots = -(-(n_assign + N_EXPERTS * (EXPERT_BLOCK - 1)) // EXPERT_BLOCK) * EXPERT_BLOCK
    top_logit, top_idx = lax.top_k(logits, TOP_K)
    top_p = jax.nn.softmax(top_logit, axis=-1)
    flat_e = top_idx.reshape(-1).astype(jnp.int32)
    experts = jnp.arange(N_EXPERTS, dtype=jnp.int32)
    lookup = lambda table, e: jnp.sum(jnp.where(e[:, None] == experts[None, :], table[None, :], 0), axis=1)
    pos = jnp.arange(n_assign, dtype=jnp.int32)
    assert N_EXPERTS * n_assign < 2 ** 31
    sorted_key = lax.sort(flat_e * n_assign + pos)
    sorted_e, order = sorted_key // n_assign, sorted_key % n_assign
    counts = jnp.sum((flat_e[:, None] == experts[None, :]).astype(jnp.int32), axis=0)
    start = jnp.cumsum(counts) - counts
    padded = (counts + EXPERT_BLOCK - 1) // EXPERT_BLOCK * EXPERT_BLOCK
    pad_end = jnp.cumsum(padded)
    pad_start = pad_end - padded
    dest_sorted = pos + lookup(pad_start - start, sorted_e)
    _, dest = lax.sort((order, dest_sorted), num_keys=1)
    block_start = jnp.arange(n_slots // EXPERT_BLOCK, dtype=jnp.int32) * EXPERT_BLOCK
    block_expert = jnp.minimum(jnp.sum((pad_end[None, :] <= block_start[:, None]).astype(jnp.int32), axis=1),
                               N_EXPERTS - 1)
    slot_e = jnp.repeat(block_expert, EXPERT_BLOCK)
    within = jnp.arange(n_slots, dtype=jnp.int32) - lookup(pad_start, slot_e)
    src = jnp.clip(lookup(start, slot_e) + within, 0, n_assign - 1)
    slot_tok = jnp.where(within < lookup(counts, slot_e), order[src] // TOP_K, 0)
    return slot_tok, dest, block_expert, top_p


def kernel(x, c, ctx, c_ctx, w_mod, b_mod, norm_mix_g, norm_ffn_g, w_in, conv_w, a_log_f, a_log_b, dt_bias_f,
           dt_bias_b, gdn_norm_g, gla_gate_w_f, gla_gate_w_b, gla_gate_b_f, gla_gate_b_b, gla_norm_g, w_out_a,
           w_out_b, w_out, w_router, b_router, w_gu, b_gu, w_down, b_down, final_norm_g):
    assert w_mod.shape[0] == 1, "single-layer kernel"
    b, l, d = x.shape
    n_ctx = ctx.shape[1]
    assert n_ctx == TOK_BLOCK and l % CONV_ROWS == 0 and d == GDN_HEADS * GDN_DIM
    gw = GDN_HEADS * GDN_DIM
    kw = GLA_HEADS * GLA_DK
    vw = GLA_HEADS * GLA_DV

    pad_rows = -(b + 1) % 8
    cc = jnp.concatenate([c, c_ctx[None], jnp.zeros((pad_rows, d), F32)], axis=0)
    mod = _modulation(cc, w_mod[0], b_mod[0])
    sh1, sc1, g1, sh2, sc2, g2 = [mod[:b, i * d:(i + 1) * d].reshape(b, 1, d) for i in range(6)]
    sh1c, sc1c = mod[b:b + 1, 0:d], mod[b:b + 1, d:2 * d]

    offs = {}
    o = 0
    for name, width in (("qa", gw), ("ka", gw), ("va", gw), ("za", gw), ("beta", 2 * GDN_HEADS),
                        ("dec", 2 * GDN_HEADS), ("qb", kw), ("kb", kw), ("vb", vw), ("rb", vw),
                        ("lr", 2 * GLA_RANK), ("gates", 2 * d)):
        offs[name] = (o, o + width)
        o += width
    wi = w_in[0]
    cols = lambda n: wi[:, offs[n][0]:offs[n][1]]
    w_main = jnp.concatenate([cols(n) for n in ("qa", "ka", "va", "za", "qb", "kb", "vb", "rb", "gates")],
                             axis=1).astype(BF16)
    zpad = lambda n: jnp.zeros((d, n), F32)
    w_small = jnp.concatenate([cols("beta"), cols("dec"), cols("lr"),
                               zpad(LANE - 4 * GDN_HEADS - 2 * GLA_RANK)], axis=1)
    w_small_hi = w_small.astype(BF16)
    w_small = jnp.stack([w_small_hi, (w_small - w_small_hi.astype(F32)).astype(BF16)])
    lane_pad = lambda v, lo: jnp.zeros((1, LANE), F32).at[0, lo:lo + v.shape[0]].set(v)
    coef_a = lane_pad(-jnp.exp(jnp.concatenate([a_log_f[0], a_log_b[0]])), 2 * GDN_HEADS)
    coef_b = lane_pad(jnp.concatenate([dt_bias_f[0], dt_bias_b[0]]), 2 * GDN_HEADS)

    p, gd = _input_projection(x, ctx, sh1, sc1, sh1c, sc1c, norm_mix_g, w_main, w_small, coef_a, coef_b)
    tot = p.shape[1]
    n_chunks = tot // CHUNK
    n_ctx_chunks = n_ctx // CHUNK

    qkv = _conv_qkv(p, conv_w[0].reshape(9, 3 * gw), n_ctx, l)
    gd_rows = gd[:, :, :4 * GDN_HEADS].reshape(b, n_chunks, CHUNK, 4 * GDN_HEADS).transpose(0, 1, 3, 2)
    gd_rows = jnp.pad(gd_rows, ((0, 0), (0, 0), (0, 0), (0, LANE - CHUNK)))
    oaf, oab = _gdn_scan(qkv, gd, _gdn_prep(qkv, gd, gd_rows), n_ctx_chunks)

    obf, obb = _gla_scan(p, gd, gla_gate_w_f[0], gla_gate_w_b[0], gla_gate_b_f, gla_gate_b_b, n_ctx_chunks,
                         qk_col=4, v_col=5)

    wr = jnp.zeros((d, LANE), F32).at[:, :N_EXPERTS].set(w_router[0])
    br = jnp.zeros((1, LANE), F32).at[0, :N_EXPERTS].set(b_router[0])
    x1, h2, logits = _merge(oaf, oab, obf, obb, p, x, g1, sh2, sc2,
                            jnp.tile(gdn_norm_g, (1, GDN_HEADS)), jnp.tile(gla_norm_g, (1, GLA_HEADS)), norm_ffn_g,
                            w_out_a[0].astype(BF16), w_out_b[0].astype(BF16), w_out[0].astype(BF16), wr, br,
                            cols=(3, 6, 7, 8))

    return _channel_mixing(x1, h2, logits, g2, w_gu, b_gu, w_down, b_down, final_norm_g)


def _channel_mixing(x1, h2, logits, g2, w_gu, b_gu, w_down, b_down, final_norm_g):
    b, l, d = x1.shape
    t = b * l
    slot_tok, dest, block_expert, top_p = _routing(logits.reshape(t, LANE)[:, :N_EXPERTS])
    yb = _experts(h2, slot_tok, block_expert, w_gu[0], b_gu[0], w_down[0], b_down[0])
    return _combine(x1, yb, dest, top_p.reshape(b, l, TOP_K), g2, final_norm_g.reshape(1, d))
```
